```python
import math
import jax, jax.numpy as jnp
from jax import lax
import numpy as np

D_MODEL = 4096
BATCH = 2
SEQ = 4096
DEPTH = 2
DEC_BATCH = 1
DEC_SEQ = 8192
PAST_LEN = 128

HEAD_DIM = 128
N_HEADS = D_MODEL // HEAD_DIM
N_KV_HEADS = N_HEADS // 4
GROUP = N_HEADS // N_KV_HEADS
Q_DIM = N_HEADS * HEAD_DIM
KV_DIM = N_KV_HEADS * HEAD_DIM
CONV_DIM = D_MODEL
WINDOW = 128
BLOCK = 128
N_META = 16
LEAD_PAD = BLOCK - N_META
N_BUCKETS = 32
MAX_DISTANCE = 128
FFN_DIM = ((8 * D_MODEL // 3 + 255) // 256) * 256
EPS = 1e-6

Q_END = Q_DIM
K_END = Q_END + KV_DIM
V_END = K_END + KV_DIM
CB_END = V_END + CONV_DIM
CC_END = CB_END + CONV_DIM
CH_END = CC_END + CONV_DIM
GA_END = CH_END + D_MODEL
IN_DIM = GA_END + D_MODEL

kernel_name = "hybrid_swa_shortconv_encoder"


def rmsnorm(x, g):
    xf = x.astype(jnp.float32)
    y = xf * lax.rsqrt(jnp.mean(xf * xf, axis=-1, keepdims=True) + EPS)
    return (y * g.astype(jnp.float32)).astype(x.dtype)


def t5_bucket(rel):
    half = N_BUCKETS // 2
    exact = half // 2
    n = jnp.abs(rel)
    n_f = jnp.maximum(n, 1).astype(jnp.float32)
    large = exact + (jnp.log(n_f / exact) / math.log(MAX_DISTANCE / exact) * (half - exact)).astype(jnp.int32)
    large = jnp.minimum(large, half - 1)
    return jnp.where(rel > 0, half, 0) + jnp.where(n < exact, n, large)


def banded_attention(q, k, v, q_g, k_g, sink, rel_bias):
    B, L = q.shape[0], q.shape[1]
    q = rmsnorm(q, q_g)
    k = rmsnorm(k, k_g)
    Lp = L + LEAD_PAD
    nb = Lp // BLOCK

    def blocks(t):
        t = jnp.pad(t, ((0, 0), (LEAD_PAD, 0), (0, 0), (0, 0)))
        return t.reshape((B, nb, BLOCK) + t.shape[2:])

    def band(t):
        tb = jnp.pad(blocks(t), ((0, 0), (1, 1), (0, 0), (0, 0), (0, 0)))
        return jnp.concatenate([tb[:, :-2], tb[:, 1:-1], tb[:, 2:]], axis=2)

    qb = blocks(q).reshape(B, nb, BLOCK, N_KV_HEADS, GROUP, HEAD_DIM)
    kb, vb = band(k), band(v)
    k_meta, v_meta = k[:, :N_META], v[:, :N_META]
    scale = HEAD_DIM ** -0.5

    s_band = jnp.einsum('bnqkgd,bnskd->bnkgqs', qb, kb).astype(jnp.float32) * scale
    s_meta = jnp.einsum('bnqkgd,bmkd->bnkgqm', qb, k_meta).astype(jnp.float32) * scale

    blk = jnp.arange(nb)[:, None, None]
    qi = jnp.arange(BLOCK)
    sj = jnp.arange(3 * BLOCK)
    q_pos = blk * BLOCK + qi[None, :, None] - LEAD_PAD
    k_pos = (blk - 1) * BLOCK + sj[None, None, :] - LEAD_PAD
    band_ok = (k_pos >= N_META) & (k_pos < L) & (jnp.abs(k_pos - q_pos) <= WINDOW)

    table = rel_bias.astype(jnp.float32)
    rel_band = sj[None, :] - BLOCK - qi[:, None]
    bias_band = table[t5_bucket(rel_band)].transpose(2, 0, 1).reshape(N_KV_HEADS, GROUP, BLOCK, 3 * BLOCK)
    rel_meta = jnp.arange(N_META)[None, None, :] - q_pos
    bias_meta = table[t5_bucket(rel_meta)].transpose(0, 3, 1, 2).reshape(nb, N_KV_HEADS, GROUP, BLOCK, N_META)

    s_band = jnp.where(band_ok[None, :, None, None], s_band + bias_band, -jnp.inf)
    s_meta = s_meta + bias_meta[None]
    sink_l = sink.astype(jnp.float32).reshape(N_KV_HEADS, GROUP, 1, 1)
    m = jnp.maximum(jnp.maximum(s_band.max(-1, keepdims=True), s_meta.max(-1, keepdims=True)), sink_l)
    p_band = jnp.exp(s_band - m)
    p_meta = jnp.exp(s_meta - m)
    inv = 1.0 / (p_band.sum(-1, keepdims=True) + p_meta.sum(-1, keepdims=True) + jnp.exp(sink_l - m))
    o = (jnp.einsum('bnkgqs,bnskd->bnqkgd', (p_band * inv).astype(v.dtype), vb)
         + jnp.einsum('bnkgqm,bmkd->bnqkgd', (p_meta * inv).astype(v.dtype), v_meta))
    return o.reshape(B, Lp, Q_DIM)[:, LEAD_PAD:]


def short_conv(gate_b, gate_c, h, conv_w):
    u = gate_c * h
    up = jnp.pad(u, ((0, 0), (1, 1), (0, 0)))
    y = conv_w[0] * up[:, :-2] + conv_w[1] * up[:, 1:-1] + conv_w[2] * up[:, 2:]
    return gate_b * y


def encoder_layer(x, rel_bias, norm1_g, w_in, q_norm_g, k_norm_g, attn_sink, conv_w,
                  branch_norm_a, branch_norm_c, w_out, norm2_g, w_ffn_gate, w_ffn_up, w_ffn_down):
    B, L, _ = x.shape
    xn = rmsnorm(x, norm1_g)
    z = xn @ w_in
    q, k, v, gb, gc, hc, ga, gcv = jnp.split(z, [Q_END, K_END, V_END, CB_END, CC_END, CH_END, GA_END], axis=-1)
    attn = banded_attention(q.reshape(B, L, N_HEADS, HEAD_DIM),
                            k.reshape(B, L, N_KV_HEADS, HEAD_DIM),
                            v.reshape(B, L, N_KV_HEADS, HEAD_DIM),
                            q_norm_g, k_norm_g, attn_sink, rel_bias)
    conv = short_conv(gb, gc, hc, conv_w)
    merged = (jax.nn.sigmoid(ga) * rmsnorm(attn, branch_norm_a)
              + jax.nn.sigmoid(gcv) * rmsnorm(conv, branch_norm_c))
    h = x + merged @ w_out
    hn = rmsnorm(h, norm2_g)
    return h + (jax.nn.silu(hn @ w_ffn_gate) * (hn @ w_ffn_up)) @ w_ffn_down


def encoder_trunk(x, meta_tokens, rel_bias, layer_params):
    B = x.shape[0]
    meta = jnp.broadcast_to(meta_tokens.astype(x.dtype)[None], (B, N_META, D_MODEL))
    h = jnp.concatenate([meta, x], axis=1)
    for l in range(DEPTH):
        h = encoder_layer(h, rel_bias, *[p[l] for p in layer_params])
    return h[:, N_META:]


def setup_inputs(seed: int = 0) -> dict:
    key = jax.random.key(seed)
    ks = jax.random.split(key, 20)
    f32 = jnp.float32
    nrm = lambda k, shape, s: jax.random.normal(k, shape, f32) * s
    gain = lambda k, shape: 1.0 + 0.02 * jax.random.normal(k, shape, f32)
    return {
        "x_prompt": nrm(ks[0], (BATCH, SEQ, D_MODEL), 1.0),
        "x_sample": nrm(ks[1], (DEC_BATCH, DEC_SEQ, D_MODEL), 1.0),
        "meta_tokens": nrm(ks[2], (N_META, D_MODEL), 1.0),
        "rel_bias": nrm(ks[3], (N_BUCKETS, N_HEADS), 0.5),
        "norm1_g": gain(ks[4], (DEPTH, D_MODEL)),
        "w_in": nrm(ks[5], (DEPTH, D_MODEL, IN_DIM), D_MODEL ** -0.5),
        "q_norm_g": gain(ks[6], (DEPTH, HEAD_DIM)),
        "k_norm_g": gain(ks[7], (DEPTH, HEAD_DIM)),
        "attn_sink": nrm(ks[8], (DEPTH, N_HEADS), 0.5),
        "conv_w": nrm(ks[9], (DEPTH, 3, CONV_DIM), 3 ** -0.5),
        "branch_norm_a": gain(ks[10], (DEPTH, D_MODEL)),
        "branch_norm_c": gain(ks[11], (DEPTH, D_MODEL)),
        "w_out": nrm(ks[12], (DEPTH, D_MODEL, D_MODEL), D_MODEL ** -0.5),
        "norm2_g": gain(ks[13], (DEPTH, D_MODEL)),
        "w_ffn_gate": nrm(ks[14], (DEPTH, D_MODEL, FFN_DIM), D_MODEL ** -0.5),
        "w_ffn_up": nrm(ks[15], (DEPTH, D_MODEL, FFN_DIM), D_MODEL ** -0.5),
        "w_ffn_down": nrm(ks[16], (DEPTH, FFN_DIM, D_MODEL), FFN_DIM ** -0.5),
    }


def reference(x_prompt, x_sample, meta_tokens, rel_bias, norm1_g, w_in, q_norm_g, k_norm_g, attn_sink,
              conv_w, branch_norm_a, branch_norm_c, w_out, norm2_g, w_ffn_gate, w_ffn_up, w_ffn_down):
    layer_params = (norm1_g, w_in, q_norm_g, k_norm_g, attn_sink, conv_w, branch_norm_a, branch_norm_c,
                    w_out, norm2_g, w_ffn_gate, w_ffn_up, w_ffn_down)
    y_prompt = encoder_trunk(x_prompt, meta_tokens, rel_bias, layer_params)
    y_sample = encoder_trunk(x_sample, meta_tokens, rel_bias, layer_params)
    return (y_prompt, y_sample)
```

```python
import functools
import math

import jax
import jax.numpy as jnp
from jax import lax
from jax.experimental import pallas as pl
from jax.experimental.pallas import tpu as pltpu

HEAD_DIM = 128
GROUP = 4
BLOCK = 128
MAX_DISTANCE = 128
EPS = 1e-6

V7X_VMEM_BYTES = 64 << 20
V7X_LANES = 128
BF16_SUBLANES = 16
F32_SUBLANES = 8
VMEM_SLACK_BYTES = 10 << 20

F32 = jnp.float32
BF16 = jnp.bfloat16


def _params(dims, window_bytes):
    limit = min(2 * window_bytes + VMEM_SLACK_BYTES, V7X_VMEM_BYTES - (6 << 20))
    return pltpu.CompilerParams(dimension_semantics=dims, vmem_limit_bytes=int(limit))


def _divisor_tile(n, cap, unit):
    if n <= cap:
        return n
    t = (cap // unit) * unit
    while t >= unit:
        if n % t == 0:
            return t
        t -= unit
    raise ValueError(f"no tile for {n}")


def _nbytes(shape, dtype):
    return math.prod(shape) * jnp.dtype(dtype).itemsize


def _rms(x, g):
    ms = jnp.mean(x * x, axis=-1, keepdims=True)
    return x * lax.rsqrt(ms + EPS) * g


def _rmsnorm_kernel(x_ref, g_ref, o_ref):
    o_ref[...] = _rms(x_ref[...], g_ref[...]).astype(o_ref.dtype)


def _rmsnorm(x, g):
    m, d = x.shape
    bm = _divisor_tile(m, 256, BF16_SUBLANES)
    return pl.pallas_call(
        _rmsnorm_kernel,
        grid=(m // bm,),
        in_specs=[pl.BlockSpec((bm, d), lambda i: (i, 0)),
                  pl.BlockSpec((1, d), lambda i: (0, 0))],
        out_specs=pl.BlockSpec((bm, d), lambda i: (i, 0)),
        out_shape=jax.ShapeDtypeStruct((m, d), BF16),
        compiler_params=_params(("parallel",), _nbytes((bm, d), F32) + _nbytes((bm, d), BF16)),
        name="rmsnorm",
    )(x, g.reshape(1, d))


def _mm_kernel(a_ref, b_ref, o_ref):
    o_ref[...] = jnp.dot(a_ref[...], b_ref[...], preferred_element_type=F32).astype(o_ref.dtype)


def _mm_res_kernel(a_ref, b_ref, r_ref, o_ref):
    @pl.when(pl.program_id(2) == 0)
    def _():
        o_ref[...] = r_ref[...]

    o_ref[...] += jnp.dot(a_ref[...], b_ref[...], preferred_element_type=F32)


def _gateup_kernel(a_ref, wg_ref, wu_ref, o_ref):
    a = a_ref[...]
    g = jnp.dot(a, wg_ref[...], preferred_element_type=F32)
    u = jnp.dot(a, wu_ref[...], preferred_element_type=F32)
    o_ref[...] = (g * jax.nn.sigmoid(g) * u).astype(o_ref.dtype)


def _matmul(a, b, out_dtype):
    m, k = a.shape
    n = b.shape[1]
    bm = _divisor_tile(m, 1024, BF16_SUBLANES)
    bn = _divisor_tile(n, 1024, V7X_LANES)
    win = _nbytes((bm, k), BF16) + _nbytes((k, bn), BF16) + _nbytes((bm, bn), out_dtype)
    return pl.pallas_call(
        _mm_kernel,
        grid=(m // bm, n // bn),
        in_specs=[pl.BlockSpec((bm, k), lambda i, j: (i, 0)),
                  pl.BlockSpec((k, bn), lambda i, j: (0, j))],
        out_specs=pl.BlockSpec((bm, bn), lambda i, j: (i, j)),
        out_shape=jax.ShapeDtypeStruct((m, n), out_dtype),
        compiler_params=_params(("parallel", "parallel"), win),
        name="matmul",
    )(a, b)


def _matmul_res(a, b, res):
    m, k = a.shape
    n = b.shape[1]
    bm = _divisor_tile(m, 1024, BF16_SUBLANES)
    bn = _divisor_tile(n, 1024, V7X_LANES)
    bk = _divisor_tile(k, 4096, V7X_LANES)
    win = _nbytes((bm, bk), BF16) + _nbytes((bk, bn), BF16) + 2 * _nbytes((bm, bn), F32)
    return pl.pallas_call(
        _mm_res_kernel,
        grid=(m // bm, n // bn, k // bk),
        in_specs=[pl.BlockSpec((bm, bk), lambda i, j, kk: (i, kk)),
                  pl.BlockSpec((bk, bn), lambda i, j, kk: (kk, j)),
                  pl.BlockSpec((bm, bn), lambda i, j, kk: (i, j))],
        out_specs=pl.BlockSpec((bm, bn), lambda i, j, kk: (i, j)),
        out_shape=jax.ShapeDtypeStruct((m, n), F32),
        compiler_params=_params(("parallel", "parallel", "arbitrary"), win),
        name="matmul_res",
    )(a, b, res)


def _gateup(a, wg, wu):
    m, k = a.shape
    n = wg.shape[1]
    bm = _divisor_tile(m, 1024, BF16_SUBLANES)
    bn = _divisor_tile(n, 512, V7X_LANES)
    win = _nbytes((bm, k), BF16) + 2 * _nbytes((k, bn), BF16) + _nbytes((bm, bn), BF16)
    return pl.pallas_call(
        _gateup_kernel,
        grid=(m // bm, n // bn),
        in_specs=[pl.BlockSpec((bm, k), lambda i, j: (i, 0)),
                  pl.BlockSpec((k, bn), lambda i, j: (0, j)),
                  pl.BlockSpec((k, bn), lambda i, j: (0, j))],
        out_specs=pl.BlockSpec((bm, bn), lambda i, j: (i, j)),
        out_shape=jax.ShapeDtypeStruct((m, n), BF16),
        compiler_params=_params(("parallel", "parallel"), win),
        name="gateup",
    )(a, wg, wu)


def _t5_bucket(rel, n_buckets):
    half = n_buckets // 2
    exact = half // 2
    n = jnp.abs(rel)
    n_f = jnp.maximum(n, 1).astype(F32)
    large = exact + (jnp.log(n_f / exact) / math.log(MAX_DISTANCE / exact) * (half - exact)).astype(jnp.int32)
    large = jnp.minimum(large, half - 1)
    return jnp.where(rel > 0, half, 0) + jnp.where(n < exact, n, large)


def _bucket_tables(n_meta, n_buckets):
    qi = jnp.arange(BLOCK)[:, None]
    sj = jnp.arange(3 * BLOCK)[None, :]
    rel_band = sj - BLOCK - qi
    band = jnp.where(jnp.abs(rel_band) <= BLOCK, _t5_bucket(rel_band, n_buckets), -1)
    masked = jnp.full((BLOCK, BLOCK), -1, jnp.int32)
    pad = jnp.full((BLOCK, BLOCK - n_meta), -1, jnp.int32)
    mk = jnp.arange(n_meta)[None, :]
    meta_first = _t5_bucket(mk - (n_meta + qi), n_buckets)
    meta_far = jnp.full((BLOCK, n_meta), n_buckets // 2 - 1, jnp.int32)
    interior = jnp.concatenate([band, meta_far, pad], axis=1)
    first = jnp.concatenate([masked, band[:, BLOCK:], meta_first, pad], axis=1)
    last = jnp.concatenate([band[:, :2 * BLOCK], masked, meta_far, pad], axis=1)
    tok = jnp.stack([interior, first, last]).astype(jnp.int32)

    mq = jnp.arange(n_meta)[:, None]
    tk = jnp.arange(BLOCK)[None, :]
    rel_tok = n_meta + tk - mq
    mband = jnp.where(jnp.abs(rel_tok) <= BLOCK, _t5_bucket(rel_tok, n_buckets), -1)
    mmeta = _t5_bucket(mk - mq, n_buckets)
    mpad = jnp.full((n_meta, BLOCK - n_meta), -1, jnp.int32)
    met = jnp.concatenate([mband, mmeta, mpad], axis=1).astype(jnp.int32)
    return tok, met


def _bias_kernel(tab_ref, bkt_ref, o_ref, *, n_buckets):
    h = pl.program_id(0)
    bkt = bkt_ref[...]
    acc = jnp.full(bkt.shape, -jnp.inf, F32)
    for b in range(n_buckets):
        acc = jnp.where(bkt == b, tab_ref[b, h], acc)
    o_ref[0] = acc


def _bias_tiles(rel_bias, bkt):
    n_buckets, n_heads = rel_bias.shape
    nd = bkt.ndim
    zeros = (0,) * nd
    return pl.pallas_call(
        functools.partial(_bias_kernel, n_buckets=n_buckets),
        grid=(n_heads,),
        in_specs=[pl.BlockSpec(memory_space=pltpu.SMEM),
                  pl.BlockSpec(bkt.shape, lambda h: zeros)],
        out_specs=pl.BlockSpec((1,) + bkt.shape, lambda h: (h,) + zeros),
        out_shape=jax.ShapeDtypeStruct((n_heads,) + bkt.shape, F32),
        compiler_params=_params(("parallel",), 2 * _nbytes(bkt.shape, F32)),
        name="bias_tiles",
    )(rel_bias, bkt)


def _softmax_pv(s, sink, v_parts):
    m = jnp.maximum(jnp.max(s, axis=-1, keepdims=True), sink)
    p = jnp.exp(s - m)
    denom = jnp.sum(p, axis=-1, keepdims=True) + jnp.exp(sink - m)
    pn = (p * (1.0 / denom)).astype(BF16)
    o = None
    c0 = 0
    for v in v_parts:
        part = jnp.dot(pn[:, c0:c0 + v.shape[0]], v, preferred_element_type=F32)
        o = part if o is None else o + part
        c0 += v.shape[0]
    return o


def _qk(q, k):
    return lax.dot_general(q, k, (((1,), (1,)), ((), ())), preferred_element_type=F32)


def _attn_kernel(sink_ref, qg_ref, kg_ref, bias_ref, q_ref, kp_ref, kc_ref, kn_ref, vp_ref, vc_ref, vn_ref,
                 km_ref, vm_ref, o_ref, k_scr, v_scr, km_scr, vm_scr, *, chunks, seq_chunks, n_meta, scale):
    h = pl.program_id(0)
    r = pl.program_id(1)
    qg = qg_ref[...]
    kg = kg_ref[...]
    body = chunks * BLOCK

    k_scr[0:BLOCK] = _rms(kp_ref[...], kg).astype(BF16)
    k_scr[BLOCK:BLOCK + body] = _rms(kc_ref[...], kg).astype(BF16)
    k_scr[BLOCK + body:2 * BLOCK + body] = _rms(kn_ref[...], kg).astype(BF16)
    v_scr[0:BLOCK] = vp_ref[...].astype(BF16)
    v_scr[BLOCK:BLOCK + body] = vc_ref[...].astype(BF16)
    v_scr[BLOCK + body:2 * BLOCK + body] = vn_ref[...].astype(BF16)
    km_scr[...] = jnp.zeros(km_scr.shape, BF16)
    vm_scr[...] = jnp.zeros(vm_scr.shape, BF16)
    km_scr[0:n_meta] = _rms(km_ref[...], kg).astype(BF16)
    vm_scr[0:n_meta] = vm_ref[...].astype(BF16)
    k_meta = km_scr[...]
    v_meta = vm_scr[...]

    def chunk(cc, carry):
        gch = r * chunks + cc
        is_first = functools.reduce(jnp.logical_or, [gch == s0 for s0, _ in seq_chunks])
        is_last = functools.reduce(jnp.logical_or, [gch == s0 + n - 1 for s0, n in seq_chunks])
        var = jnp.where(is_first, 1, jnp.where(is_last, 2, 0))
        row0 = pl.multiple_of(cc * BLOCK, BLOCK)
        k_win = k_scr[pl.ds(row0, 3 * BLOCK), :]
        v_win = v_scr[pl.ds(row0, 3 * BLOCK), :]
        qc = q_ref[pl.ds(row0, BLOCK), :]
        for g in range(GROUP):
            qn = _rms(qc[:, g * HEAD_DIM:(g + 1) * HEAD_DIM], qg).astype(BF16)
            s = jnp.concatenate([_qk(qn, k_win), _qk(qn, k_meta)], axis=1) * scale + bias_ref[g, var]
            o = _softmax_pv(s, sink_ref[h * GROUP + g], (v_win, v_meta))
            o_ref[pl.ds(row0, BLOCK), g * HEAD_DIM:(g + 1) * HEAD_DIM] = o
        return carry

    lax.fori_loop(0, chunks, chunk, 0)


def _seq_of_chunk(c, seq_chunks):
    s = 0
    for s0, _ in seq_chunks[1:]:
        s = s + (c >= s0).astype(jnp.int32)
    return s


def _attention(z, zm, bias_tok, sink, q_g, k_g, *, seq_chunks, n_meta, d_model, k_col, v_col):
    m = z.shape[0]
    n_chunks = m // BLOCK
    kv_heads = d_model // (GROUP * HEAD_DIM)
    chunks = math.gcd(8, *[n for _, n in seq_chunks])
    body = chunks * BLOCK
    qw = GROUP * HEAD_DIM
    kb, vb = k_col // HEAD_DIM, v_col // HEAD_DIM
    seq_of = functools.partial(_seq_of_chunk, seq_chunks=seq_chunks)

    def halo_prev(h, r):
        return jnp.maximum(r * chunks - 1, 0)

    def halo_next(h, r):
        return jnp.minimum(r * chunks + chunks, n_chunks - 1)

    in_specs = [
        pl.BlockSpec(memory_space=pltpu.SMEM),
        pl.BlockSpec((1, HEAD_DIM), lambda h, r: (0, 0)),
        pl.BlockSpec((1, HEAD_DIM), lambda h, r: (0, 0)),
        pl.BlockSpec((GROUP, 3, BLOCK, 4 * BLOCK), lambda h, r: (h, 0, 0, 0)),
        pl.BlockSpec((body, qw), lambda h, r: (r, h)),
        pl.BlockSpec((BLOCK, HEAD_DIM), lambda h, r: (halo_prev(h, r), kb + h)),
        pl.BlockSpec((body, HEAD_DIM), lambda h, r: (r, kb + h)),
        pl.BlockSpec((BLOCK, HEAD_DIM), lambda h, r: (halo_next(h, r), kb + h)),
        pl.BlockSpec((BLOCK, HEAD_DIM), lambda h, r: (halo_prev(h, r), vb + h)),
        pl.BlockSpec((body, HEAD_DIM), lambda h, r: (r, vb + h)),
        pl.BlockSpec((BLOCK, HEAD_DIM), lambda h, r: (halo_next(h, r), vb + h)),
        pl.BlockSpec((n_meta, HEAD_DIM), lambda h, r: (seq_of(r * chunks), kb + h)),
        pl.BlockSpec((n_meta, HEAD_DIM), lambda h, r: (seq_of(r * chunks), vb + h)),
    ]
    win = (_nbytes((GROUP, 3, BLOCK, 4 * BLOCK), F32) + 2 * _nbytes((body, qw), F32)
           + 4 * _nbytes((body + 2 * BLOCK, HEAD_DIM), F32))
    return pl.pallas_call(
        functools.partial(_attn_kernel, chunks=chunks, seq_chunks=seq_chunks, n_meta=n_meta,
                          scale=HEAD_DIM ** -0.5),
        grid=(kv_heads, m // body),
        in_specs=in_specs,
        out_specs=pl.BlockSpec((body, qw), lambda h, r: (r, h)),
        out_shape=jax.ShapeDtypeStruct((m, d_model), F32),
        scratch_shapes=[pltpu.VMEM((body + 2 * BLOCK, HEAD_DIM), BF16),
                        pltpu.VMEM((body + 2 * BLOCK, HEAD_DIM), BF16),
                        pltpu.VMEM((BLOCK, HEAD_DIM), BF16),
                        pltpu.VMEM((BLOCK, HEAD_DIM), BF16)],
        compiler_params=_params(("parallel", "arbitrary"), win),
        name="attention",
    )(sink, q_g.reshape(1, HEAD_DIM), k_g.reshape(1, HEAD_DIM), bias_tok, z, z, z, z, z, z, z, zm, zm)


def _attn_meta_kernel(sink_ref, qg_ref, kg_ref, bias_ref, q_ref, k1_ref, v1_ref, km_ref, vm_ref, o_ref,
                      km_scr, vm_scr, *, n_meta, scale):
    h = pl.program_id(1)
    qg = qg_ref[...]
    kg = kg_ref[...]
    k_tok = _rms(k1_ref[...], kg).astype(BF16)
    v_tok = v1_ref[...].astype(BF16)
    km_scr[...] = jnp.zeros(km_scr.shape, BF16)
    vm_scr[...] = jnp.zeros(vm_scr.shape, BF16)
    km_scr[0:n_meta] = _rms(km_ref[...], kg).astype(BF16)
    vm_scr[0:n_meta] = vm_ref[...].astype(BF16)
    k_meta = km_scr[...]
    v_meta = vm_scr[...]
    q = q_ref[...]
    for g in range(GROUP):
        qn = _rms(q[:, g * HEAD_DIM:(g + 1) * HEAD_DIM], qg).astype(BF16)
        s = jnp.concatenate([_qk(qn, k_tok), _qk(qn, k_meta)], axis=1) * scale + bias_ref[g]
        o_ref[:, g * HEAD_DIM:(g + 1) * HEAD_DIM] = _softmax_pv(s, sink_ref[h * GROUP + g], (v_tok, v_meta))


def _attention_meta(z, zm, bias_met, sink, q_g, k_g, *, seq_chunks, n_meta, d_model, k_col, v_col):
    n_seq = len(seq_chunks)
    kv_heads = d_model // (GROUP * HEAD_DIM)
    qw = GROUP * HEAD_DIM
    kb, vb = k_col // HEAD_DIM, v_col // HEAD_DIM

    def first_chunk(s):
        c = 0
        for i, (s0, _) in enumerate(seq_chunks):
            c = c + jnp.where(s == i, s0, 0)
        return c

    in_specs = [
        pl.BlockSpec(memory_space=pltpu.SMEM),
        pl.BlockSpec((1, HEAD_DIM), lambda s, h: (0, 0)),
        pl.BlockSpec((1, HEAD_DIM), lambda s, h: (0, 0)),
        pl.BlockSpec((GROUP, n_meta, 2 * BLOCK), lambda s, h: (h, 0, 0)),
        pl.BlockSpec((n_meta, qw), lambda s, h: (s, h)),
        pl.BlockSpec((BLOCK, HEAD_DIM), lambda s, h: (first_chunk(s), kb + h)),
        pl.BlockSpec((BLOCK, HEAD_DIM), lambda s, h: (first_chunk(s), vb + h)),
        pl.BlockSpec((n_meta, HEAD_DIM), lambda s, h: (s, kb + h)),
        pl.BlockSpec((n_meta, HEAD_DIM), lambda s, h: (s, vb + h)),
    ]
    return pl.pallas_call(
        functools.partial(_attn_meta_kernel, n_meta=n_meta, scale=HEAD_DIM ** -0.5),
        grid=(n_seq, kv_heads),
        in_specs=in_specs,
        out_specs=pl.BlockSpec((n_meta, qw), lambda s, h: (s, h)),
        out_shape=jax.ShapeDtypeStruct((n_seq * n_meta, d_model), F32),
        scratch_shapes=[pltpu.VMEM((BLOCK, HEAD_DIM), BF16), pltpu.VMEM((BLOCK, HEAD_DIM), BF16)],
        compiler_params=_params(("parallel", "parallel"), 1 << 20),
        name="attention_meta",
    )(sink, q_g.reshape(1, HEAD_DIM), k_g.reshape(1, HEAD_DIM), bias_met, zm, z, z, zm, zm)


def _merge_math(attn, gb, gc, hc, ga, gcv, u_prev, u_next, cw, na, nc):
    u = gc * hc
    rows = u.shape[0]
    ridx = lax.broadcasted_iota(jnp.int32, u.shape, 0)
    up = jnp.where(ridx == 0, u_prev, pltpu.roll(u, 1, axis=0))
    un = jnp.where(ridx == rows - 1, u_next, pltpu.roll(u, rows - 1, axis=0))
    conv = gb * (cw[0:1] * up + cw[1:2] * u + cw[2:3] * un)
    return jax.nn.sigmoid(ga) * _rms(attn, na) + jax.nn.sigmoid(gcv) * _rms(conv, nc)


def _merge_kernel(attn_ref, gb_ref, gc_ref, hc_ref, ga_ref, gcv_ref, gcp_ref, hcp_ref, gcn_ref, hcn_ref,
                  gcm_ref, hcm_ref, cw_ref, na_ref, nc_ref, o_ref, *, seq_chunks):
    c = pl.program_id(0)
    is_first = functools.reduce(jnp.logical_or, [c == s0 for s0, _ in seq_chunks])
    is_last = functools.reduce(jnp.logical_or, [c == s0 + n - 1 for s0, n in seq_chunks])
    last = F32_SUBLANES - 1
    u_prev_tok = (gcp_ref[...] * hcp_ref[...])[last:last + 1]
    u_prev_meta = (gcm_ref[...] * hcm_ref[...])[last:last + 1]
    u_prev = jnp.where(is_first, u_prev_meta, u_prev_tok)
    u_next_tok = (gcn_ref[...] * hcn_ref[...])[0:1]
    u_next = jnp.where(is_last, jnp.zeros_like(u_next_tok), u_next_tok)
    o_ref[...] = _merge_math(attn_ref[...], gb_ref[...], gc_ref[...], hc_ref[...], ga_ref[...], gcv_ref[...],
                             u_prev, u_next, cw_ref[...], na_ref[...], nc_ref[...]).astype(o_ref.dtype)


def _merge(attn, z, zm, conv_w, norm_a, norm_c, *, seq_chunks, n_meta, cols):
    m, d = attn.shape
    n_chunks = m // BLOCK
    per_chunk = BLOCK // F32_SUBLANES
    per_meta = n_meta // F32_SUBLANES
    gb, gc, hc, ga, gcv = [c // d for c in cols]
    seq_of = functools.partial(_seq_of_chunk, seq_chunks=seq_chunks)

    def main(col):
        return pl.BlockSpec((BLOCK, d), lambda c: (c, col))

    def prev8(col):
        return pl.BlockSpec((F32_SUBLANES, d), lambda c: (jnp.maximum(c * per_chunk - 1, 0), col))

    def next8(col):
        return pl.BlockSpec((F32_SUBLANES, d), lambda c: (jnp.minimum(c + 1, n_chunks - 1) * per_chunk, col))

    def meta8(col):
        return pl.BlockSpec((F32_SUBLANES, d), lambda c: (seq_of(c) * per_meta + per_meta - 1, col))

    def row(nrows):
        return pl.BlockSpec((nrows, d), lambda c: (0, 0))

    win = 6 * _nbytes((BLOCK, d), F32) + _nbytes((BLOCK, d), BF16) + 8 * _nbytes((F32_SUBLANES, d), F32)
    return pl.pallas_call(
        functools.partial(_merge_kernel, seq_chunks=seq_chunks),
        grid=(n_chunks,),
        in_specs=[main(0), main(gb), main(gc), main(hc), main(ga), main(gcv),
                  prev8(gc), prev8(hc), next8(gc), next8(hc), meta8(gc), meta8(hc),
                  row(3), row(1), row(1)],
        out_specs=pl.BlockSpec((BLOCK, d), lambda c: (c, 0)),
        out_shape=jax.ShapeDtypeStruct((m, d), BF16),
        compiler_params=_params(("parallel",), win),
        name="merge",
    )(attn, z, z, z, z, z, z, z, z, z, zm, zm, conv_w, norm_a.reshape(1, d), norm_c.reshape(1, d))


def _merge_meta_kernel(attn_ref, gb_ref, gc_ref, hc_ref, ga_ref, gcv_ref, gcn_ref, hcn_ref,
                       cw_ref, na_ref, nc_ref, o_ref):
    u_next = (gcn_ref[...] * hcn_ref[...])[0:1]
    u_prev = jnp.zeros_like(u_next)
    o_ref[...] = _merge_math(attn_ref[...], gb_ref[...], gc_ref[...], hc_ref[...], ga_ref[...], gcv_ref[...],
                             u_prev, u_next, cw_ref[...], na_ref[...], nc_ref[...]).astype(o_ref.dtype)


def _merge_meta(attn_m, z, zm, conv_w, norm_a, norm_c, *, seq_chunks, n_meta, cols):
    mm, d = attn_m.shape
    per_chunk = BLOCK // F32_SUBLANES
    gb, gc, hc, ga, gcv = [c // d for c in cols]

    def first_row8(s):
        r = 0
        for i, (s0, _) in enumerate(seq_chunks):
            r = r + jnp.where(s == i, s0 * per_chunk, 0)
        return r

    def main(col):
        return pl.BlockSpec((n_meta, d), lambda s: (s, col))

    def next8(col):
        return pl.BlockSpec((F32_SUBLANES, d), lambda s: (first_row8(s), col))

    def row(nrows):
        return pl.BlockSpec((nrows, d), lambda s: (0, 0))

    win = 7 * _nbytes((n_meta, d), F32) + 2 * _nbytes((F32_SUBLANES, d), F32)
    return pl.pallas_call(
        _merge_meta_kernel,
        grid=(len(seq_chunks),),
        in_specs=[main(0), main(gb), main(gc), main(hc), main(ga), main(gcv), next8(gc), next8(hc),
                  row(3), row(1), row(1)],
        out_specs=pl.BlockSpec((n_meta, d), lambda s: (s, 0)),
        out_shape=jax.ShapeDtypeStruct((mm, d), BF16),
        compiler_params=_params(("parallel",), win),
        name="merge_meta",
    )(attn_m, zm, zm, zm, zm, zm, z, z, conv_w, norm_a.reshape(1, d), norm_c.reshape(1, d))


def kernel(x_prompt, x_sample, meta_tokens, rel_bias, norm1_g, w_in, q_norm_g, k_norm_g, attn_sink, conv_w,
           branch_norm_a, branch_norm_c, w_out, norm2_g, w_ffn_gate, w_ffn_up, w_ffn_down):
    d = x_prompt.shape[-1]
    n_meta = meta_tokens.shape[0]
    n_buckets = rel_bias.shape[0]
    depth, _, in_dim = w_in.shape
    ffn = w_ffn_gate.shape[-1]
    kv_dim = (in_dim - 6 * d) // 2
    assert d % (GROUP * HEAD_DIM) == 0 and kv_dim == d // GROUP and attn_sink.shape[1] * HEAD_DIM == d
    assert n_meta % F32_SUBLANES == 0 and n_meta <= BLOCK

    groups = (x_prompt, x_sample)
    seq_chunks = []
    for xg in groups:
        assert xg.shape[1] % BLOCK == 0 and xg.shape[1] >= 2 * BLOCK
        for _ in range(xg.shape[0]):
            start = seq_chunks[-1][0] + seq_chunks[-1][1] if seq_chunks else 0
            seq_chunks.append((start, xg.shape[1] // BLOCK))
    seq_chunks = tuple(seq_chunks)
    n_seq = len(seq_chunks)

    x = jnp.concatenate([xg.reshape(-1, d) for xg in groups], axis=0)
    xm = jnp.tile(meta_tokens.astype(x.dtype), (n_seq, 1))

    q_end, v_end = d, d + 2 * kv_dim
    w_in_b = jnp.concatenate([w_in[:, :, :q_end], w_in[:, :, v_end:], w_in[:, :, q_end:v_end]], axis=-1).astype(BF16)
    cols = tuple(d * i for i in range(1, 6))
    k_col, v_col = 6 * d, 6 * d + kv_dim
    w_out_b = w_out.astype(BF16)
    ffn_p = -(-ffn // 512) * 512
    wg_b = jnp.pad(w_ffn_gate, ((0, 0), (0, 0), (0, ffn_p - ffn))).astype(BF16)
    wu_b = jnp.pad(w_ffn_up, ((0, 0), (0, 0), (0, ffn_p - ffn))).astype(BF16)
    wd_b = jnp.pad(w_ffn_down, ((0, 0), (0, ffn_p - ffn), (0, 0))).astype(BF16)

    bkt_tok, bkt_met = _bucket_tables(n_meta, n_buckets)
    bias_tok = _bias_tiles(rel_bias, bkt_tok)
    bias_met = _bias_tiles(rel_bias, bkt_met)

    geo = dict(seq_chunks=seq_chunks, n_meta=n_meta)
    for l in range(depth):
        last_layer = l == depth - 1
        z = _matmul(_rmsnorm(x, norm1_g[l]), w_in_b[l], F32)
        zm = _matmul(_rmsnorm(xm, norm1_g[l]), w_in_b[l], F32)
        attn = _attention(z, zm, bias_tok, attn_sink[l], q_norm_g[l], k_norm_g[l],
                          d_model=d, k_col=k_col, v_col=v_col, **geo)
        merged = _merge(attn, z, zm, conv_w[l], branch_norm_a[l], branch_norm_c[l], cols=cols, **geo)
        h = _matmul_res(merged, w_out_b[l], x)
        t = _gateup(_rmsnorm(h, norm2_g[l]), wg_b[l], wu_b[l])
        x = _matmul_res(t, wd_b[l], h)
        if not last_layer:
            attn_m = _attention_meta(z, zm, bias_met, attn_sink[l], q_norm_g[l], k_norm_g[l],
                                     d_model=d, k_col=k_col, v_col=v_col, **geo)
            merged_m = _merge_meta(attn_m, z, zm, conv_w[l], branch_norm_a[l], branch_norm_c[l], cols=cols, **geo)
            hm = _matmul_res(merged_m, w_out_b[l], xm)
            tm = _gateup(_rmsnorm(hm, norm2_g[l]), wg_b[l], wu_b[l])
            xm = _matmul_res(tm, wd_b[l], hm)

    outs = []
    row = 0
    for xg in groups:
        rows = xg.shape[0] * xg.shape[1]
        outs.append(x[row:row + rows].reshape(xg.shape))
        row += rows
    return tuple(outs)
```

```python
import functools
import math

import jax
import jax.numpy as jnp
from jax import lax
from jax.experimental import pallas as pl
from jax.experimental.pallas import tpu as pltpu

HEAD_DIM = 128
GROUP = 4
BLOCK = 128
MAX_DISTANCE = 128
EPS = 1e-6

V7X_VMEM_BYTES = 64 << 20
V7X_LANES = 128
BF16_SUBLANES = 16
F32_SUBLANES = 8
VMEM_SLACK_BYTES = 10 << 20

F32 = jnp.float32
BF16 = jnp.bfloat16


def _params(dims, window_bytes):
    limit = min(2 * window_bytes + VMEM_SLACK_BYTES, V7X_VMEM_BYTES - (6 << 20))
    return pltpu.CompilerParams(dimension_semantics=dims, vmem_limit_bytes=int(limit))


def _divisor_tile(n, cap, unit):
    if n <= cap:
        return n
    t = (cap // unit) * unit
    while t >= unit:
        if n % t == 0:
            return t
        t -= unit
    raise ValueError(f"no tile for {n}")


def _nbytes(shape, dtype):
    return math.prod(shape) * jnp.dtype(dtype).itemsize


def _rms(x, g):
    ms = jnp.mean(x * x, axis=-1, keepdims=True)
    return x * lax.rsqrt(ms + EPS) * g


def _rmsnorm_kernel(x_ref, g_ref, o_ref):
    o_ref[...] = _rms(x_ref[...], g_ref[...]).astype(o_ref.dtype)


def _rmsnorm(x, g):
    m, d = x.shape
    bm = _divisor_tile(m, 256, BF16_SUBLANES)
    return pl.pallas_call(
        _rmsnorm_kernel,
        grid=(m // bm,),
        in_specs=[pl.BlockSpec((bm, d), lambda i: (i, 0)),
                  pl.BlockSpec((1, d), lambda i: (0, 0))],
        out_specs=pl.BlockSpec((bm, d), lambda i: (i, 0)),
        out_shape=jax.ShapeDtypeStruct((m, d), BF16),
        compiler_params=_params(("parallel",), _nbytes((bm, d), F32) + _nbytes((bm, d), BF16)),
        name="rmsnorm",
    )(x, g.reshape(1, d))


def _mm_kernel(a_ref, b_ref, o_ref):
    o_ref[...] = jnp.dot(a_ref[...], b_ref[...], preferred_element_type=F32).astype(o_ref.dtype)


def _mm_res_kernel(a_ref, b_ref, r_ref, o_ref):
    @pl.when(pl.program_id(2) == 0)
    def _():
        o_ref[...] = r_ref[...]

    o_ref[...] += jnp.dot(a_ref[...], b_ref[...], preferred_element_type=F32)


def _gateup_kernel(a_ref, wg_ref, wu_ref, o_ref):
    a = a_ref[...]
    g = jnp.dot(a, wg_ref[...], preferred_element_type=F32)
    u = jnp.dot(a, wu_ref[...], preferred_element_type=F32)
    o_ref[...] = (g * jax.nn.sigmoid(g) * u).astype(o_ref.dtype)


def _matmul(a, b, out_dtype):
    m, k = a.shape
    n = b.shape[1]
    bm = _divisor_tile(m, 1024, BF16_SUBLANES)
    bn = _divisor_tile(n, 1024, V7X_LANES)
    win = _nbytes((bm, k), BF16) + _nbytes((k, bn), BF16) + _nbytes((bm, bn), out_dtype)
    return pl.pallas_call(
        _mm_kernel,
        grid=(m // bm, n // bn),
        in_specs=[pl.BlockSpec((bm, k), lambda i, j: (i, 0)),
                  pl.BlockSpec((k, bn), lambda i, j: (0, j))],
        out_specs=pl.BlockSpec((bm, bn), lambda i, j: (i, j)),
        out_shape=jax.ShapeDtypeStruct((m, n), out_dtype),
        compiler_params=_params(("parallel", "parallel"), win),
        name="matmul",
    )(a, b)


def _matmul_res(a, b, res):
    m, k = a.shape
    n = b.shape[1]
    bm = _divisor_tile(m, 1024, BF16_SUBLANES)
    bn = _divisor_tile(n, 1024, V7X_LANES)
    bk = _divisor_tile(k, 4096, V7X_LANES)
    win = _nbytes((bm, bk), BF16) + _nbytes((bk, bn), BF16) + 2 * _nbytes((bm, bn), F32)
    return pl.pallas_call(
        _mm_res_kernel,
        grid=(m // bm, n // bn, k // bk),
        in_specs=[pl.BlockSpec((bm, bk), lambda i, j, kk: (i, kk)),
                  pl.BlockSpec((bk, bn), lambda i, j, kk: (kk, j)),
                  pl.BlockSpec((bm, bn), lambda i, j, kk: (i, j))],
        out_specs=pl.BlockSpec((bm, bn), lambda i, j, kk: (i, j)),
        out_shape=jax.ShapeDtypeStruct((m, n), F32),
        compiler_params=_params(("parallel", "parallel", "arbitrary"), win),
        name="matmul_res",
    )(a, b, res)


def _gateup(a, wg, wu):
    m, k = a.shape
    n = wg.shape[1]
    bm = _divisor_tile(m, 1024, BF16_SUBLANES)
    bn = _divisor_tile(n, 512, V7X_LANES)
    win = _nbytes((bm, k), BF16) + 2 * _nbytes((k, bn), BF16) + _nbytes((bm, bn), BF16)
    return pl.pallas_call(
        _gateup_kernel,
        grid=(m // bm, n // bn),
        in_specs=[pl.BlockSpec((bm, k), lambda i, j: (i, 0)),
                  pl.BlockSpec((k, bn), lambda i, j: (0, j)),
                  pl.BlockSpec((k, bn), lambda i, j: (0, j))],
        out_specs=pl.BlockSpec((bm, bn), lambda i, j: (i, j)),
        out_shape=jax.ShapeDtypeStruct((m, n), BF16),
        compiler_params=_params(("parallel", "parallel"), win),
        name="gateup",
    )(a, wg, wu)


def _t5_bucket(rel, n_buckets):
    half = n_buckets // 2
    exact = half // 2
    n = jnp.abs(rel)
    n_f = jnp.maximum(n, 1).astype(F32)
    large = exact + (jnp.log(n_f / exact) / math.log(MAX_DISTANCE / exact) * (half - exact)).astype(jnp.int32)
    large = jnp.minimum(large, half - 1)
    return jnp.where(rel > 0, half, 0) + jnp.where(n < exact, n, large)


def _bucket_tables(n_meta, n_buckets):
    qi = jnp.arange(BLOCK)[:, None]
    sj = jnp.arange(3 * BLOCK)[None, :]
    rel_band = sj - BLOCK - qi
    band = jnp.where(jnp.abs(rel_band) <= BLOCK, _t5_bucket(rel_band, n_buckets), -1)
    masked = jnp.full((BLOCK, BLOCK), -1, jnp.int32)
    sink = jnp.full((BLOCK, 1), n_buckets, jnp.int32)
    pad = jnp.full((BLOCK, BLOCK - n_meta - 1), -1, jnp.int32)
    mk = jnp.arange(n_meta)[None, :]
    meta_first = _t5_bucket(mk - (n_meta + qi), n_buckets)
    meta_far = jnp.full((BLOCK, n_meta), n_buckets // 2 - 1, jnp.int32)
    interior = jnp.concatenate([band, meta_far, sink, pad], axis=1)
    first = jnp.concatenate([masked, band[:, BLOCK:], meta_first, sink, pad], axis=1)
    last = jnp.concatenate([band[:, :2 * BLOCK], masked, meta_far, sink, pad], axis=1)
    tok = jnp.stack([interior, first, last]).astype(jnp.int32)

    mq = jnp.arange(n_meta)[:, None]
    tk = jnp.arange(BLOCK)[None, :]
    rel_tok = n_meta + tk - mq
    mband = jnp.where(jnp.abs(rel_tok) <= BLOCK, _t5_bucket(rel_tok, n_buckets), -1)
    mmeta = _t5_bucket(mk - mq, n_buckets)
    met = jnp.concatenate([mband, mmeta, sink[:n_meta], pad[:n_meta]], axis=1).astype(jnp.int32)
    return tok, met[None]


def _bias_kernel(tab_ref, bkt_ref, o_ref, *, n_ids):
    h = pl.program_id(0)
    bkt = bkt_ref[...]
    acc = jnp.full(bkt.shape, -jnp.inf, F32)
    for b in range(n_ids):
        acc = jnp.where(bkt == b, tab_ref[b, h], acc)
    o_ref[0, :, 0] = acc


def _bias_tiles(table, bkt):
    n_ids, n_heads = table.shape
    v, r, c = bkt.shape
    return pl.pallas_call(
        functools.partial(_bias_kernel, n_ids=n_ids),
        grid=(n_heads,),
        in_specs=[pl.BlockSpec(memory_space=pltpu.SMEM),
                  pl.BlockSpec((v, r, c), lambda h: (0, 0, 0))],
        out_specs=pl.BlockSpec((1, v, 1, r, c), lambda h: (h // GROUP, 0, h % GROUP, 0, 0)),
        out_shape=jax.ShapeDtypeStruct((n_heads // GROUP, v, GROUP, r, c), F32),
        compiler_params=_params(("parallel",), 2 * _nbytes(bkt.shape, F32)),
        name="bias_tiles",
    )(table, bkt)


def _softmax_pv(q, k, v, bias, scale):
    s = lax.dot_general(q, k, (((1,), (1,)), ((), ())), preferred_element_type=F32) * scale + bias
    m = jnp.max(s, axis=-1, keepdims=True)
    p = jnp.exp(s - m)
    denom = jnp.sum(p, axis=-1, keepdims=True)
    pn = (p * (1.0 / denom)).astype(BF16)
    return jnp.dot(pn, v, preferred_element_type=F32)


def _fill_meta_keys(km_scr, vm_scr, km_ref, vm_ref, kg, n_meta):
    km_scr[...] = jnp.zeros(km_scr.shape, BF16)
    vm_scr[...] = jnp.zeros(vm_scr.shape, BF16)
    km_scr[0:n_meta] = _rms(km_ref[...], kg).astype(BF16)
    vm_scr[0:n_meta] = vm_ref[...].astype(BF16)


def _attn_kernel(qg_ref, kg_ref, bias_ref, q_ref, kp_ref, kc_ref, kn_ref, vp_ref, vc_ref, vn_ref,
                 km_ref, vm_ref, o_ref, q_scr, k_scr, v_scr, km_scr, vm_scr, *, chunks, seq_chunks, n_meta, scale):
    r = pl.program_id(1)
    qg = qg_ref[...]
    kg = kg_ref[...]
    body = chunks * BLOCK
    stack = GROUP * BLOCK

    k_scr[0:BLOCK] = _rms(kp_ref[...], kg).astype(BF16)
    k_scr[BLOCK:BLOCK + body] = _rms(kc_ref[...], kg).astype(BF16)
    k_scr[BLOCK + body:2 * BLOCK + body] = _rms(kn_ref[...], kg).astype(BF16)
    v_scr[0:BLOCK] = vp_ref[...].astype(BF16)
    v_scr[BLOCK:BLOCK + body] = vc_ref[...].astype(BF16)
    v_scr[BLOCK + body:2 * BLOCK + body] = vn_ref[...].astype(BF16)
    _fill_meta_keys(km_scr, vm_scr, km_ref, vm_ref, kg, n_meta)
    for g in range(GROUP):
        qn = _rms(q_ref[:, g * HEAD_DIM:(g + 1) * HEAD_DIM], qg).astype(BF16)
        for c in range(chunks):
            q_scr[(c * GROUP + g) * BLOCK:(c * GROUP + g + 1) * BLOCK] = qn[c * BLOCK:(c + 1) * BLOCK]
    k_meta = km_scr[...]
    v_meta = vm_scr[...]

    def chunk(cc, carry):
        gch = r * chunks + cc
        is_first = functools.reduce(jnp.logical_or, [gch == s0 for s0, _ in seq_chunks])
        is_last = functools.reduce(jnp.logical_or, [gch == s0 + n - 1 for s0, n in seq_chunks])
        var = jnp.where(is_first, 1, jnp.where(is_last, 2, 0))
        row0 = pl.multiple_of(cc * BLOCK, BLOCK)
        k_all = jnp.concatenate([k_scr[pl.ds(row0, 3 * BLOCK), :], k_meta], axis=0)
        v_all = jnp.concatenate([v_scr[pl.ds(row0, 3 * BLOCK), :], v_meta], axis=0)
        q = q_scr[pl.ds(pl.multiple_of(cc * stack, stack), stack), :]
        bias = bias_ref[0, var].reshape(stack, 4 * BLOCK)
        o = _softmax_pv(q, k_all, v_all, bias, scale)
        for g in range(GROUP):
            o_ref[pl.ds(row0, BLOCK), g * HEAD_DIM:(g + 1) * HEAD_DIM] = o[g * BLOCK:(g + 1) * BLOCK]
        return carry

    lax.fori_loop(0, chunks, chunk, 0, unroll=True)


def _seq_of_chunk(c, seq_chunks):
    s = 0
    for s0, _ in seq_chunks[1:]:
        s = s + (c >= s0).astype(jnp.int32)
    return s


def _attention(z, zm, bias_tok, q_g, k_g, *, seq_chunks, n_meta, d_model, k_col, v_col):
    m = z.shape[0]
    n_chunks = m // BLOCK
    kv_heads = d_model // (GROUP * HEAD_DIM)
    chunks = math.gcd(8, *[n for _, n in seq_chunks])
    body = chunks * BLOCK
    qw = GROUP * HEAD_DIM
    kb, vb = k_col // HEAD_DIM, v_col // HEAD_DIM
    seq_of = functools.partial(_seq_of_chunk, seq_chunks=seq_chunks)

    def halo_prev(h, r):
        return jnp.maximum(r * chunks - 1, 0)

    def halo_next(h, r):
        return jnp.minimum(r * chunks + chunks, n_chunks - 1)

    in_specs = [
        pl.BlockSpec((1, HEAD_DIM), lambda h, r: (0, 0)),
        pl.BlockSpec((1, HEAD_DIM), lambda h, r: (0, 0)),
        pl.BlockSpec((1, 3, GROUP, BLOCK, 4 * BLOCK), lambda h, r: (h, 0, 0, 0, 0)),
        pl.BlockSpec((body, qw), lambda h, r: (r, h)),
        pl.BlockSpec((BLOCK, HEAD_DIM), lambda h, r: (halo_prev(h, r), kb + h)),
        pl.BlockSpec((body, HEAD_DIM), lambda h, r: (r, kb + h)),
        pl.BlockSpec((BLOCK, HEAD_DIM), lambda h, r: (halo_next(h, r), kb + h)),
        pl.BlockSpec((BLOCK, HEAD_DIM), lambda h, r: (halo_prev(h, r), vb + h)),
        pl.BlockSpec((body, HEAD_DIM), lambda h, r: (r, vb + h)),
        pl.BlockSpec((BLOCK, HEAD_DIM), lambda h, r: (halo_next(h, r), vb + h)),
        pl.BlockSpec((n_meta, HEAD_DIM), lambda h, r: (seq_of(r * chunks), kb + h)),
        pl.BlockSpec((n_meta, HEAD_DIM), lambda h, r: (seq_of(r * chunks), vb + h)),
    ]
    win = (_nbytes((3, GROUP, BLOCK, 4 * BLOCK), F32) + 2 * _nbytes((body, qw), F32)
           + 4 * _nbytes((body + 2 * BLOCK, HEAD_DIM), F32))
    return pl.pallas_call(
        functools.partial(_attn_kernel, chunks=chunks, seq_chunks=seq_chunks, n_meta=n_meta,
                          scale=HEAD_DIM ** -0.5),
        grid=(kv_heads, m // body),
        in_specs=in_specs,
        out_specs=pl.BlockSpec((body, qw), lambda h, r: (r, h)),
        out_shape=jax.ShapeDtypeStruct((m, d_model), F32),
        scratch_shapes=[pltpu.VMEM((body * GROUP, HEAD_DIM), BF16),
                        pltpu.VMEM((body + 2 * BLOCK, HEAD_DIM), BF16),
                        pltpu.VMEM((body + 2 * BLOCK, HEAD_DIM), BF16),
                        pltpu.VMEM((BLOCK, HEAD_DIM), BF16),
                        pltpu.VMEM((BLOCK, HEAD_DIM), BF16)],
        compiler_params=_params(("parallel", "arbitrary"), win),
        name="attention",
    )(q_g.reshape(1, HEAD_DIM), k_g.reshape(1, HEAD_DIM), bias_tok, z, z, z, z, z, z, z, zm, zm)


def _attn_meta_kernel(qg_ref, kg_ref, bias_ref, q_ref, k1_ref, v1_ref, km_ref, vm_ref, o_ref,
                      km_scr, vm_scr, *, n_meta, scale):
    qg = qg_ref[...]
    kg = kg_ref[...]
    _fill_meta_keys(km_scr, vm_scr, km_ref, vm_ref, kg, n_meta)
    k_all = jnp.concatenate([_rms(k1_ref[...], kg).astype(BF16), km_scr[...]], axis=0)
    v_all = jnp.concatenate([v1_ref[...].astype(BF16), vm_scr[...]], axis=0)
    q = q_ref[...]
    qn = jnp.concatenate([_rms(q[:, g * HEAD_DIM:(g + 1) * HEAD_DIM], qg).astype(BF16) for g in range(GROUP)], axis=0)
    o = _softmax_pv(qn, k_all, v_all, bias_ref[0, 0].reshape(GROUP * n_meta, 2 * BLOCK), scale)
    for g in range(GROUP):
        o_ref[:, g * HEAD_DIM:(g + 1) * HEAD_DIM] = o[g * n_meta:(g + 1) * n_meta]


def _attention_meta(z, zm, bias_met, q_g, k_g, *, seq_chunks, n_meta, d_model, k_col, v_col):
    n_seq = len(seq_chunks)
    kv_heads = d_model // (GROUP * HEAD_DIM)
    qw = GROUP * HEAD_DIM
    kb, vb = k_col // HEAD_DIM, v_col // HEAD_DIM

    def first_chunk(s):
        c = 0
        for i, (s0, _) in enumerate(seq_chunks):
            c = c + jnp.where(s == i, s0, 0)
        return c

    in_specs = [
        pl.BlockSpec((1, HEAD_DIM), lambda s, h: (0, 0)),
        pl.BlockSpec((1, HEAD_DIM), lambda s, h: (0, 0)),
        pl.BlockSpec((1, 1, GROUP, n_meta, 2 * BLOCK), lambda s, h: (h, 0, 0, 0, 0)),
        pl.BlockSpec((n_meta, qw), lambda s, h: (s, h)),
        pl.BlockSpec((BLOCK, HEAD_DIM), lambda s, h: (first_chunk(s), kb + h)),
        pl.BlockSpec((BLOCK, HEAD_DIM), lambda s, h: (first_chunk(s), vb + h)),
        pl.BlockSpec((n_meta, HEAD_DIM), lambda s, h: (s, kb + h)),
        pl.BlockSpec((n_meta, HEAD_DIM), lambda s, h: (s, vb + h)),
    ]
    return pl.pallas_call(
        functools.partial(_attn_meta_kernel, n_meta=n_meta, scale=HEAD_DIM ** -0.5),
        grid=(n_seq, kv_heads),
        in_specs=in_specs,
        out_specs=pl.BlockSpec((n_meta, qw), lambda s, h: (s, h)),
        out_shape=jax.ShapeDtypeStruct((n_seq * n_meta, d_model), F32),
        scratch_shapes=[pltpu.VMEM((BLOCK, HEAD_DIM), BF16), pltpu.VMEM((BLOCK, HEAD_DIM), BF16)],
        compiler_params=_params(("parallel", "parallel"), 1 << 20),
        name="attention_meta",
    )(q_g.reshape(1, HEAD_DIM), k_g.reshape(1, HEAD_DIM), bias_met, zm, z, z, zm, zm)


def _merge_math(attn, gb, gc, hc, ga, gcv, u_prev, u_next, cw, na, nc):
    u = gc * hc
    rows = u.shape[0]
    ridx = lax.broadcasted_iota(jnp.int32, u.shape, 0)
    up = jnp.where(ridx == 0, u_prev, pltpu.roll(u, 1, axis=0))
    un = jnp.where(ridx == rows - 1, u_next, pltpu.roll(u, rows - 1, axis=0))
    conv = gb * (cw[0:1] * up + cw[1:2] * u + cw[2:3] * un)
    return jax.nn.sigmoid(ga) * _rms(attn, na) + jax.nn.sigmoid(gcv) * _rms(conv, nc)


def _merge_kernel(attn_ref, gb_ref, gc_ref, hc_ref, ga_ref, gcv_ref, gcp_ref, hcp_ref, gcn_ref, hcn_ref,
                  gcm_ref, hcm_ref, cw_ref, na_ref, nc_ref, o_ref, *, seq_chunks):
    c = pl.program_id(0)
    is_first = functools.reduce(jnp.logical_or, [c == s0 for s0, _ in seq_chunks])
    is_last = functools.reduce(jnp.logical_or, [c == s0 + n - 1 for s0, n in seq_chunks])
    last = F32_SUBLANES - 1
    u_prev_tok = (gcp_ref[...] * hcp_ref[...])[last:last + 1]
    u_prev_meta = (gcm_ref[...] * hcm_ref[...])[last:last + 1]
    u_prev = jnp.where(is_first, u_prev_meta, u_prev_tok)
    u_next_tok = (gcn_ref[...] * hcn_ref[...])[0:1]
    u_next = jnp.where(is_last, jnp.zeros_like(u_next_tok), u_next_tok)
    o_ref[...] = _merge_math(attn_ref[...], gb_ref[...], gc_ref[...], hc_ref[...], ga_ref[...], gcv_ref[...],
                             u_prev, u_next, cw_ref[...], na_ref[...], nc_ref[...]).astype(o_ref.dtype)


def _merge(attn, z, zm, conv_w, norm_a, norm_c, *, seq_chunks, n_meta, cols):
    m, d = attn.shape
    n_chunks = m // BLOCK
    per_chunk = BLOCK // F32_SUBLANES
    per_meta = n_meta // F32_SUBLANES
    gb, gc, hc, ga, gcv = [c // d for c in cols]
    seq_of = functools.partial(_seq_of_chunk, seq_chunks=seq_chunks)

    def main(col):
        return pl.BlockSpec((BLOCK, d), lambda c: (c, col))

    def prev8(col):
        return pl.BlockSpec((F32_SUBLANES, d), lambda c: (jnp.maximum(c * per_chunk - 1, 0), col))

    def next8(col):
        return pl.BlockSpec((F32_SUBLANES, d), lambda c: (jnp.minimum(c + 1, n_chunks - 1) * per_chunk, col))

    def meta8(col):
        return pl.BlockSpec((F32_SUBLANES, d), lambda c: (seq_of(c) * per_meta + per_meta - 1, col))

    def row(nrows):
        return pl.BlockSpec((nrows, d), lambda c: (0, 0))

    win = 6 * _nbytes((BLOCK, d), F32) + _nbytes((BLOCK, d), BF16) + 8 * _nbytes((F32_SUBLANES, d), F32)
    return pl.pallas_call(
        functools.partial(_merge_kernel, seq_chunks=seq_chunks),
        grid=(n_chunks,),
        in_specs=[main(0), main(gb), main(gc), main(hc), main(ga), main(gcv),
                  prev8(gc), prev8(hc), next8(gc), next8(hc), meta8(gc), meta8(hc),
                  row(3), row(1), row(1)],
        out_specs=pl.BlockSpec((BLOCK, d), lambda c: (c, 0)),
        out_shape=jax.ShapeDtypeStruct((m, d), BF16),
        compiler_params=_params(("parallel",), win),
        name="merge",
    )(attn, z, z, z, z, z, z, z, z, z, zm, zm, conv_w, norm_a.reshape(1, d), norm_c.reshape(1, d))


def _merge_meta_kernel(attn_ref, gb_ref, gc_ref, hc_ref, ga_ref, gcv_ref, gcn_ref, hcn_ref,
                       cw_ref, na_ref, nc_ref, o_ref):
    u_next = (gcn_ref[...] * hcn_ref[...])[0:1]
    u_prev = jnp.zeros_like(u_next)
    o_ref[...] = _merge_math(attn_ref[...], gb_ref[...], gc_ref[...], hc_ref[...], ga_ref[...], gcv_ref[...],
                             u_prev, u_next, cw_ref[...], na_ref[...], nc_ref[...]).astype(o_ref.dtype)


def _merge_meta(attn_m, z, zm, conv_w, norm_a, norm_c, *, seq_chunks, n_meta, cols):
    mm, d = attn_m.shape
    per_chunk = BLOCK // F32_SUBLANES
    gb, gc, hc, ga, gcv = [c // d for c in cols]

    def first_row8(s):
        r = 0
        for i, (s0, _) in enumerate(seq_chunks):
            r = r + jnp.where(s == i, s0 * per_chunk, 0)
        return r

    def main(col):
        return pl.BlockSpec((n_meta, d), lambda s: (s, col))

    def next8(col):
        return pl.BlockSpec((F32_SUBLANES, d), lambda s: (first_row8(s), col))

    def row(nrows):
        return pl.BlockSpec((nrows, d), lambda s: (0, 0))

    win = 7 * _nbytes((n_meta, d), F32) + 2 * _nbytes((F32_SUBLANES, d), F32)
    return pl.pallas_call(
        _merge_meta_kernel,
        grid=(len(seq_chunks),),
        in_specs=[main(0), main(gb), main(gc), main(hc), main(ga), main(gcv), next8(gc), next8(hc),
                  row(3), row(1), row(1)],
        out_specs=pl.BlockSpec((n_meta, d), lambda s: (s, 0)),
        out_shape=jax.ShapeDtypeStruct((mm, d), BF16),
        compiler_params=_params(("parallel",), win),
        name="merge_meta",
    )(attn_m, zm, zm, zm, zm, zm, z, z, conv_w, norm_a.reshape(1, d), norm_c.reshape(1, d))


def kernel(x_prompt, x_sample, meta_tokens, rel_bias, norm1_g, w_in, q_norm_g, k_norm_g, attn_sink, conv_w,
           branch_norm_a, branch_norm_c, w_out, norm2_g, w_ffn_gate, w_ffn_up, w_ffn_down):
    d = x_prompt.shape[-1]
    n_meta = meta_tokens.shape[0]
    n_buckets = rel_bias.shape[0]
    depth, _, in_dim = w_in.shape
    ffn = w_ffn_gate.shape[-1]
    kv_dim = (in_dim - 6 * d) // 2
    assert d % (GROUP * HEAD_DIM) == 0 and kv_dim == d // GROUP and attn_sink.shape[1] * HEAD_DIM == d
    assert n_meta % F32_SUBLANES == 0 and n_meta < BLOCK

    groups = (x_prompt, x_sample)
    seq_chunks = []
    for xg in groups:
        assert xg.shape[1] % BLOCK == 0 and xg.shape[1] >= 2 * BLOCK
        for _ in range(xg.shape[0]):
            start = seq_chunks[-1][0] + seq_chunks[-1][1] if seq_chunks else 0
            seq_chunks.append((start, xg.shape[1] // BLOCK))
    seq_chunks = tuple(seq_chunks)
    n_seq = len(seq_chunks)

    x = jnp.concatenate([xg.reshape(-1, d) for xg in groups], axis=0)
    xm = jnp.tile(meta_tokens.astype(x.dtype), (n_seq, 1))

    q_end, v_end = d, d + 2 * kv_dim
    cols = tuple(d * i for i in range(1, 6))
    k_col, v_col = 6 * d, 6 * d + kv_dim
    ffn_p = -(-ffn // 512) * 512

    def layer_weights(l):
        wi = w_in[l]
        w_in_b = jnp.concatenate([wi[:, :q_end], wi[:, v_end:], wi[:, q_end:v_end]], axis=-1).astype(BF16)
        wg_b = jnp.pad(w_ffn_gate[l], ((0, 0), (0, ffn_p - ffn))).astype(BF16)
        wu_b = jnp.pad(w_ffn_up[l], ((0, 0), (0, ffn_p - ffn))).astype(BF16)
        wd_b = jnp.pad(w_ffn_down[l], ((0, ffn_p - ffn), (0, 0))).astype(BF16)
        return w_in_b, w_out[l].astype(BF16), wg_b, wu_b, wd_b

    bkt_tok, bkt_met = _bucket_tables(n_meta, n_buckets)

    geo = dict(seq_chunks=seq_chunks, n_meta=n_meta)
    for l in range(depth):
        last_layer = l == depth - 1
        w_in_b, w_out_b, wg_b, wu_b, wd_b = layer_weights(l)
        table = jnp.concatenate([rel_bias, attn_sink[l][None]], axis=0)
        bias_tok = _bias_tiles(table, bkt_tok)
        z = _matmul(_rmsnorm(x, norm1_g[l]), w_in_b, F32)
        zm = _matmul(_rmsnorm(xm, norm1_g[l]), w_in_b, F32)
        attn = _attention(z, zm, bias_tok, q_norm_g[l], k_norm_g[l], d_model=d, k_col=k_col, v_col=v_col, **geo)
        merged = _merge(attn, z, zm, conv_w[l], branch_norm_a[l], branch_norm_c[l], cols=cols, **geo)
        h = _matmul_res(merged, w_out_b, x)
        t = _gateup(_rmsnorm(h, norm2_g[l]), wg_b, wu_b)
        x = _matmul_res(t, wd_b, h)
        if not last_layer:
            bias_met = _bias_tiles(table, bkt_met)
            attn_m = _attention_meta(z, zm, bias_met, q_norm_g[l], k_norm_g[l],
                                     d_model=d, k_col=k_col, v_col=v_col, **geo)
            merged_m = _merge_meta(attn_m, z, zm, conv_w[l], branch_norm_a[l], branch_norm_c[l], cols=cols, **geo)
            hm = _matmul_res(merged_m, w_out_b, xm)
            tm = _gateup(_rmsnorm(hm, norm2_g[l]), wg_b, wu_b)
            xm = _matmul_res(tm, wd_b, hm)

    outs = []
    row = 0
    for xg in groups:
        rows = xg.shape[0] * xg.shape[1]
        outs.append(x[row:row + rows].reshape(xg.shape))
        row += rows
    return tuple(outs)
```

```python
import functools
import math

import jax
import jax.numpy as jnp
from jax import lax
from jax.experimental import pallas as pl
from jax.experimental.pallas import tpu as pltpu

HEAD_DIM = 128
GROUP = 4
BLOCK = 128
MAX_DISTANCE = 128
EPS = 1e-6

V7X_VMEM_BYTES = 64 << 20
V7X_LANES = 128
BF16_SUBLANES = 16
F32_SUBLANES = 8
HALO_ROWS = BF16_SUBLANES
VMEM_SLACK_BYTES = 10 << 20

F32 = jnp.float32
BF16 = jnp.bfloat16


def _params(dims, window_bytes):
    limit = min(2 * window_bytes + VMEM_SLACK_BYTES, V7X_VMEM_BYTES - (6 << 20))
    return pltpu.CompilerParams(dimension_semantics=dims, vmem_limit_bytes=int(limit))


def _divisor_tile(n, cap, unit):
    if n <= cap:
        return n
    t = (cap // unit) * unit
    while t >= unit:
        if n % t == 0:
            return t
        t -= unit
    raise ValueError(f"no tile for {n}")


def _nbytes(shape, dtype):
    return math.prod(shape) * jnp.dtype(dtype).itemsize


def _rms(x, g):
    ms = jnp.mean(x * x, axis=-1, keepdims=True)
    return x * lax.rsqrt(ms + EPS) * g


def _rmsnorm_kernel(x_ref, g_ref, o_ref):
    o_ref[...] = _rms(x_ref[...], g_ref[...]).astype(o_ref.dtype)


def _rmsnorm(x, g):
    m, d = x.shape
    bm = _divisor_tile(m, 256, BF16_SUBLANES)
    return pl.pallas_call(
        _rmsnorm_kernel,
        grid=(m // bm,),
        in_specs=[pl.BlockSpec((bm, d), lambda i: (i, 0)),
                  pl.BlockSpec((1, d), lambda i: (0, 0))],
        out_specs=pl.BlockSpec((bm, d), lambda i: (i, 0)),
        out_shape=jax.ShapeDtypeStruct((m, d), BF16),
        compiler_params=_params(("parallel",), _nbytes((bm, d), F32) + _nbytes((bm, d), BF16)),
        name="rmsnorm",
    )(x, g.reshape(1, d))


def _cast_kernel(w_ref, o_ref, *, rows, cols, ragged):
    w = w_ref[...]
    if ragged:
        br, bc = w.shape
        r = pl.program_id(0) * br + lax.broadcasted_iota(jnp.int32, w.shape, 0)
        c = pl.program_id(1) * bc + lax.broadcasted_iota(jnp.int32, w.shape, 1)
        w = jnp.where(jnp.logical_and(r < rows, c < cols), w, 0.0)
    o_ref[...] = w.astype(o_ref.dtype)


def _cast_weight(w, layer, out_rows, out_cols, bc, src_block=None):
    _, rows, cols = w.shape
    br = _divisor_tile(out_rows, 1024, BF16_SUBLANES)
    ragged = rows % br != 0 or cols % bc != 0
    assert src_block is None or not ragged
    src = src_block if src_block is not None else (lambda j: j)
    return pl.pallas_call(
        functools.partial(_cast_kernel, rows=rows, cols=cols, ragged=ragged),
        grid=(out_rows // br, out_cols // bc),
        in_specs=[pl.BlockSpec((None, br, bc), lambda i, j: (layer, i, src(j)))],
        out_specs=pl.BlockSpec((br, bc), lambda i, j: (i, j)),
        out_shape=jax.ShapeDtypeStruct((out_rows, out_cols), BF16),
        compiler_params=_params(("parallel", "parallel"), _nbytes((br, bc), F32) + _nbytes((br, bc), BF16)),
        name="cast_weight",
    )(w)


def _mm_kernel(a_ref, b_ref, o_ref):
    o_ref[...] = jnp.dot(a_ref[...], b_ref[...], preferred_element_type=F32).astype(o_ref.dtype)


def _mm_res_kernel(a_ref, b_ref, r_ref, o_ref):
    @pl.when(pl.program_id(2) == 0)
    def _():
        o_ref[...] = r_ref[...]

    o_ref[...] += jnp.dot(a_ref[...], b_ref[...], preferred_element_type=F32)


def _gateup_kernel(a_ref, wg_ref, wu_ref, o_ref):
    a = a_ref[...]
    g = jnp.dot(a, wg_ref[...], preferred_element_type=F32)
    u = jnp.dot(a, wu_ref[...], preferred_element_type=F32)
    o_ref[...] = (g * jax.nn.sigmoid(g) * u).astype(o_ref.dtype)


def _matmul(a, b, out_dtype):
    m, k = a.shape
    n = b.shape[1]
    bm = _divisor_tile(m, 1024, BF16_SUBLANES)
    bn = _divisor_tile(n, 1024, V7X_LANES)
    win = _nbytes((bm, k), BF16) + _nbytes((k, bn), BF16) + _nbytes((bm, bn), out_dtype)
    return pl.pallas_call(
        _mm_kernel,
        grid=(m // bm, n // bn),
        in_specs=[pl.BlockSpec((bm, k), lambda i, j: (i, 0)),
                  pl.BlockSpec((k, bn), lambda i, j: (0, j))],
        out_specs=pl.BlockSpec((bm, bn), lambda i, j: (i, j)),
        out_shape=jax.ShapeDtypeStruct((m, n), out_dtype),
        compiler_params=_params(("parallel", "parallel"), win),
        name="matmul",
    )(a, b)


def _matmul_res(a, b, res):
    m, k = a.shape
    n = b.shape[1]
    bm = _divisor_tile(m, 1024, BF16_SUBLANES)
    bn = _divisor_tile(n, 1024, V7X_LANES)
    bk = _divisor_tile(k, 4096, V7X_LANES)
    win = _nbytes((bm, bk), BF16) + _nbytes((bk, bn), BF16) + 2 * _nbytes((bm, bn), F32)
    return pl.pallas_call(
        _mm_res_kernel,
        grid=(m // bm, n // bn, k // bk),
        in_specs=[pl.BlockSpec((bm, bk), lambda i, j, kk: (i, kk)),
                  pl.BlockSpec((bk, bn), lambda i, j, kk: (kk, j)),
                  pl.BlockSpec((bm, bn), lambda i, j, kk: (i, j))],
        out_specs=pl.BlockSpec((bm, bn), lambda i, j, kk: (i, j)),
        out_shape=jax.ShapeDtypeStruct((m, n), F32),
        compiler_params=_params(("parallel", "parallel", "arbitrary"), win),
        name="matmul_res",
    )(a, b, res)


def _gateup(a, wg, wu):
    m, k = a.shape
    n = wg.shape[1]
    bm = _divisor_tile(m, 1024, BF16_SUBLANES)
    bn = _divisor_tile(n, 512, V7X_LANES)
    win = _nbytes((bm, k), BF16) + 2 * _nbytes((k, bn), BF16) + _nbytes((bm, bn), BF16)
    return pl.pallas_call(
        _gateup_kernel,
        grid=(m // bm, n // bn),
        in_specs=[pl.BlockSpec((bm, k), lambda i, j: (i, 0)),
                  pl.BlockSpec((k, bn), lambda i, j: (0, j)),
                  pl.BlockSpec((k, bn), lambda i, j: (0, j))],
        out_specs=pl.BlockSpec((bm, bn), lambda i, j: (i, j)),
        out_shape=jax.ShapeDtypeStruct((m, n), BF16),
        compiler_params=_params(("parallel", "parallel"), win),
        name="gateup",
    )(a, wg, wu)


def _t5_bucket(rel, n_buckets):
    half = n_buckets // 2
    exact = half // 2
    n = jnp.abs(rel)
    n_f = jnp.maximum(n, 1).astype(F32)
    large = exact + (jnp.log(n_f / exact) / math.log(MAX_DISTANCE / exact) * (half - exact)).astype(jnp.int32)
    large = jnp.minimum(large, half - 1)
    return jnp.where(rel > 0, half, 0) + jnp.where(n < exact, n, large)


def _bucket_tables(n_meta, n_buckets):
    qi = jnp.arange(BLOCK)[:, None]
    sj = jnp.arange(3 * BLOCK)[None, :]
    rel_band = sj - BLOCK - qi
    band = jnp.where(jnp.abs(rel_band) <= BLOCK, _t5_bucket(rel_band, n_buckets), -1)
    masked = jnp.full((BLOCK, BLOCK), -1, jnp.int32)
    sink = jnp.full((BLOCK, 1), n_buckets, jnp.int32)
    pad = jnp.full((BLOCK, BLOCK - n_meta - 1), -1, jnp.int32)
    mk = jnp.arange(n_meta)[None, :]
    meta_first = _t5_bucket(mk - (n_meta + qi), n_buckets)
    meta_far = jnp.full((BLOCK, n_meta), n_buckets // 2 - 1, jnp.int32)
    interior = jnp.concatenate([band, meta_far, sink, pad], axis=1)
    first = jnp.concatenate([masked, band[:, BLOCK:], meta_first, sink, pad], axis=1)
    last = jnp.concatenate([band[:, :2 * BLOCK], masked, meta_far, sink, pad], axis=1)
    tok = jnp.stack([interior, first, last]).astype(jnp.int32)

    mq = jnp.arange(n_meta)[:, None]
    tk = jnp.arange(BLOCK)[None, :]
    rel_tok = n_meta + tk - mq
    mband = jnp.where(jnp.abs(rel_tok) <= BLOCK, _t5_bucket(rel_tok, n_buckets), -1)
    mmeta = _t5_bucket(mk - mq, n_buckets)
    met = jnp.concatenate([mband, mmeta, sink[:n_meta], pad[:n_meta]], axis=1).astype(jnp.int32)
    return tok, met[None]


def _bias_kernel(tab_ref, bkt_ref, o_ref, *, n_ids):
    h = pl.program_id(0)
    bkt = bkt_ref[...]
    acc = jnp.full(bkt.shape, -jnp.inf, F32)
    for b in range(n_ids):
        acc = jnp.where(bkt == b, tab_ref[b, h], acc)
    o_ref[0, :, 0] = acc


def _bias_tiles(table, bkt):
    n_ids, n_heads = table.shape
    v, r, c = bkt.shape
    return pl.pallas_call(
        functools.partial(_bias_kernel, n_ids=n_ids),
        grid=(n_heads,),
        in_specs=[pl.BlockSpec(memory_space=pltpu.SMEM),
                  pl.BlockSpec((v, r, c), lambda h: (0, 0, 0))],
        out_specs=pl.BlockSpec((1, v, 1, r, c), lambda h: (h // GROUP, 0, h % GROUP, 0, 0)),
        out_shape=jax.ShapeDtypeStruct((n_heads // GROUP, v, GROUP, r, c), F32),
        compiler_params=_params(("parallel",), 2 * _nbytes(bkt.shape, F32)),
        name="bias_tiles",
    )(table, bkt)


def _softmax_pv(q, k, v, bias, scale):
    s = lax.dot_general(q, k, (((1,), (1,)), ((), ())), preferred_element_type=F32) * scale + bias
    m = jnp.max(s, axis=-1, keepdims=True)
    p = jnp.exp(s - m)
    denom = jnp.sum(p, axis=-1, keepdims=True)
    pn = (p * (1.0 / denom)).astype(BF16)
    return jnp.dot(pn, v, preferred_element_type=F32)


def _fill_meta_keys(km_scr, vm_scr, km_ref, vm_ref, kg, n_meta):
    km_scr[...] = jnp.zeros(km_scr.shape, BF16)
    vm_scr[...] = jnp.zeros(vm_scr.shape, BF16)
    km_scr[0:n_meta] = _rms(km_ref[...].astype(F32), kg).astype(BF16)
    vm_scr[0:n_meta] = vm_ref[...].astype(BF16)


def _attn_kernel(qg_ref, kg_ref, bias_ref, q_ref, kp_ref, kc_ref, kn_ref, vp_ref, vc_ref, vn_ref,
                 km_ref, vm_ref, o_ref, q_scr, k_scr, v_scr, km_scr, vm_scr, *, chunks, seq_chunks, n_meta, scale):
    r = pl.program_id(1)
    qg = qg_ref[...]
    kg = kg_ref[...]
    body = chunks * BLOCK
    stack = GROUP * BLOCK

    k_scr[0:BLOCK] = _rms(kp_ref[...].astype(F32), kg).astype(BF16)
    k_scr[BLOCK:BLOCK + body] = _rms(kc_ref[...].astype(F32), kg).astype(BF16)
    k_scr[BLOCK + body:2 * BLOCK + body] = _rms(kn_ref[...].astype(F32), kg).astype(BF16)
    v_scr[0:BLOCK] = vp_ref[...].astype(BF16)
    v_scr[BLOCK:BLOCK + body] = vc_ref[...].astype(BF16)
    v_scr[BLOCK + body:2 * BLOCK + body] = vn_ref[...].astype(BF16)
    _fill_meta_keys(km_scr, vm_scr, km_ref, vm_ref, kg, n_meta)
    for g in range(GROUP):
        qn = _rms(q_ref[:, g * HEAD_DIM:(g + 1) * HEAD_DIM].astype(F32), qg).astype(BF16)
        for c in range(chunks):
            q_scr[(c * GROUP + g) * BLOCK:(c * GROUP + g + 1) * BLOCK] = qn[c * BLOCK:(c + 1) * BLOCK]
    k_meta = km_scr[...]
    v_meta = vm_scr[...]

    def chunk(cc, carry):
        gch = r * chunks + cc
        is_first = functools.reduce(jnp.logical_or, [gch == s0 for s0, _ in seq_chunks])
        is_last = functools.reduce(jnp.logical_or, [gch == s0 + n - 1 for s0, n in seq_chunks])
        var = jnp.where(is_first, 1, jnp.where(is_last, 2, 0))
        row0 = pl.multiple_of(cc * BLOCK, BLOCK)
        k_all = jnp.concatenate([k_scr[pl.ds(row0, 3 * BLOCK), :], k_meta], axis=0)
        v_all = jnp.concatenate([v_scr[pl.ds(row0, 3 * BLOCK), :], v_meta], axis=0)
        q = q_scr[pl.ds(pl.multiple_of(cc * stack, stack), stack), :]
        bias = bias_ref[0, var].reshape(stack, 4 * BLOCK)
        o = _softmax_pv(q, k_all, v_all, bias, scale)
        for g in range(GROUP):
            o_ref[pl.ds(row0, BLOCK), g * HEAD_DIM:(g + 1) * HEAD_DIM] = o[g * BLOCK:(g + 1) * BLOCK]
        return carry

    lax.fori_loop(0, chunks, chunk, 0, unroll=True)


def _seq_of_chunk(c, seq_chunks):
    s = 0
    for s0, _ in seq_chunks[1:]:
        s = s + (c >= s0).astype(jnp.int32)
    return s


def _attention(z, zm, bias_tok, q_g, k_g, *, seq_chunks, n_meta, d_model, k_col, v_col):
    m = z.shape[0]
    n_chunks = m // BLOCK
    kv_heads = d_model // (GROUP * HEAD_DIM)
    chunks = math.gcd(8, *[n for _, n in seq_chunks])
    body = chunks * BLOCK
    qw = GROUP * HEAD_DIM
    kb, vb = k_col // HEAD_DIM, v_col // HEAD_DIM
    seq_of = functools.partial(_seq_of_chunk, seq_chunks=seq_chunks)

    def halo_prev(h, r):
        return jnp.maximum(r * chunks - 1, 0)

    def halo_next(h, r):
        return jnp.minimum(r * chunks + chunks, n_chunks - 1)

    in_specs = [
        pl.BlockSpec((1, HEAD_DIM), lambda h, r: (0, 0)),
        pl.BlockSpec((1, HEAD_DIM), lambda h, r: (0, 0)),
        pl.BlockSpec((1, 3, GROUP, BLOCK, 4 * BLOCK), lambda h, r: (h, 0, 0, 0, 0)),
        pl.BlockSpec((body, qw), lambda h, r: (r, h)),
        pl.BlockSpec((BLOCK, HEAD_DIM), lambda h, r: (halo_prev(h, r), kb + h)),
        pl.BlockSpec((body, HEAD_DIM), lambda h, r: (r, kb + h)),
        pl.BlockSpec((BLOCK, HEAD_DIM), lambda h, r: (halo_next(h, r), kb + h)),
        pl.BlockSpec((BLOCK, HEAD_DIM), lambda h, r: (halo_prev(h, r), vb + h)),
        pl.BlockSpec((body, HEAD_DIM), lambda h, r: (r, vb + h)),
        pl.BlockSpec((BLOCK, HEAD_DIM), lambda h, r: (halo_next(h, r), vb + h)),
        pl.BlockSpec((n_meta, HEAD_DIM), lambda h, r: (seq_of(r * chunks), kb + h)),
        pl.BlockSpec((n_meta, HEAD_DIM), lambda h, r: (seq_of(r * chunks), vb + h)),
    ]
    win = (_nbytes((3, GROUP, BLOCK, 4 * BLOCK), F32) + 2 * _nbytes((body, qw), F32)
           + 4 * _nbytes((body + 2 * BLOCK, HEAD_DIM), F32))
    return pl.pallas_call(
        functools.partial(_attn_kernel, chunks=chunks, seq_chunks=seq_chunks, n_meta=n_meta,
                          scale=HEAD_DIM ** -0.5),
        grid=(kv_heads, m // body),
        in_specs=in_specs,
        out_specs=pl.BlockSpec((body, qw), lambda h, r: (r, h)),
        out_shape=jax.ShapeDtypeStruct((m, d_model), F32),
        scratch_shapes=[pltpu.VMEM((body * GROUP, HEAD_DIM), BF16),
                        pltpu.VMEM((body + 2 * BLOCK, HEAD_DIM), BF16),
                        pltpu.VMEM((body + 2 * BLOCK, HEAD_DIM), BF16),
                        pltpu.VMEM((BLOCK, HEAD_DIM), BF16),
                        pltpu.VMEM((BLOCK, HEAD_DIM), BF16)],
        compiler_params=_params(("parallel", "arbitrary"), win),
        name="attention",
    )(q_g.reshape(1, HEAD_DIM), k_g.reshape(1, HEAD_DIM), bias_tok, z, z, z, z, z, z, z, zm, zm)


def _attn_meta_kernel(qg_ref, kg_ref, bias_ref, q_ref, k1_ref, v1_ref, km_ref, vm_ref, o_ref,
                      km_scr, vm_scr, *, n_meta, scale):
    qg = qg_ref[...]
    kg = kg_ref[...]
    _fill_meta_keys(km_scr, vm_scr, km_ref, vm_ref, kg, n_meta)
    k_all = jnp.concatenate([_rms(k1_ref[...].astype(F32), kg).astype(BF16), km_scr[...]], axis=0)
    v_all = jnp.concatenate([v1_ref[...].astype(BF16), vm_scr[...]], axis=0)
    q = q_ref[...].astype(F32)
    qn = jnp.concatenate([_rms(q[:, g * HEAD_DIM:(g + 1) * HEAD_DIM], qg).astype(BF16) for g in range(GROUP)], axis=0)
    o = _softmax_pv(qn, k_all, v_all, bias_ref[0, 0].reshape(GROUP * n_meta, 2 * BLOCK), scale)
    for g in range(GROUP):
        o_ref[:, g * HEAD_DIM:(g + 1) * HEAD_DIM] = o[g * n_meta:(g + 1) * n_meta]


def _attention_meta(z, zm, bias_met, q_g, k_g, *, seq_chunks, n_meta, d_model, k_col, v_col):
    n_seq = len(seq_chunks)
    kv_heads = d_model // (GROUP * HEAD_DIM)
    qw = GROUP * HEAD_DIM
    kb, vb = k_col // HEAD_DIM, v_col // HEAD_DIM

    def first_chunk(s):
        c = 0
        for i, (s0, _) in enumerate(seq_chunks):
            c = c + jnp.where(s == i, s0, 0)
        return c

    in_specs = [
        pl.BlockSpec((1, HEAD_DIM), lambda s, h: (0, 0)),
        pl.BlockSpec((1, HEAD_DIM), lambda s, h: (0, 0)),
        pl.BlockSpec((1, 1, GROUP, n_meta, 2 * BLOCK), lambda s, h: (h, 0, 0, 0, 0)),
        pl.BlockSpec((n_meta, qw), lambda s, h: (s, h)),
        pl.BlockSpec((BLOCK, HEAD_DIM), lambda s, h: (first_chunk(s), kb + h)),
        pl.BlockSpec((BLOCK, HEAD_DIM), lambda s, h: (first_chunk(s), vb + h)),
        pl.BlockSpec((n_meta, HEAD_DIM), lambda s, h: (s, kb + h)),
        pl.BlockSpec((n_meta, HEAD_DIM), lambda s, h: (s, vb + h)),
    ]
    return pl.pallas_call(
        functools.partial(_attn_meta_kernel, n_meta=n_meta, scale=HEAD_DIM ** -0.5),
        grid=(n_seq, kv_heads),
        in_specs=in_specs,
        out_specs=pl.BlockSpec((n_meta, qw), lambda s, h: (s, h)),
        out_shape=jax.ShapeDtypeStruct((n_seq * n_meta, d_model), F32),
        scratch_shapes=[pltpu.VMEM((BLOCK, HEAD_DIM), BF16), pltpu.VMEM((BLOCK, HEAD_DIM), BF16)],
        compiler_params=_params(("parallel", "parallel"), 1 << 20),
        name="attention_meta",
    )(q_g.reshape(1, HEAD_DIM), k_g.reshape(1, HEAD_DIM), bias_met, zm, z, z, zm, zm)


def _merge_math(attn, gb, gc, hc, ga, gcv, u_prev, u_next, cw, na, nc):
    u = gc * hc
    rows = u.shape[0]
    ridx = lax.broadcasted_iota(jnp.int32, u.shape, 0)
    up = jnp.where(ridx == 0, u_prev, pltpu.roll(u, 1, axis=0))
    un = jnp.where(ridx == rows - 1, u_next, pltpu.roll(u, rows - 1, axis=0))
    conv = gb * (cw[0:1] * up + cw[1:2] * u + cw[2:3] * un)
    return jax.nn.sigmoid(ga) * _rms(attn, na) + jax.nn.sigmoid(gcv) * _rms(conv, nc)


def _merge_kernel(attn_ref, gb_ref, gc_ref, hc_ref, ga_ref, gcv_ref, gcp_ref, hcp_ref, gcn_ref, hcn_ref,
                  gcm_ref, hcm_ref, cw_ref, na_ref, nc_ref, o_ref, *, seq_chunks):
    c = pl.program_id(0)
    is_first = functools.reduce(jnp.logical_or, [c == s0 for s0, _ in seq_chunks])
    is_last = functools.reduce(jnp.logical_or, [c == s0 + n - 1 for s0, n in seq_chunks])
    last = HALO_ROWS - 1
    f32 = lambda ref: ref[...].astype(F32)
    u_prev_tok = (f32(gcp_ref) * f32(hcp_ref))[last:last + 1]
    u_prev_meta = (f32(gcm_ref) * f32(hcm_ref))[last:last + 1]
    u_prev = jnp.where(is_first, u_prev_meta, u_prev_tok)
    u_next_tok = (f32(gcn_ref) * f32(hcn_ref))[0:1]
    u_next = jnp.where(is_last, jnp.zeros_like(u_next_tok), u_next_tok)
    o_ref[...] = _merge_math(attn_ref[...], f32(gb_ref), f32(gc_ref), f32(hc_ref), f32(ga_ref), f32(gcv_ref),
                             u_prev, u_next, cw_ref[...], na_ref[...], nc_ref[...]).astype(o_ref.dtype)


def _merge(attn, z, zm, conv_w, norm_a, norm_c, *, seq_chunks, n_meta, cols):
    m, d = attn.shape
    n_chunks = m // BLOCK
    per_chunk = BLOCK // HALO_ROWS
    per_meta = n_meta // HALO_ROWS
    gb, gc, hc, ga, gcv = [c // d for c in cols]
    seq_of = functools.partial(_seq_of_chunk, seq_chunks=seq_chunks)

    def main(col):
        return pl.BlockSpec((BLOCK, d), lambda c: (c, col))

    def prev_rows(col):
        return pl.BlockSpec((HALO_ROWS, d), lambda c: (jnp.maximum(c * per_chunk - 1, 0), col))

    def next_rows(col):
        return pl.BlockSpec((HALO_ROWS, d), lambda c: (jnp.minimum(c + 1, n_chunks - 1) * per_chunk, col))

    def meta_rows(col):
        return pl.BlockSpec((HALO_ROWS, d), lambda c: (seq_of(c) * per_meta + per_meta - 1, col))

    def row(nrows):
        return pl.BlockSpec((nrows, d), lambda c: (0, 0))

    win = 6 * _nbytes((BLOCK, d), F32) + _nbytes((BLOCK, d), BF16) + 8 * _nbytes((HALO_ROWS, d), F32)
    return pl.pallas_call(
        functools.partial(_merge_kernel, seq_chunks=seq_chunks),
        grid=(n_chunks,),
        in_specs=[main(0), main(gb), main(gc), main(hc), main(ga), main(gcv),
                  prev_rows(gc), prev_rows(hc), next_rows(gc), next_rows(hc), meta_rows(gc), meta_rows(hc),
                  row(3), row(1), row(1)],
        out_specs=pl.BlockSpec((BLOCK, d), lambda c: (c, 0)),
        out_shape=jax.ShapeDtypeStruct((m, d), BF16),
        compiler_params=_params(("parallel",), win),
        name="merge",
    )(attn, z, z, z, z, z, z, z, z, z, zm, zm, conv_w, norm_a.reshape(1, d), norm_c.reshape(1, d))


def _merge_meta_kernel(attn_ref, gb_ref, gc_ref, hc_ref, ga_ref, gcv_ref, gcn_ref, hcn_ref,
                       cw_ref, na_ref, nc_ref, o_ref):
    f32 = lambda ref: ref[...].astype(F32)
    u_next = (f32(gcn_ref) * f32(hcn_ref))[0:1]
    u_prev = jnp.zeros_like(u_next)
    o_ref[...] = _merge_math(attn_ref[...], f32(gb_ref), f32(gc_ref), f32(hc_ref), f32(ga_ref), f32(gcv_ref),
                             u_prev, u_next, cw_ref[...], na_ref[...], nc_ref[...]).astype(o_ref.dtype)


def _merge_meta(attn_m, z, zm, conv_w, norm_a, norm_c, *, seq_chunks, n_meta, cols):
    mm, d = attn_m.shape
    per_chunk = BLOCK // HALO_ROWS
    gb, gc, hc, ga, gcv = [c // d for c in cols]

    def first_rows(s):
        r = 0
        for i, (s0, _) in enumerate(seq_chunks):
            r = r + jnp.where(s == i, s0 * per_chunk, 0)
        return r

    def main(col):
        return pl.BlockSpec((n_meta, d), lambda s: (s, col))

    def next_rows(col):
        return pl.BlockSpec((HALO_ROWS, d), lambda s: (first_rows(s), col))

    def row(nrows):
        return pl.BlockSpec((nrows, d), lambda s: (0, 0))

    win = 7 * _nbytes((n_meta, d), F32) + 2 * _nbytes((HALO_ROWS, d), F32)
    return pl.pallas_call(
        _merge_meta_kernel,
        grid=(len(seq_chunks),),
        in_specs=[main(0), main(gb), main(gc), main(hc), main(ga), main(gcv), next_rows(gc), next_rows(hc),
                  row(3), row(1), row(1)],
        out_specs=pl.BlockSpec((n_meta, d), lambda s: (s, 0)),
        out_shape=jax.ShapeDtypeStruct((mm, d), BF16),
        compiler_params=_params(("parallel",), win),
        name="merge_meta",
    )(attn_m, zm, zm, zm, zm, zm, z, z, conv_w, norm_a.reshape(1, d), norm_c.reshape(1, d))


def kernel(x_prompt, x_sample, meta_tokens, rel_bias, norm1_g, w_in, q_norm_g, k_norm_g, attn_sink, conv_w,
           branch_norm_a, branch_norm_c, w_out, norm2_g, w_ffn_gate, w_ffn_up, w_ffn_down):
    d = x_prompt.shape[-1]
    n_meta = meta_tokens.shape[0]
    n_buckets = rel_bias.shape[0]
    depth, _, in_dim = w_in.shape
    ffn = w_ffn_gate.shape[-1]
    kv_dim = (in_dim - 6 * d) // 2
    assert d % (GROUP * HEAD_DIM) == 0 and kv_dim == d // GROUP and attn_sink.shape[1] * HEAD_DIM == d
    assert n_meta % HALO_ROWS == 0 and n_meta < BLOCK

    groups = (x_prompt, x_sample)
    seq_chunks = []
    for xg in groups:
        assert xg.shape[1] % BLOCK == 0 and xg.shape[1] >= 2 * BLOCK
        for _ in range(xg.shape[0]):
            start = seq_chunks[-1][0] + seq_chunks[-1][1] if seq_chunks else 0
            seq_chunks.append((start, xg.shape[1] // BLOCK))
    seq_chunks = tuple(seq_chunks)
    n_seq = len(seq_chunks)

    x = jnp.concatenate([xg.reshape(-1, d) for xg in groups], axis=0)
    xm = jnp.tile(meta_tokens.astype(x.dtype), (n_seq, 1))

    q_end, v_end = d, d + 2 * kv_dim
    cols = tuple(d * i for i in range(1, 6))
    k_col, v_col = 6 * d, 6 * d + kv_dim
    ffn_p = -(-ffn // 512) * 512

    n_q, n_rest = d // kv_dim, 5 * d // kv_dim

    def regroup(j):
        return jnp.where(j < n_q, j, jnp.where(j < n_q + n_rest, j + 2, j - n_rest))

    def layer_weights(l):
        w_in_b = _cast_weight(w_in, l, d, in_dim, kv_dim, regroup)
        w_out_b = _cast_weight(w_out, l, d, d, _divisor_tile(d, 1024, V7X_LANES))
        wg_b = _cast_weight(w_ffn_gate, l, d, ffn_p, 512)
        wu_b = _cast_weight(w_ffn_up, l, d, ffn_p, 512)
        wd_b = _cast_weight(w_ffn_down, l, ffn_p, d, _divisor_tile(d, 1024, V7X_LANES))
        return w_in_b, w_out_b, wg_b, wu_b, wd_b

    bkt_tok, bkt_met = _bucket_tables(n_meta, n_buckets)

    geo = dict(seq_chunks=seq_chunks, n_meta=n_meta)
    for l in range(depth):
        last_layer = l == depth - 1
        w_in_b, w_out_b, wg_b, wu_b, wd_b = layer_weights(l)
        table = jnp.concatenate([rel_bias, attn_sink[l][None]], axis=0)
        bias_tok = _bias_tiles(table, bkt_tok)
        z = _matmul(_rmsnorm(x, norm1_g[l]), w_in_b, BF16)
        zm = _matmul(_rmsnorm(xm, norm1_g[l]), w_in_b, BF16)
        attn = _attention(z, zm, bias_tok, q_norm_g[l], k_norm_g[l], d_model=d, k_col=k_col, v_col=v_col, **geo)
        merged = _merge(attn, z, zm, conv_w[l], branch_norm_a[l], branch_norm_c[l], cols=cols, **geo)
        h = _matmul_res(merged, w_out_b, x)
        t = _gateup(_rmsnorm(h, norm2_g[l]), wg_b, wu_b)
        x = _matmul_res(t, wd_b, h)
        if not last_layer:
            bias_met = _bias_tiles(table, bkt_met)
            attn_m = _attention_meta(z, zm, bias_met, q_norm_g[l], k_norm_g[l],
                                     d_model=d, k_col=k_col, v_col=v_col, **geo)
            merged_m = _merge_meta(attn_m, z, zm, conv_w[l], branch_norm_a[l], branch_norm_c[l], cols=cols, **geo)
            hm = _matmul_res(merged_m, w_out_b, xm)
            tm = _gateup(_rmsnorm(hm, norm2_g[l]), wg_b, wu_b)
            xm = _matmul_res(tm, wd_b, hm)

    outs = []
    row = 0
    for xg in groups:
        rows = xg.shape[0] * xg.shape[1]
        outs.append(x[row:row + rows].reshape(xg.shape))
        row += rows
    return tuple(outs)
```

```python
import functools
import math

import jax
import jax.numpy as jnp
from jax import lax
from jax.experimental import pallas as pl
from jax.experimental.pallas import tpu as pltpu

HEAD_DIM = 128
GROUP = 4
BLOCK = 128
MAX_DISTANCE = 128
EPS = 1e-6

V7X_VMEM_BYTES = 64 << 20
V7X_LANES = 128
BF16_SUBLANES = 16
F32_SUBLANES = 8
HALO_ROWS = BF16_SUBLANES
VMEM_SLACK_BYTES = 10 << 20

F32 = jnp.float32
BF16 = jnp.bfloat16


def _params(dims, window_bytes):
    limit = min(2 * window_bytes + VMEM_SLACK_BYTES, V7X_VMEM_BYTES - (6 << 20))
    return pltpu.CompilerParams(dimension_semantics=dims, vmem_limit_bytes=int(limit))


def _divisor_tile(n, cap, unit):
    if n <= cap:
        return n
    t = (cap // unit) * unit
    while t >= unit:
        if n % t == 0:
            return t
        t -= unit
    raise ValueError(f"no tile for {n}")


def _nbytes(shape, dtype):
    return math.prod(shape) * jnp.dtype(dtype).itemsize


def _rms(x, g):
    ms = jnp.mean(x * x, axis=-1, keepdims=True)
    return x * lax.rsqrt(ms + EPS) * g


def _rmsnorm_kernel(x_ref, g_ref, o_ref):
    o_ref[...] = _rms(x_ref[...], g_ref[...]).astype(o_ref.dtype)


def _rmsnorm(x, g):
    m, d = x.shape
    bm = _divisor_tile(m, 256, BF16_SUBLANES)
    return pl.pallas_call(
        _rmsnorm_kernel,
        grid=(m // bm,),
        in_specs=[pl.BlockSpec((bm, d), lambda i: (i, 0)),
                  pl.BlockSpec((1, d), lambda i: (0, 0))],
        out_specs=pl.BlockSpec((bm, d), lambda i: (i, 0)),
        out_shape=jax.ShapeDtypeStruct((m, d), BF16),
        compiler_params=_params(("parallel",), _nbytes((bm, d), F32) + _nbytes((bm, d), BF16)),
        name="rmsnorm",
    )(x, g.reshape(1, d))


def _cast_kernel(w_ref, o_ref, *, rows, cols, ragged):
    w = w_ref[...]
    if ragged:
        br, bc = w.shape
        r = pl.program_id(0) * br + lax.broadcasted_iota(jnp.int32, w.shape, 0)
        c = pl.program_id(1) * bc + lax.broadcasted_iota(jnp.int32, w.shape, 1)
        w = jnp.where(jnp.logical_and(r < rows, c < cols), w, 0.0)
    o_ref[...] = w.astype(o_ref.dtype)


def _cast_weight(w, layer, out_rows, out_cols, bc, src_block=None):
    _, rows, cols = w.shape
    br = _divisor_tile(out_rows, 1024, BF16_SUBLANES)
    ragged = rows % br != 0 or cols % bc != 0
    assert src_block is None or not ragged
    src = src_block if src_block is not None else (lambda j: j)
    return pl.pallas_call(
        functools.partial(_cast_kernel, rows=rows, cols=cols, ragged=ragged),
        grid=(out_rows // br, out_cols // bc),
        in_specs=[pl.BlockSpec((None, br, bc), lambda i, j: (layer, i, src(j)))],
        out_specs=pl.BlockSpec((br, bc), lambda i, j: (i, j)),
        out_shape=jax.ShapeDtypeStruct((out_rows, out_cols), BF16),
        compiler_params=_params(("parallel", "parallel"), _nbytes((br, bc), F32) + _nbytes((br, bc), BF16)),
        name="cast_weight",
    )(w)


def _wmm_kernel(*refs, n_w, has_res, has_aux, act, n_tiles, rows, src_cols):
    it = iter(refs)
    a_ref = next(it)
    aux_ref = next(it) if has_aux else None
    w_refs = [next(it) for _ in range(n_w)]
    res_ref = next(it) if has_res else None
    auxres_ref = next(it) if has_res and has_aux else None
    o_ref = next(it)
    oaux_ref = next(it) if has_aux else None
    w_scr = next(it)
    g = pl.program_id(0)
    i = pl.program_id(1)
    bn = w_scr.shape[-1]

    @pl.when(g < n_tiles)
    def _stage():
        row0 = pl.multiple_of(i * rows, rows)
        for t in range(n_w):
            w = w_refs[t][...]
            if src_cols % bn:
                col = g * bn + lax.broadcasted_iota(jnp.int32, w.shape, 1)
                w = jnp.where(col < src_cols, w, 0.0)
            w_scr[g % 2, t, pl.ds(row0, rows), :] = w.astype(w_scr.dtype)

    def apply(a, res):
        ys = [jnp.dot(a, w_scr[(g + 1) % 2, t], preferred_element_type=F32) for t in range(n_w)]
        y = ys[0] * jax.nn.sigmoid(ys[0]) * ys[1] if act else ys[0]
        return y if res is None else res + y

    @pl.when(g > 0)
    def _compute():
        o_ref[...] = apply(a_ref[...], res_ref[...] if has_res else None).astype(o_ref.dtype)
        if has_aux:
            @pl.when(i == 0)
            def _():
                oaux_ref[...] = apply(aux_ref[...], auxres_ref[...] if has_res else None).astype(oaux_ref.dtype)


def _wmm(a, aux, ws, layer, n_out, bn, out_dtype, *, src_block=None, act=False, res=None, auxres=None):
    m, k = a.shape
    src_cols = ws[0].shape[2]
    bm = _divisor_tile(m, 1024, BF16_SUBLANES)
    ni, n_tiles = m // bm, n_out // bn
    rows = k // ni
    assert k % ni == 0 and rows % BF16_SUBLANES == 0 and n_out % bn == 0
    assert src_block is None or src_cols % bn == 0
    src = src_block if src_block is not None else (lambda j: j)
    has_aux, has_res = aux is not None, res is not None
    ma = aux.shape[0] if has_aux else 0

    def row_blk(g, i):
        return jnp.where(g == 0, 0, i)

    def col_blk(g):
        return jnp.maximum(g - 1, 0)

    in_specs = [pl.BlockSpec((bm, k), lambda g, i: (row_blk(g, i), 0))]
    args = [a]
    if has_aux:
        in_specs.append(pl.BlockSpec((ma, k), lambda g, i: (0, 0)))
        args.append(aux)
    for w in ws:
        in_specs.append(pl.BlockSpec((None, rows, bn), lambda g, i: (layer, i, src(jnp.minimum(g, n_tiles - 1)))))
        args.append(w)
    if has_res:
        in_specs.append(pl.BlockSpec((bm, bn), lambda g, i: (row_blk(g, i), col_blk(g))))
        args.append(res)
        if has_aux:
            in_specs.append(pl.BlockSpec((ma, bn), lambda g, i: (0, col_blk(g))))
            args.append(auxres)
    out_specs = [pl.BlockSpec((bm, bn), lambda g, i: (row_blk(g, i), col_blk(g)))]
    out_shape = [jax.ShapeDtypeStruct((m, n_out), out_dtype)]
    if has_aux:
        out_specs.append(pl.BlockSpec((ma, bn), lambda g, i: (0, col_blk(g))))
        out_shape.append(jax.ShapeDtypeStruct((ma, n_out), out_dtype))
    win = (_nbytes((bm + ma, k), BF16) + len(ws) * _nbytes((rows, bn), F32)
           + _nbytes((bm + ma, bn), out_dtype) + has_res * _nbytes((bm + ma, bn), F32))
    scratch = _nbytes((2, len(ws), k, bn), BF16)
    return pl.pallas_call(
        functools.partial(_wmm_kernel, n_w=len(ws), has_res=has_res, has_aux=has_aux, act=act,
                          n_tiles=n_tiles, rows=rows, src_cols=src_cols),
        grid=(n_tiles + 1, ni),
        in_specs=in_specs,
        out_specs=out_specs,
        out_shape=out_shape,
        scratch_shapes=[pltpu.VMEM((2, len(ws), k, bn), BF16)],
        compiler_params=_params(("arbitrary", "arbitrary"), win + scratch // 2),
        name="wmm",
    )(*args)


def _mm_res_kernel(*refs, has_aux):
    if has_aux:
        a_ref, aux_ref, b_ref, r_ref, auxr_ref, o_ref, oaux_ref = refs
    else:
        a_ref, b_ref, r_ref, o_ref = refs
    kk = pl.program_id(2)
    b = b_ref[...]

    @pl.when(kk == 0)
    def _():
        o_ref[...] = r_ref[...]

    o_ref[...] += jnp.dot(a_ref[...], b, preferred_element_type=F32)
    if has_aux:
        @pl.when(pl.program_id(1) == 0)
        def _():
            @pl.when(kk == 0)
            def _():
                oaux_ref[...] = auxr_ref[...]

            oaux_ref[...] += jnp.dot(aux_ref[...], b, preferred_element_type=F32)


def _matmul_res(a, b, res, aux=None, auxres=None):
    m, k = a.shape
    n = b.shape[1]
    bm = _divisor_tile(m, 1024, BF16_SUBLANES)
    bn = _divisor_tile(n, 1024, V7X_LANES)
    bk = _divisor_tile(k, 4096, V7X_LANES)
    has_aux = aux is not None
    ma = aux.shape[0] if has_aux else 0
    in_specs = [pl.BlockSpec((bm, bk), lambda j, i, kk: (i, kk))]
    args = [a]
    if has_aux:
        in_specs.append(pl.BlockSpec((ma, bk), lambda j, i, kk: (0, kk)))
        args.append(aux)
    in_specs += [pl.BlockSpec((bk, bn), lambda j, i, kk: (kk, j)),
                 pl.BlockSpec((bm, bn), lambda j, i, kk: (i, j))]
    args += [b, res]
    out_specs = [pl.BlockSpec((bm, bn), lambda j, i, kk: (i, j))]
    out_shape = [jax.ShapeDtypeStruct((m, n), F32)]
    if has_aux:
        in_specs.append(pl.BlockSpec((ma, bn), lambda j, i, kk: (0, j)))
        args.append(auxres)
        out_specs.append(pl.BlockSpec((ma, bn), lambda j, i, kk: (0, j)))
        out_shape.append(jax.ShapeDtypeStruct((ma, n), F32))
    win = _nbytes((bm + ma, bk), BF16) + _nbytes((bk, bn), BF16) + 2 * _nbytes((bm + ma, bn), F32)
    return pl.pallas_call(
        functools.partial(_mm_res_kernel, has_aux=has_aux),
        grid=(n // bn, m // bm, k // bk),
        in_specs=in_specs,
        out_specs=out_specs,
        out_shape=out_shape,
        compiler_params=_params(("arbitrary", "arbitrary", "arbitrary"), win),
        name="matmul_res",
    )(*args)


def _t5_bucket(rel, n_buckets):
    half = n_buckets // 2
    exact = half // 2
    n = jnp.abs(rel)
    n_f = jnp.maximum(n, 1).astype(F32)
    large = exact + (jnp.log(n_f / exact) / math.log(MAX_DISTANCE / exact) * (half - exact)).astype(jnp.int32)
    large = jnp.minimum(large, half - 1)
    return jnp.where(rel > 0, half, 0) + jnp.where(n < exact, n, large)


def _bucket_tables(n_meta, n_buckets):
    qi = jnp.arange(BLOCK)[:, None]
    sj = jnp.arange(3 * BLOCK)[None, :]
    rel_band = sj - BLOCK - qi
    band = jnp.where(jnp.abs(rel_band) <= BLOCK, _t5_bucket(rel_band, n_buckets), -1)
    masked = jnp.full((BLOCK, BLOCK), -1, jnp.int32)
    sink = jnp.full((BLOCK, 1), n_buckets, jnp.int32)
    pad = jnp.full((BLOCK, BLOCK - n_meta - 1), -1, jnp.int32)
    mk = jnp.arange(n_meta)[None, :]
    meta_first = _t5_bucket(mk - (n_meta + qi), n_buckets)
    meta_far = jnp.full((BLOCK, n_meta), n_buckets // 2 - 1, jnp.int32)
    interior = jnp.concatenate([band, meta_far, sink, pad], axis=1)
    first = jnp.concatenate([masked, band[:, BLOCK:], meta_first, sink, pad], axis=1)
    last = jnp.concatenate([band[:, :2 * BLOCK], masked, meta_far, sink, pad], axis=1)
    tok = jnp.stack([interior, first, last]).astype(jnp.int32)

    mq = jnp.arange(n_meta)[:, None]
    tk = jnp.arange(BLOCK)[None, :]
    rel_tok = n_meta + tk - mq
    mband = jnp.where(jnp.abs(rel_tok) <= BLOCK, _t5_bucket(rel_tok, n_buckets), -1)
    mmeta = _t5_bucket(mk - mq, n_buckets)
    met = jnp.concatenate([mband, mmeta, sink[:n_meta], pad[:n_meta]], axis=1).astype(jnp.int32)
    return tok, met[None]


def _bias_kernel(tab_ref, bkt_ref, o_ref, *, n_ids):
    h = pl.program_id(0)
    bkt = bkt_ref[...]
    acc = jnp.full(bkt.shape, -jnp.inf, F32)
    for b in range(n_ids):
        acc = jnp.where(bkt == b, tab_ref[b, h], acc)
    o_ref[0, :, 0] = acc


def _bias_tiles(table, bkt):
    n_ids, n_heads = table.shape
    v, r, c = bkt.shape
    return pl.pallas_call(
        functools.partial(_bias_kernel, n_ids=n_ids),
        grid=(n_heads,),
        in_specs=[pl.BlockSpec(memory_space=pltpu.SMEM),
                  pl.BlockSpec((v, r, c), lambda h: (0, 0, 0))],
        out_specs=pl.BlockSpec((1, v, 1, r, c), lambda h: (h // GROUP, 0, h % GROUP, 0, 0)),
        out_shape=jax.ShapeDtypeStruct((n_heads // GROUP, v, GROUP, r, c), F32),
        compiler_params=_params(("parallel",), 2 * _nbytes(bkt.shape, F32)),
        name="bias_tiles",
    )(table, bkt)


def _softmax_pv(q, k, v, bias, scale):
    s = lax.dot_general(q, k, (((1,), (1,)), ((), ())), preferred_element_type=F32) * scale + bias
    m = jnp.max(s, axis=-1, keepdims=True)
    p = jnp.exp(s - m)
    denom = jnp.sum(p, axis=-1, keepdims=True)
    pn = (p * (1.0 / denom)).astype(BF16)
    return jnp.dot(pn, v, preferred_element_type=F32)


def _fill_meta_keys(km_scr, vm_scr, km_ref, vm_ref, kg, n_meta):
    km_scr[...] = jnp.zeros(km_scr.shape, BF16)
    vm_scr[...] = jnp.zeros(vm_scr.shape, BF16)
    km_scr[0:n_meta] = _rms(km_ref[...].astype(F32), kg).astype(BF16)
    vm_scr[0:n_meta] = vm_ref[...].astype(BF16)


def _attn_kernel(qg_ref, kg_ref, bias_ref, q_ref, kp_ref, kc_ref, kn_ref, vp_ref, vc_ref, vn_ref,
                 km_ref, vm_ref, o_ref, q_scr, k_scr, v_scr, km_scr, vm_scr, *, chunks, seq_chunks, n_meta, scale):
    r = pl.program_id(1)
    qg = qg_ref[...]
    kg = kg_ref[...]
    body = chunks * BLOCK
    stack = GROUP * BLOCK

    k_scr[0:BLOCK] = _rms(kp_ref[...].astype(F32), kg).astype(BF16)
    k_scr[BLOCK:BLOCK + body] = _rms(kc_ref[...].astype(F32), kg).astype(BF16)
    k_scr[BLOCK + body:2 * BLOCK + body] = _rms(kn_ref[...].astype(F32), kg).astype(BF16)
    v_scr[0:BLOCK] = vp_ref[...].astype(BF16)
    v_scr[BLOCK:BLOCK + body] = vc_ref[...].astype(BF16)
    v_scr[BLOCK + body:2 * BLOCK + body] = vn_ref[...].astype(BF16)
    _fill_meta_keys(km_scr, vm_scr, km_ref, vm_ref, kg, n_meta)
    for g in range(GROUP):
        qn = _rms(q_ref[:, g * HEAD_DIM:(g + 1) * HEAD_DIM].astype(F32), qg).astype(BF16)
        for c in range(chunks):
            q_scr[(c * GROUP + g) * BLOCK:(c * GROUP + g + 1) * BLOCK] = qn[c * BLOCK:(c + 1) * BLOCK]
    k_meta = km_scr[...]
    v_meta = vm_scr[...]

    def chunk(cc, carry):
        gch = r * chunks + cc
        is_first = functools.reduce(jnp.logical_or, [gch == s0 for s0, _ in seq_chunks])
        is_last = functools.reduce(jnp.logical_or, [gch == s0 + n - 1 for s0, n in seq_chunks])
        var = jnp.where(is_first, 1, jnp.where(is_last, 2, 0))
        row0 = pl.multiple_of(cc * BLOCK, BLOCK)
        k_all = jnp.concatenate([k_scr[pl.ds(row0, 3 * BLOCK), :], k_meta], axis=0)
        v_all = jnp.concatenate([v_scr[pl.ds(row0, 3 * BLOCK), :], v_meta], axis=0)
        q = q_scr[pl.ds(pl.multiple_of(cc * stack, stack), stack), :]
        bias = bias_ref[0, var].reshape(stack, 4 * BLOCK)
        o = _softmax_pv(q, k_all, v_all, bias, scale)
        for g in range(GROUP):
            o_ref[pl.ds(row0, BLOCK), g * HEAD_DIM:(g + 1) * HEAD_DIM] = o[g * BLOCK:(g + 1) * BLOCK]
        return carry

    lax.fori_loop(0, chunks, chunk, 0, unroll=True)


def _seq_of_chunk(c, seq_chunks):
    s = 0
    for s0, _ in seq_chunks[1:]:
        s = s + (c >= s0).astype(jnp.int32)
    return s


def _attention(z, zm, bias_tok, q_g, k_g, *, seq_chunks, n_meta, d_model, k_col, v_col):
    m = z.shape[0]
    n_chunks = m // BLOCK
    kv_heads = d_model // (GROUP * HEAD_DIM)
    chunks = math.gcd(8, *[n for _, n in seq_chunks])
    body = chunks * BLOCK
    qw = GROUP * HEAD_DIM
    kb, vb = k_col // HEAD_DIM, v_col // HEAD_DIM
    seq_of = functools.partial(_seq_of_chunk, seq_chunks=seq_chunks)

    def halo_prev(h, r):
        return jnp.maximum(r * chunks - 1, 0)

    def halo_next(h, r):
        return jnp.minimum(r * chunks + chunks, n_chunks - 1)

    in_specs = [
        pl.BlockSpec((1, HEAD_DIM), lambda h, r: (0, 0)),
        pl.BlockSpec((1, HEAD_DIM), lambda h, r: (0, 0)),
        pl.BlockSpec((1, 3, GROUP, BLOCK, 4 * BLOCK), lambda h, r: (h, 0, 0, 0, 0)),
        pl.BlockSpec((body, qw), lambda h, r: (r, h)),
        pl.BlockSpec((BLOCK, HEAD_DIM), lambda h, r: (halo_prev(h, r), kb + h)),
        pl.BlockSpec((body, HEAD_DIM), lambda h, r: (r, kb + h)),
        pl.BlockSpec((BLOCK, HEAD_DIM), lambda h, r: (halo_next(h, r), kb + h)),
        pl.BlockSpec((BLOCK, HEAD_DIM), lambda h, r: (halo_prev(h, r), vb + h)),
        pl.BlockSpec((body, HEAD_DIM), lambda h, r: (r, vb + h)),
        pl.BlockSpec((BLOCK, HEAD_DIM), lambda h, r: (halo_next(h, r), vb + h)),
        pl.BlockSpec((n_meta, HEAD_DIM), lambda h, r: (seq_of(r * chunks), kb + h)),
        pl.BlockSpec((n_meta, HEAD_DIM), lambda h, r: (seq_of(r * chunks), vb + h)),
    ]
    win = (_nbytes((3, GROUP, BLOCK, 4 * BLOCK), F32) + 2 * _nbytes((body, qw), F32)
           + 4 * _nbytes((body + 2 * BLOCK, HEAD_DIM), F32))
    return pl.pallas_call(
        functools.partial(_attn_kernel, chunks=chunks, seq_chunks=seq_chunks, n_meta=n_meta,
                          scale=HEAD_DIM ** -0.5),
        grid=(kv_heads, m // body),
        in_specs=in_specs,
        out_specs=pl.BlockSpec((body, qw), lambda h, r: (r, h)),
        out_shape=jax.ShapeDtypeStruct((m, d_model), F32),
        scratch_shapes=[pltpu.VMEM((body * GROUP, HEAD_DIM), BF16),
                        pltpu.VMEM((body + 2 * BLOCK, HEAD_DIM), BF16),
                        pltpu.VMEM((body + 2 * BLOCK, HEAD_DIM), BF16),
                        pltpu.VMEM((BLOCK, HEAD_DIM), BF16),
                        pltpu.VMEM((BLOCK, HEAD_DIM), BF16)],
        compiler_params=_params(("parallel", "arbitrary"), win),
        name="attention",
    )(q_g.reshape(1, HEAD_DIM), k_g.reshape(1, HEAD_DIM), bias_tok, z, z, z, z, z, z, z, zm, zm)


def _attn_meta_kernel(qg_ref, kg_ref, bias_ref, q_ref, k1_ref, v1_ref, km_ref, vm_ref, o_ref,
                      km_scr, vm_scr, *, n_meta, scale):
    qg = qg_ref[...]
    kg = kg_ref[...]
    _fill_meta_keys(km_scr, vm_scr, km_ref, vm_ref, kg, n_meta)
    k_all = jnp.concatenate([_rms(k1_ref[...].astype(F32), kg).astype(BF16), km_scr[...]], axis=0)
    v_all = jnp.concatenate([v1_ref[...].astype(BF16), vm_scr[...]], axis=0)
    q = q_ref[...].astype(F32)
    qn = jnp.concatenate([_rms(q[:, g * HEAD_DIM:(g + 1) * HEAD_DIM], qg).astype(BF16) for g in range(GROUP)], axis=0)
    o = _softmax_pv(qn, k_all, v_all, bias_ref[0, 0].reshape(GROUP * n_meta, 2 * BLOCK), scale)
    for g in range(GROUP):
        o_ref[:, g * HEAD_DIM:(g + 1) * HEAD_DIM] = o[g * n_meta:(g + 1) * n_meta]


def _attention_meta(z, zm, bias_met, q_g, k_g, *, seq_chunks, n_meta, d_model, k_col, v_col):
    n_seq = len(seq_chunks)
    kv_heads = d_model // (GROUP * HEAD_DIM)
    qw = GROUP * HEAD_DIM
    kb, vb = k_col // HEAD_DIM, v_col // HEAD_DIM

    def first_chunk(s):
        c = 0
        for i, (s0, _) in enumerate(seq_chunks):
            c = c + jnp.where(s == i, s0, 0)
        return c

    in_specs = [
        pl.BlockSpec((1, HEAD_DIM), lambda s, h: (0, 0)),
        pl.BlockSpec((1, HEAD_DIM), lambda s, h: (0, 0)),
        pl.BlockSpec((1, 1, GROUP, n_meta, 2 * BLOCK), lambda s, h: (h, 0, 0, 0, 0)),
        pl.BlockSpec((n_meta, qw), lambda s, h: (s, h)),
        pl.BlockSpec((BLOCK, HEAD_DIM), lambda s, h: (first_chunk(s), kb + h)),
        pl.BlockSpec((BLOCK, HEAD_DIM), lambda s, h: (first_chunk(s), vb + h)),
        pl.BlockSpec((n_meta, HEAD_DIM), lambda s, h: (s, kb + h)),
        pl.BlockSpec((n_meta, HEAD_DIM), lambda s, h: (s, vb + h)),
    ]
    return pl.pallas_call(
        functools.partial(_attn_meta_kernel, n_meta=n_meta, scale=HEAD_DIM ** -0.5),
        grid=(n_seq, kv_heads),
        in_specs=in_specs,
        out_specs=pl.BlockSpec((n_meta, qw), lambda s, h: (s, h)),
        out_shape=jax.ShapeDtypeStruct((n_seq * n_meta, d_model), F32),
        scratch_shapes=[pltpu.VMEM((BLOCK, HEAD_DIM), BF16), pltpu.VMEM((BLOCK, HEAD_DIM), BF16)],
        compiler_params=_params(("parallel", "parallel"), 1 << 20),
        name="attention_meta",
    )(q_g.reshape(1, HEAD_DIM), k_g.reshape(1, HEAD_DIM), bias_met, zm, z, z, zm, zm)


def _merge_math(attn, gb, gc, hc, ga, gcv, u_prev, u_next, cw, na, nc):
    u = gc * hc
    rows = u.shape[0]
    ridx = lax.broadcasted_iota(jnp.int32, u.shape, 0)
    up = jnp.where(ridx == 0, u_prev, pltpu.roll(u, 1, axis=0))
    un = jnp.where(ridx == rows - 1, u_next, pltpu.roll(u, rows - 1, axis=0))
    conv = gb * (cw[0:1] * up + cw[1:2] * u + cw[2:3] * un)
    return jax.nn.sigmoid(ga) * _rms(attn, na) + jax.nn.sigmoid(gcv) * _rms(conv, nc)


def _merge_kernel(attn_ref, gb_ref, gc_ref, hc_ref, ga_ref, gcv_ref, gcp_ref, hcp_ref, gcn_ref, hcn_ref,
                  gcm_ref, hcm_ref, cw_ref, na_ref, nc_ref, o_ref, *, seq_chunks):
    c = pl.program_id(0)
    is_first = functools.reduce(jnp.logical_or, [c == s0 for s0, _ in seq_chunks])
    is_last = functools.reduce(jnp.logical_or, [c == s0 + n - 1 for s0, n in seq_chunks])
    last = HALO_ROWS - 1
    f32 = lambda ref: ref[...].astype(F32)
    u_prev_tok = (f32(gcp_ref) * f32(hcp_ref))[last:last + 1]
    u_prev_meta = (f32(gcm_ref) * f32(hcm_ref))[last:last + 1]
    u_prev = jnp.where(is_first, u_prev_meta, u_prev_tok)
    u_next_tok = (f32(gcn_ref) * f32(hcn_ref))[0:1]
    u_next = jnp.where(is_last, jnp.zeros_like(u_next_tok), u_next_tok)
    o_ref[...] = _merge_math(attn_ref[...], f32(gb_ref), f32(gc_ref), f32(hc_ref), f32(ga_ref), f32(gcv_ref),
                             u_prev, u_next, cw_ref[...], na_ref[...], nc_ref[...]).astype(o_ref.dtype)


def _merge(attn, z, zm, conv_w, norm_a, norm_c, *, seq_chunks, n_meta, cols):
    m, d = attn.shape
    n_chunks = m // BLOCK
    per_chunk = BLOCK // HALO_ROWS
    per_meta = n_meta // HALO_ROWS
    gb, gc, hc, ga, gcv = [c // d for c in cols]
    seq_of = functools.partial(_seq_of_chunk, seq_chunks=seq_chunks)

    def main(col):
        return pl.BlockSpec((BLOCK, d), lambda c: (c, col))

    def prev_rows(col):
        return pl.BlockSpec((HALO_ROWS, d), lambda c: (jnp.maximum(c * per_chunk - 1, 0), col))

    def next_rows(col):
        return pl.BlockSpec((HALO_ROWS, d), lambda c: (jnp.minimum(c + 1, n_chunks - 1) * per_chunk, col))

    def meta_rows(col):
        return pl.BlockSpec((HALO_ROWS, d), lambda c: (seq_of(c) * per_meta + per_meta - 1, col))

    def row(nrows):
        return pl.BlockSpec((nrows, d), lambda c: (0, 0))

    win = 6 * _nbytes((BLOCK, d), F32) + _nbytes((BLOCK, d), BF16) + 8 * _nbytes((HALO_ROWS, d), F32)
    return pl.pallas_call(
        functools.partial(_merge_kernel, seq_chunks=seq_chunks),
        grid=(n_chunks,),
        in_specs=[main(0), main(gb), main(gc), main(hc), main(ga), main(gcv),
                  prev_rows(gc), prev_rows(hc), next_rows(gc), next_rows(hc), meta_rows(gc), meta_rows(hc),
                  row(3), row(1), row(1)],
        out_specs=pl.BlockSpec((BLOCK, d), lambda c: (c, 0)),
        out_shape=jax.ShapeDtypeStruct((m, d), BF16),
        compiler_params=_params(("parallel",), win),
        name="merge",
    )(attn, z, z, z, z, z, z, z, z, z, zm, zm, conv_w, norm_a.reshape(1, d), norm_c.reshape(1, d))


def _merge_meta_kernel(attn_ref, gb_ref, gc_ref, hc_ref, ga_ref, gcv_ref, gcn_ref, hcn_ref,
                       cw_ref, na_ref, nc_ref, o_ref):
    f32 = lambda ref: ref[...].astype(F32)
    u_next = (f32(gcn_ref) * f32(hcn_ref))[0:1]
    u_prev = jnp.zeros_like(u_next)
    o_ref[...] = _merge_math(attn_ref[...], f32(gb_ref), f32(gc_ref), f32(hc_ref), f32(ga_ref), f32(gcv_ref),
                             u_prev, u_next, cw_ref[...], na_ref[...], nc_ref[...]).astype(o_ref.dtype)


def _merge_meta(attn_m, z, zm, conv_w, norm_a, norm_c, *, seq_chunks, n_meta, cols):
    mm, d = attn_m.shape
    per_chunk = BLOCK // HALO_ROWS
    gb, gc, hc, ga, gcv = [c // d for c in cols]

    def first_rows(s):
        r = 0
        for i, (s0, _) in enumerate(seq_chunks):
            r = r + jnp.where(s == i, s0 * per_chunk, 0)
        return r

    def main(col):
        return pl.BlockSpec((n_meta, d), lambda s: (s, col))

    def next_rows(col):
        return pl.BlockSpec((HALO_ROWS, d), lambda s: (first_rows(s), col))

    def row(nrows):
        return pl.BlockSpec((nrows, d), lambda s: (0, 0))

    win = 7 * _nbytes((n_meta, d), F32) + 2 * _nbytes((HALO_ROWS, d), F32)
    return pl.pallas_call(
        _merge_meta_kernel,
        grid=(len(seq_chunks),),
        in_specs=[main(0), main(gb), main(gc), main(hc), main(ga), main(gcv), next_rows(gc), next_rows(hc),
                  row(3), row(1), row(1)],
        out_specs=pl.BlockSpec((n_meta, d), lambda s: (s, 0)),
        out_shape=jax.ShapeDtypeStruct((mm, d), BF16),
        compiler_params=_params(("parallel",), win),
        name="merge_meta",
    )(attn_m, zm, zm, zm, zm, zm, z, z, conv_w, norm_a.reshape(1, d), norm_c.reshape(1, d))


def kernel(x_prompt, x_sample, meta_tokens, rel_bias, norm1_g, w_in, q_norm_g, k_norm_g, attn_sink, conv_w,
           branch_norm_a, branch_norm_c, w_out, norm2_g, w_ffn_gate, w_ffn_up, w_ffn_down):
    d = x_prompt.shape[-1]
    n_meta = meta_tokens.shape[0]
    n_buckets = rel_bias.shape[0]
    depth, _, in_dim = w_in.shape
    ffn = w_ffn_gate.shape[-1]
    kv_dim = (in_dim - 6 * d) // 2
    assert d % (GROUP * HEAD_DIM) == 0 and kv_dim == d // GROUP and attn_sink.shape[1] * HEAD_DIM == d
    assert n_meta % HALO_ROWS == 0 and n_meta < BLOCK

    groups = (x_prompt, x_sample)
    seq_chunks = []
    for xg in groups:
        assert xg.shape[1] % BLOCK == 0 and xg.shape[1] >= 2 * BLOCK
        for _ in range(xg.shape[0]):
            start = seq_chunks[-1][0] + seq_chunks[-1][1] if seq_chunks else 0
            seq_chunks.append((start, xg.shape[1] // BLOCK))
    seq_chunks = tuple(seq_chunks)
    n_seq = len(seq_chunks)

    x = jnp.concatenate([xg.reshape(-1, d) for xg in groups], axis=0)
    xm = jnp.tile(meta_tokens.astype(x.dtype), (n_seq, 1))

    cols = tuple(d * i for i in range(1, 6))
    k_col, v_col = 6 * d, 6 * d + kv_dim
    ffn_tile = 512
    ffn_p = -(-ffn // ffn_tile) * ffn_tile
    out_tile = _divisor_tile(d, 512, V7X_LANES)
    n_q, n_rest = d // kv_dim, 5 * d // kv_dim

    def regroup(j):
        return jnp.where(j < n_q, j, jnp.where(j < n_q + n_rest, j + 2, j - n_rest))

    bkt_tok, bkt_met = _bucket_tables(n_meta, n_buckets)

    geo = dict(seq_chunks=seq_chunks, n_meta=n_meta)
    for l in range(depth):
        with_meta = l < depth - 1
        wd_b = _cast_weight(w_ffn_down, l, ffn_p, d, _divisor_tile(d, 1024, V7X_LANES))
        table = jnp.concatenate([rel_bias, attn_sink[l][None]], axis=0)
        z, zm = _wmm(_rmsnorm(x, norm1_g[l]), _rmsnorm(xm, norm1_g[l]), [w_in], l, in_dim, kv_dim, BF16,
                     src_block=regroup)
        attn = _attention(z, zm, _bias_tiles(table, bkt_tok), q_norm_g[l], k_norm_g[l],
                          d_model=d, k_col=k_col, v_col=v_col, **geo)
        merged = _merge(attn, z, zm, conv_w[l], branch_norm_a[l], branch_norm_c[l], cols=cols, **geo)
        merged_m = hn_m = None
        if with_meta:
            attn_m = _attention_meta(z, zm, _bias_tiles(table, bkt_met), q_norm_g[l], k_norm_g[l],
                                     d_model=d, k_col=k_col, v_col=v_col, **geo)
            merged_m = _merge_meta(attn_m, z, zm, conv_w[l], branch_norm_a[l], branch_norm_c[l], cols=cols, **geo)
        h, *hm = _wmm(merged, merged_m, [w_out], l, d, out_tile, F32, res=x, auxres=xm if with_meta else None)
        if with_meta:
            hn_m = _rmsnorm(hm[0], norm2_g[l])
        t, *tm = _wmm(_rmsnorm(h, norm2_g[l]), hn_m, [w_ffn_gate, w_ffn_up], l, ffn_p, ffn_tile, BF16, act=True)
        x, *xm_next = _matmul_res(t, wd_b, h, aux=tm[0] if with_meta else None, auxres=hm[0] if with_meta else None)
        if with_meta:
            xm = xm_next[0]

    outs = []
    row = 0
    for xg in groups:
        rows = xg.shape[0] * xg.shape[1]
        outs.append(x[row:row + rows].reshape(xg.shape))
        row += rows
    return tuple(outs)
```

```python
import functools
import math

import jax
import jax.numpy as jnp
from jax import lax
from jax.experimental import pallas as pl
from jax.experimental.pallas import tpu as pltpu

HEAD_DIM = 128
GROUP = 4
BLOCK = 128
MAX_DISTANCE = 128
EPS = 1e-6

V7X_VMEM_BYTES = 64 << 20
V7X_LANES = 128
BF16_SUBLANES = 16
F32_SUBLANES = 8
HALO_ROWS = BF16_SUBLANES
VMEM_SLACK_BYTES = 10 << 20

F32 = jnp.float32
BF16 = jnp.bfloat16


def _params(dims, window_bytes):
    limit = min(2 * window_bytes + VMEM_SLACK_BYTES, V7X_VMEM_BYTES - (6 << 20))
    return pltpu.CompilerParams(dimension_semantics=dims, vmem_limit_bytes=int(limit))


def _divisor_tile(n, cap, unit):
    if n <= cap:
        return n
    t = (cap // unit) * unit
    while t >= unit:
        if n % t == 0:
            return t
        t -= unit
    raise ValueError(f"no tile for {n}")


def _nbytes(shape, dtype):
    return math.prod(shape) * jnp.dtype(dtype).itemsize


def _rms(x, g):
    ms = jnp.mean(x * x, axis=-1, keepdims=True)
    return x * lax.rsqrt(ms + EPS) * g


def _rmsnorm_kernel(x_ref, g_ref, o_ref):
    o_ref[...] = _rms(x_ref[...], g_ref[...]).astype(o_ref.dtype)


def _rmsnorm(x, g):
    m, d = x.shape
    bm = _divisor_tile(m, 256, BF16_SUBLANES)
    return pl.pallas_call(
        _rmsnorm_kernel,
        grid=(m // bm,),
        in_specs=[pl.BlockSpec((bm, d), lambda i: (i, 0)),
                  pl.BlockSpec((1, d), lambda i: (0, 0))],
        out_specs=pl.BlockSpec((bm, d), lambda i: (i, 0)),
        out_shape=jax.ShapeDtypeStruct((m, d), BF16),
        compiler_params=_params(("parallel",), _nbytes((bm, d), F32) + _nbytes((bm, d), BF16)),
        name="rmsnorm",
    )(x, g.reshape(1, d))


def _cast_kernel(w_ref, o_ref, *, rows, cols, ragged):
    w = w_ref[...]
    if ragged:
        br, bc = w.shape
        r = pl.program_id(0) * br + lax.broadcasted_iota(jnp.int32, w.shape, 0)
        c = pl.program_id(1) * bc + lax.broadcasted_iota(jnp.int32, w.shape, 1)
        w = jnp.where(jnp.logical_and(r < rows, c < cols), w, 0.0)
    o_ref[...] = w.astype(o_ref.dtype)


def _cast_weight(w, layer, out_rows, out_cols, bc, src_block=None):
    _, rows, cols = w.shape
    br = _divisor_tile(out_rows, 1024, BF16_SUBLANES)
    ragged = rows % br != 0 or cols % bc != 0
    assert src_block is None or not ragged
    src = src_block if src_block is not None else (lambda j: j)
    return pl.pallas_call(
        functools.partial(_cast_kernel, rows=rows, cols=cols, ragged=ragged),
        grid=(out_rows // br, out_cols // bc),
        in_specs=[pl.BlockSpec((None, br, bc), lambda i, j: (layer, i, src(j)))],
        out_specs=pl.BlockSpec((br, bc), lambda i, j: (i, j)),
        out_shape=jax.ShapeDtypeStruct((out_rows, out_cols), BF16),
        compiler_params=_params(("parallel", "parallel"), _nbytes((br, bc), F32) + _nbytes((br, bc), BF16)),
        name="cast_weight",
    )(w)


def _wmm_kernel(*refs, n_w, has_res, has_aux, act, n_tiles, rows, src_cols):
    it = iter(refs)
    a_ref = next(it)
    aux_ref = next(it) if has_aux else None
    w_refs = [next(it) for _ in range(n_w)]
    res_ref = next(it) if has_res else None
    auxres_ref = next(it) if has_res and has_aux else None
    o_ref = next(it)
    oaux_ref = next(it) if has_aux else None
    w_bufs = (next(it), next(it))
    g = pl.program_id(0)
    i = pl.program_id(1)
    bn = w_bufs[0].shape[-1]

    def stage(buf):
        row0 = pl.multiple_of(i * rows, rows)
        for t in range(n_w):
            w = w_refs[t][...]
            if src_cols % bn:
                col = jnp.minimum(g, n_tiles - 1) * bn + lax.broadcasted_iota(jnp.int32, w.shape, 1)
                w = jnp.where(col < src_cols, w, 0.0)
            buf[t, pl.ds(row0, rows), :] = w.astype(buf.dtype)

    def apply(buf, a, res):
        ys = [jnp.dot(a, buf[t], preferred_element_type=F32) for t in range(n_w)]
        y = ys[0] * jax.nn.sigmoid(ys[0]) * ys[1] if act else ys[0]
        return y if res is None else res + y

    def compute(buf):
        o_ref[...] = apply(buf, a_ref[...], res_ref[...] if has_res else None).astype(o_ref.dtype)
        if has_aux:
            @pl.when(i == 0)
            def _():
                oaux_ref[...] = apply(buf, aux_ref[...], auxres_ref[...] if has_res else None).astype(oaux_ref.dtype)

    @pl.when(g == 0)
    def _():
        stage(w_bufs[0])

    for parity in range(2):
        @pl.when(jnp.logical_and(g > 0, g % 2 == parity))
        def _():
            stage(w_bufs[parity])
            compute(w_bufs[1 - parity])


def _wmm(a, aux, ws, layer, n_out, bn, out_dtype, *, src_block=None, act=False, res=None, auxres=None):
    m, k = a.shape
    src_cols = ws[0].shape[2]
    bm = _divisor_tile(m, 1024, BF16_SUBLANES)
    ni, n_tiles = m // bm, n_out // bn
    rows = k // ni
    assert k % ni == 0 and rows % BF16_SUBLANES == 0 and n_out % bn == 0
    assert src_block is None or src_cols % bn == 0
    src = src_block if src_block is not None else (lambda j: j)
    has_aux, has_res = aux is not None, res is not None
    ma = aux.shape[0] if has_aux else 0

    def row_blk(g, i):
        return jnp.where(g == 0, 0, i)

    def col_blk(g):
        return jnp.maximum(g - 1, 0)

    in_specs = [pl.BlockSpec((bm, k), lambda g, i: (row_blk(g, i), 0))]
    args = [a]
    if has_aux:
        in_specs.append(pl.BlockSpec((ma, k), lambda g, i: (0, 0)))
        args.append(aux)
    for w in ws:
        in_specs.append(pl.BlockSpec((None, rows, bn), lambda g, i: (layer, i, src(jnp.minimum(g, n_tiles - 1)))))
        args.append(w)
    if has_res:
        in_specs.append(pl.BlockSpec((bm, bn), lambda g, i: (row_blk(g, i), col_blk(g))))
        args.append(res)
        if has_aux:
            in_specs.append(pl.BlockSpec((ma, bn), lambda g, i: (0, col_blk(g))))
            args.append(auxres)
    out_specs = [pl.BlockSpec((bm, bn), lambda g, i: (row_blk(g, i), col_blk(g)))]
    out_shape = [jax.ShapeDtypeStruct((m, n_out), out_dtype)]
    if has_aux:
        out_specs.append(pl.BlockSpec((ma, bn), lambda g, i: (0, col_blk(g))))
        out_shape.append(jax.ShapeDtypeStruct((ma, n_out), out_dtype))
    win = (_nbytes((bm + ma, k), BF16) + len(ws) * _nbytes((rows, bn), F32)
           + _nbytes((bm + ma, bn), out_dtype) + has_res * _nbytes((bm + ma, bn), F32))
    scratch = _nbytes((2, len(ws), k, bn), BF16)
    return pl.pallas_call(
        functools.partial(_wmm_kernel, n_w=len(ws), has_res=has_res, has_aux=has_aux, act=act,
                          n_tiles=n_tiles, rows=rows, src_cols=src_cols),
        grid=(n_tiles + 1, ni),
        in_specs=in_specs,
        out_specs=out_specs,
        out_shape=out_shape,
        scratch_shapes=[pltpu.VMEM((len(ws), k, bn), BF16), pltpu.VMEM((len(ws), k, bn), BF16)],
        compiler_params=_params(("arbitrary", "arbitrary"), win + scratch // 2),
        name="wmm",
    )(*args)


def _mm_res_kernel(a_ref, b_ref, r_ref, o_ref):
    @pl.when(pl.program_id(2) == 0)
    def _():
        o_ref[...] = r_ref[...]

    o_ref[...] += jnp.dot(a_ref[...], b_ref[...], preferred_element_type=F32)


def _matmul_res(a, b, res):
    m, k = a.shape
    n = b.shape[1]
    bm = _divisor_tile(m, 1024, BF16_SUBLANES)
    bn = _divisor_tile(n, 1024, V7X_LANES)
    bk = _divisor_tile(k, 4096, V7X_LANES)
    win = _nbytes((bm, bk), BF16) + _nbytes((bk, bn), BF16) + 2 * _nbytes((bm, bn), F32)
    return pl.pallas_call(
        _mm_res_kernel,
        grid=(m // bm, n // bn, k // bk),
        in_specs=[pl.BlockSpec((bm, bk), lambda i, j, kk: (i, kk)),
                  pl.BlockSpec((bk, bn), lambda i, j, kk: (kk, j)),
                  pl.BlockSpec((bm, bn), lambda i, j, kk: (i, j))],
        out_specs=pl.BlockSpec((bm, bn), lambda i, j, kk: (i, j)),
        out_shape=jax.ShapeDtypeStruct((m, n), F32),
        compiler_params=_params(("parallel", "parallel", "arbitrary"), win),
        name="matmul_res",
    )(a, b, res)


def _t5_bucket(rel, n_buckets):
    half = n_buckets // 2
    exact = half // 2
    n = jnp.abs(rel)
    n_f = jnp.maximum(n, 1).astype(F32)
    large = exact + (jnp.log(n_f / exact) / math.log(MAX_DISTANCE / exact) * (half - exact)).astype(jnp.int32)
    large = jnp.minimum(large, half - 1)
    return jnp.where(rel > 0, half, 0) + jnp.where(n < exact, n, large)


def _bucket_tables(n_meta, n_buckets):
    qi = jnp.arange(BLOCK)[:, None]
    sj = jnp.arange(3 * BLOCK)[None, :]
    rel_band = sj - BLOCK - qi
    band = jnp.where(jnp.abs(rel_band) <= BLOCK, _t5_bucket(rel_band, n_buckets), -1)
    masked = jnp.full((BLOCK, BLOCK), -1, jnp.int32)
    sink = jnp.full((BLOCK, 1), n_buckets, jnp.int32)
    pad = jnp.full((BLOCK, BLOCK - n_meta - 1), -1, jnp.int32)
    mk = jnp.arange(n_meta)[None, :]
    meta_first = _t5_bucket(mk - (n_meta + qi), n_buckets)
    meta_far = jnp.full((BLOCK, n_meta), n_buckets // 2 - 1, jnp.int32)
    interior = jnp.concatenate([band, meta_far, sink, pad], axis=1)
    first = jnp.concatenate([masked, band[:, BLOCK:], meta_first, sink, pad], axis=1)
    last = jnp.concatenate([band[:, :2 * BLOCK], masked, meta_far, sink, pad], axis=1)
    tok = jnp.stack([interior, first, last]).astype(jnp.int32)

    mq = jnp.arange(n_meta)[:, None]
    tk = jnp.arange(BLOCK)[None, :]
    rel_tok = n_meta + tk - mq
    mband = jnp.where(jnp.abs(rel_tok) <= BLOCK, _t5_bucket(rel_tok, n_buckets), -1)
    mmeta = _t5_bucket(mk - mq, n_buckets)
    met = jnp.concatenate([mband, mmeta, sink[:n_meta], pad[:n_meta]], axis=1).astype(jnp.int32)
    return tok, met[None]


def _bias_kernel(tab_ref, bkt_ref, o_ref, *, n_ids):
    h = pl.program_id(0)
    bkt = bkt_ref[...]
    acc = jnp.full(bkt.shape, -jnp.inf, F32)
    for b in range(n_ids):
        acc = jnp.where(bkt == b, tab_ref[b, h], acc)
    o_ref[0, :, 0] = acc


def _bias_tiles(table, bkt):
    n_ids, n_heads = table.shape
    v, r, c = bkt.shape
    return pl.pallas_call(
        functools.partial(_bias_kernel, n_ids=n_ids),
        grid=(n_heads,),
        in_specs=[pl.BlockSpec(memory_space=pltpu.SMEM),
                  pl.BlockSpec((v, r, c), lambda h: (0, 0, 0))],
        out_specs=pl.BlockSpec((1, v, 1, r, c), lambda h: (h // GROUP, 0, h % GROUP, 0, 0)),
        out_shape=jax.ShapeDtypeStruct((n_heads // GROUP, v, GROUP, r, c), F32),
        compiler_params=_params(("parallel",), 2 * _nbytes(bkt.shape, F32)),
        name="bias_tiles",
    )(table, bkt)


def _softmax_pv(q, k, v, bias, scale):
    s = lax.dot_general(q, k, (((1,), (1,)), ((), ())), preferred_element_type=F32) * scale + bias
    m = jnp.max(s, axis=-1, keepdims=True)
    p = jnp.exp(s - m)
    denom = jnp.sum(p, axis=-1, keepdims=True)
    pn = (p * (1.0 / denom)).astype(BF16)
    return jnp.dot(pn, v, preferred_element_type=F32)


def _fill_meta_keys(km_scr, vm_scr, km_ref, vm_ref, kg, n_meta):
    km_scr[...] = jnp.zeros(km_scr.shape, BF16)
    vm_scr[...] = jnp.zeros(vm_scr.shape, BF16)
    km_scr[0:n_meta] = _rms(km_ref[...].astype(F32), kg).astype(BF16)
    vm_scr[0:n_meta] = vm_ref[...].astype(BF16)


def _attn_kernel(qg_ref, kg_ref, bias_ref, q_ref, kp_ref, kc_ref, kn_ref, vp_ref, vc_ref, vn_ref,
                 km_ref, vm_ref, o_ref, q_scr, k_scr, v_scr, km_scr, vm_scr, *, chunks, seq_chunks, n_meta, scale):
    r = pl.program_id(1)
    qg = qg_ref[...]
    kg = kg_ref[...]
    body = chunks * BLOCK
    stack = GROUP * BLOCK

    k_scr[0:BLOCK] = _rms(kp_ref[...].astype(F32), kg).astype(BF16)
    k_scr[BLOCK:BLOCK + body] = _rms(kc_ref[...].astype(F32), kg).astype(BF16)
    k_scr[BLOCK + body:2 * BLOCK + body] = _rms(kn_ref[...].astype(F32), kg).astype(BF16)
    v_scr[0:BLOCK] = vp_ref[...].astype(BF16)
    v_scr[BLOCK:BLOCK + body] = vc_ref[...].astype(BF16)
    v_scr[BLOCK + body:2 * BLOCK + body] = vn_ref[...].astype(BF16)
    _fill_meta_keys(km_scr, vm_scr, km_ref, vm_ref, kg, n_meta)
    for g in range(GROUP):
        qn = _rms(q_ref[:, g * HEAD_DIM:(g + 1) * HEAD_DIM].astype(F32), qg).astype(BF16)
        for c in range(chunks):
            q_scr[(c * GROUP + g) * BLOCK:(c * GROUP + g + 1) * BLOCK] = qn[c * BLOCK:(c + 1) * BLOCK]
    k_meta = km_scr[...]
    v_meta = vm_scr[...]

    def chunk(cc, carry):
        gch = r * chunks + cc
        is_first = functools.reduce(jnp.logical_or, [gch == s0 for s0, _ in seq_chunks])
        is_last = functools.reduce(jnp.logical_or, [gch == s0 + n - 1 for s0, n in seq_chunks])
        var = jnp.where(is_first, 1, jnp.where(is_last, 2, 0))
        row0 = pl.multiple_of(cc * BLOCK, BLOCK)
        k_all = jnp.concatenate([k_scr[pl.ds(row0, 3 * BLOCK), :], k_meta], axis=0)
        v_all = jnp.concatenate([v_scr[pl.ds(row0, 3 * BLOCK), :], v_meta], axis=0)
        q = q_scr[pl.ds(pl.multiple_of(cc * stack, stack), stack), :]
        bias = bias_ref[0, var].reshape(stack, 4 * BLOCK)
        o = _softmax_pv(q, k_all, v_all, bias, scale)
        for g in range(GROUP):
            o_ref[pl.ds(row0, BLOCK), g * HEAD_DIM:(g + 1) * HEAD_DIM] = o[g * BLOCK:(g + 1) * BLOCK]
        return carry

    lax.fori_loop(0, chunks, chunk, 0, unroll=True)


def _seq_of_chunk(c, seq_chunks):
    s = 0
    for s0, _ in seq_chunks[1:]:
        s = s + (c >= s0).astype(jnp.int32)
    return s


def _attention(z, zm, bias_tok, q_g, k_g, *, seq_chunks, n_meta, d_model, k_col, v_col):
    m = z.shape[0]
    n_chunks = m // BLOCK
    kv_heads = d_model // (GROUP * HEAD_DIM)
    chunks = math.gcd(8, *[n for _, n in seq_chunks])
    body = chunks * BLOCK
    qw = GROUP * HEAD_DIM
    kb, vb = k_col // HEAD_DIM, v_col // HEAD_DIM
    seq_of = functools.partial(_seq_of_chunk, seq_chunks=seq_chunks)

    def halo_prev(h, r):
        return jnp.maximum(r * chunks - 1, 0)

    def halo_next(h, r):
        return jnp.minimum(r * chunks + chunks, n_chunks - 1)

    in_specs = [
        pl.BlockSpec((1, HEAD_DIM), lambda h, r: (0, 0)),
        pl.BlockSpec((1, HEAD_DIM), lambda h, r: (0, 0)),
        pl.BlockSpec((1, 3, GROUP, BLOCK, 4 * BLOCK), lambda h, r: (h, 0, 0, 0, 0)),
        pl.BlockSpec((body, qw), lambda h, r: (r, h)),
        pl.BlockSpec((BLOCK, HEAD_DIM), lambda h, r: (halo_prev(h, r), kb + h)),
        pl.BlockSpec((body, HEAD_DIM), lambda h, r: (r, kb + h)),
        pl.BlockSpec((BLOCK, HEAD_DIM), lambda h, r: (halo_next(h, r), kb + h)),
        pl.BlockSpec((BLOCK, HEAD_DIM), lambda h, r: (halo_prev(h, r), vb + h)),
        pl.BlockSpec((body, HEAD_DIM), lambda h, r: (r, vb + h)),
        pl.BlockSpec((BLOCK, HEAD_DIM), lambda h, r: (halo_next(h, r), vb + h)),
        pl.BlockSpec((n_meta, HEAD_DIM), lambda h, r: (seq_of(r * chunks), kb + h)),
        pl.BlockSpec((n_meta, HEAD_DIM), lambda h, r: (seq_of(r * chunks), vb + h)),
    ]
    win = (_nbytes((3, GROUP, BLOCK, 4 * BLOCK), F32) + 2 * _nbytes((body, qw), F32)
           + 4 * _nbytes((body + 2 * BLOCK, HEAD_DIM), F32))
    return pl.pallas_call(
        functools.partial(_attn_kernel, chunks=chunks, seq_chunks=seq_chunks, n_meta=n_meta,
                          scale=HEAD_DIM ** -0.5),
        grid=(kv_heads, m // body),
        in_specs=in_specs,
        out_specs=pl.BlockSpec((body, qw), lambda h, r: (r, h)),
        out_shape=jax.ShapeDtypeStruct((m, d_model), F32),
        scratch_shapes=[pltpu.VMEM((body * GROUP, HEAD_DIM), BF16),
                        pltpu.VMEM((body + 2 * BLOCK, HEAD_DIM), BF16),
                        pltpu.VMEM((body + 2 * BLOCK, HEAD_DIM), BF16),
                        pltpu.VMEM((BLOCK, HEAD_DIM), BF16),
                        pltpu.VMEM((BLOCK, HEAD_DIM), BF16)],
        compiler_params=_params(("parallel", "arbitrary"), win),
        name="attention",
    )(q_g.reshape(1, HEAD_DIM), k_g.reshape(1, HEAD_DIM), bias_tok, z, z, z, z, z, z, z, zm, zm)


def _attn_meta_kernel(qg_ref, kg_ref, bias_ref, q_ref, k1_ref, v1_ref, km_ref, vm_ref, o_ref,
                      km_scr, vm_scr, *, n_meta, scale):
    qg = qg_ref[...]
    kg = kg_ref[...]
    _fill_meta_keys(km_scr, vm_scr, km_ref, vm_ref, kg, n_meta)
    k_all = jnp.concatenate([_rms(k1_ref[...].astype(F32), kg).astype(BF16), km_scr[...]], axis=0)
    v_all = jnp.concatenate([v1_ref[...].astype(BF16), vm_scr[...]], axis=0)
    q = q_ref[...].astype(F32)
    qn = jnp.concatenate([_rms(q[:, g * HEAD_DIM:(g + 1) * HEAD_DIM], qg).astype(BF16) for g in range(GROUP)], axis=0)
    o = _softmax_pv(qn, k_all, v_all, bias_ref[0, 0].reshape(GROUP * n_meta, 2 * BLOCK), scale)
    for g in range(GROUP):
        o_ref[:, g * HEAD_DIM:(g + 1) * HEAD_DIM] = o[g * n_meta:(g + 1) * n_meta]


def _attention_meta(z, zm, bias_met, q_g, k_g, *, seq_chunks, n_meta, d_model, k_col, v_col):
    n_seq = len(seq_chunks)
    kv_heads = d_model // (GROUP * HEAD_DIM)
    qw = GROUP * HEAD_DIM
    kb, vb = k_col // HEAD_DIM, v_col // HEAD_DIM

    def first_chunk(s):
        c = 0
        for i, (s0, _) in enumerate(seq_chunks):
            c = c + jnp.where(s == i, s0, 0)
        return c

    in_specs = [
        pl.BlockSpec((1, HEAD_DIM), lambda s, h: (0, 0)),
        pl.BlockSpec((1, HEAD_DIM), lambda s, h: (0, 0)),
        pl.BlockSpec((1, 1, GROUP, n_meta, 2 * BLOCK), lambda s, h: (h, 0, 0, 0, 0)),
        pl.BlockSpec((n_meta, qw), lambda s, h: (s, h)),
        pl.BlockSpec((BLOCK, HEAD_DIM), lambda s, h: (first_chunk(s), kb + h)),
        pl.BlockSpec((BLOCK, HEAD_DIM), lambda s, h: (first_chunk(s), vb + h)),
        pl.BlockSpec((n_meta, HEAD_DIM), lambda s, h: (s, kb + h)),
        pl.BlockSpec((n_meta, HEAD_DIM), lambda s, h: (s, vb + h)),
    ]
    return pl.pallas_call(
        functools.partial(_attn_meta_kernel, n_meta=n_meta, scale=HEAD_DIM ** -0.5),
        grid=(n_seq, kv_heads),
        in_specs=in_specs,
        out_specs=pl.BlockSpec((n_meta, qw), lambda s, h: (s, h)),
        out_shape=jax.ShapeDtypeStruct((n_seq * n_meta, d_model), F32),
        scratch_shapes=[pltpu.VMEM((BLOCK, HEAD_DIM), BF16), pltpu.VMEM((BLOCK, HEAD_DIM), BF16)],
        compiler_params=_params(("parallel", "parallel"), 1 << 20),
        name="attention_meta",
    )(q_g.reshape(1, HEAD_DIM), k_g.reshape(1, HEAD_DIM), bias_met, zm, z, z, zm, zm)


def _merge_math(attn, gb, gc, hc, ga, gcv, u_prev, u_next, cw, na, nc):
    u = gc * hc
    rows = u.shape[0]
    ridx = lax.broadcasted_iota(jnp.int32, u.shape, 0)
    up = jnp.where(ridx == 0, u_prev, pltpu.roll(u, 1, axis=0))
    un = jnp.where(ridx == rows - 1, u_next, pltpu.roll(u, rows - 1, axis=0))
    conv = gb * (cw[0:1] * up + cw[1:2] * u + cw[2:3] * un)
    return jax.nn.sigmoid(ga) * _rms(attn, na) + jax.nn.sigmoid(gcv) * _rms(conv, nc)


def _merge_kernel(attn_ref, gb_ref, gc_ref, hc_ref, ga_ref, gcv_ref, gcp_ref, hcp_ref, gcn_ref, hcn_ref,
                  gcm_ref, hcm_ref, cw_ref, na_ref, nc_ref, o_ref, conv_scr, *, seq_chunks):
    c = pl.program_id(0)
    is_first = functools.reduce(jnp.logical_or, [c == s0 for s0, _ in seq_chunks])
    is_last = functools.reduce(jnp.logical_or, [c == s0 + n - 1 for s0, n in seq_chunks])
    last = HALO_ROWS - 1
    rows, d = o_ref.shape
    n_tiles = d // V7X_LANES
    ridx = lax.broadcasted_iota(jnp.int32, (rows, V7X_LANES), 0)

    def cols(ct):
        return pl.ds(pl.multiple_of(ct * V7X_LANES, V7X_LANES), V7X_LANES)

    def f32(ref, sl):
        return ref[:, sl].astype(F32)

    def conv_pass(ct, carry):
        ss_attn, ss_conv = carry
        sl = cols(ct)
        u = f32(gc_ref, sl) * f32(hc_ref, sl)
        u_prev_tok = (f32(gcp_ref, sl) * f32(hcp_ref, sl))[last:last + 1]
        u_prev_meta = (f32(gcm_ref, sl) * f32(hcm_ref, sl))[last:last + 1]
        u_prev = jnp.where(is_first, u_prev_meta, u_prev_tok)
        u_next_tok = (f32(gcn_ref, sl) * f32(hcn_ref, sl))[0:1]
        u_next = jnp.where(is_last, jnp.zeros_like(u_next_tok), u_next_tok)
        up = jnp.where(ridx == 0, u_prev, pltpu.roll(u, 1, axis=0))
        un = jnp.where(ridx == rows - 1, u_next, pltpu.roll(u, rows - 1, axis=0))
        cw = cw_ref[:, sl]
        conv = f32(gb_ref, sl) * (cw[0:1] * up + cw[1:2] * u + cw[2:3] * un)
        conv_scr[:, sl] = conv
        attn = attn_ref[:, sl]
        return ss_attn + attn * attn, ss_conv + conv * conv

    zero = jnp.zeros((rows, V7X_LANES), F32)
    ss_attn, ss_conv = lax.fori_loop(0, n_tiles, conv_pass, (zero, zero))
    r_attn = lax.rsqrt(jnp.sum(ss_attn, axis=-1, keepdims=True) / d + EPS)
    r_conv = lax.rsqrt(jnp.sum(ss_conv, axis=-1, keepdims=True) / d + EPS)

    def gate_pass(ct, carry):
        sl = cols(ct)
        a = attn_ref[:, sl] * r_attn * na_ref[:, sl]
        c = conv_scr[:, sl] * r_conv * nc_ref[:, sl]
        o_ref[:, sl] = (jax.nn.sigmoid(f32(ga_ref, sl)) * a + jax.nn.sigmoid(f32(gcv_ref, sl)) * c).astype(o_ref.dtype)
        return carry

    lax.fori_loop(0, n_tiles, gate_pass, 0)


def _merge(attn, z, zm, conv_w, norm_a, norm_c, *, seq_chunks, n_meta, cols):
    m, d = attn.shape
    n_chunks = m // BLOCK
    per_chunk = BLOCK // HALO_ROWS
    per_meta = n_meta // HALO_ROWS
    gb, gc, hc, ga, gcv = [c // d for c in cols]
    seq_of = functools.partial(_seq_of_chunk, seq_chunks=seq_chunks)

    def main(col):
        return pl.BlockSpec((BLOCK, d), lambda c: (c, col))

    def prev_rows(col):
        return pl.BlockSpec((HALO_ROWS, d), lambda c: (jnp.maximum(c * per_chunk - 1, 0), col))

    def next_rows(col):
        return pl.BlockSpec((HALO_ROWS, d), lambda c: (jnp.minimum(c + 1, n_chunks - 1) * per_chunk, col))

    def meta_rows(col):
        return pl.BlockSpec((HALO_ROWS, d), lambda c: (seq_of(c) * per_meta + per_meta - 1, col))

    def row(nrows):
        return pl.BlockSpec((nrows, d), lambda c: (0, 0))

    win = 6 * _nbytes((BLOCK, d), F32) + _nbytes((BLOCK, d), BF16) + 8 * _nbytes((HALO_ROWS, d), F32)
    return pl.pallas_call(
        functools.partial(_merge_kernel, seq_chunks=seq_chunks),
        grid=(n_chunks,),
        in_specs=[main(0), main(gb), main(gc), main(hc), main(ga), main(gcv),
                  prev_rows(gc), prev_rows(hc), next_rows(gc), next_rows(hc), meta_rows(gc), meta_rows(hc),
                  row(3), row(1), row(1)],
        out_specs=pl.BlockSpec((BLOCK, d), lambda c: (c, 0)),
        out_shape=jax.ShapeDtypeStruct((m, d), BF16),
        scratch_shapes=[pltpu.VMEM((BLOCK, d), F32)],
        compiler_params=_params(("parallel",), win),
        name="merge",
    )(attn, z, z, z, z, z, z, z, z, z, zm, zm, conv_w, norm_a.reshape(1, d), norm_c.reshape(1, d))


def _merge_meta_kernel(attn_ref, gb_ref, gc_ref, hc_ref, ga_ref, gcv_ref, gcn_ref, hcn_ref,
                       cw_ref, na_ref, nc_ref, o_ref):
    f32 = lambda ref: ref[...].astype(F32)
    u_next = (f32(gcn_ref) * f32(hcn_ref))[0:1]
    u_prev = jnp.zeros_like(u_next)
    o_ref[...] = _merge_math(attn_ref[...], f32(gb_ref), f32(gc_ref), f32(hc_ref), f32(ga_ref), f32(gcv_ref),
                             u_prev, u_next, cw_ref[...], na_ref[...], nc_ref[...]).astype(o_ref.dtype)


def _merge_meta(attn_m, z, zm, conv_w, norm_a, norm_c, *, seq_chunks, n_meta, cols):
    mm, d = attn_m.shape
    per_chunk = BLOCK // HALO_ROWS
    gb, gc, hc, ga, gcv = [c // d for c in cols]

    def first_rows(s):
        r = 0
        for i, (s0, _) in enumerate(seq_chunks):
            r = r + jnp.where(s == i, s0 * per_chunk, 0)
        return r

    def main(col):
        return pl.BlockSpec((n_meta, d), lambda s: (s, col))

    def next_rows(col):
        return pl.BlockSpec((HALO_ROWS, d), lambda s: (first_rows(s), col))

    def row(nrows):
        return pl.BlockSpec((nrows, d), lambda s: (0, 0))

    win = 7 * _nbytes((n_meta, d), F32) + 2 * _nbytes((HALO_ROWS, d), F32)
    return pl.pallas_call(
        _merge_meta_kernel,
        grid=(len(seq_chunks),),
        in_specs=[main(0), main(gb), main(gc), main(hc), main(ga), main(gcv), next_rows(gc), next_rows(hc),
                  row(3), row(1), row(1)],
        out_specs=pl.BlockSpec((n_meta, d), lambda s: (s, 0)),
        out_shape=jax.ShapeDtypeStruct((mm, d), BF16),
        compiler_params=_params(("parallel",), win),
        name="merge_meta",
    )(attn_m, zm, zm, zm, zm, zm, z, z, conv_w, norm_a.reshape(1, d), norm_c.reshape(1, d))


def kernel(x_prompt, x_sample, meta_tokens, rel_bias, norm1_g, w_in, q_norm_g, k_norm_g, attn_sink, conv_w,
           branch_norm_a, branch_norm_c, w_out, norm2_g, w_ffn_gate, w_ffn_up, w_ffn_down):
    d = x_prompt.shape[-1]
    n_meta = meta_tokens.shape[0]
    n_buckets = rel_bias.shape[0]
    depth, _, in_dim = w_in.shape
    ffn = w_ffn_gate.shape[-1]
    kv_dim = (in_dim - 6 * d) // 2
    assert d % (GROUP * HEAD_DIM) == 0 and kv_dim == d // GROUP and attn_sink.shape[1] * HEAD_DIM == d
    assert n_meta % HALO_ROWS == 0 and n_meta < BLOCK

    groups = (x_prompt, x_sample)
    seq_chunks = []
    for xg in groups:
        assert xg.shape[1] % BLOCK == 0 and xg.shape[1] >= 2 * BLOCK
        for _ in range(xg.shape[0]):
            start = seq_chunks[-1][0] + seq_chunks[-1][1] if seq_chunks else 0
            seq_chunks.append((start, xg.shape[1] // BLOCK))
    seq_chunks = tuple(seq_chunks)
    n_seq = len(seq_chunks)

    x = jnp.concatenate([xg.reshape(-1, d) for xg in groups], axis=0)
    xm = jnp.tile(meta_tokens.astype(x.dtype), (n_seq, 1))

    cols = tuple(d * i for i in range(1, 6))
    k_col, v_col = 6 * d, 6 * d + kv_dim
    ffn_tile = 512
    ffn_p = -(-ffn // ffn_tile) * ffn_tile
    out_tile = _divisor_tile(d, 512, V7X_LANES)
    n_q, n_rest = d // kv_dim, 5 * d // kv_dim

    def regroup(j):
        return jnp.where(j < n_q, j, jnp.where(j < n_q + n_rest, j + 2, j - n_rest))

    bkt_tok, bkt_met = _bucket_tables(n_meta, n_buckets)

    geo = dict(seq_chunks=seq_chunks, n_meta=n_meta)
    for l in range(depth):
        with_meta = l < depth - 1
        wd_b = _cast_weight(w_ffn_down, l, ffn_p, d, _divisor_tile(d, 1024, V7X_LANES))
        table = jnp.concatenate([rel_bias, attn_sink[l][None]], axis=0)
        z, zm = _wmm(_rmsnorm(x, norm1_g[l]), _rmsnorm(xm, norm1_g[l]), [w_in], l, in_dim, kv_dim, BF16,
                     src_block=regroup)
        attn = _attention(z, zm, _bias_tiles(table, bkt_tok), q_norm_g[l], k_norm_g[l],
                          d_model=d, k_col=k_col, v_col=v_col, **geo)
        merged = _merge(attn, z, zm, conv_w[l], branch_norm_a[l], branch_norm_c[l], cols=cols, **geo)
        merged_m = hn_m = None
        if with_meta:
            attn_m = _attention_meta(z, zm, _bias_tiles(table, bkt_met), q_norm_g[l], k_norm_g[l],
                                     d_model=d, k_col=k_col, v_col=v_col, **geo)
            merged_m = _merge_meta(attn_m, z, zm, conv_w[l], branch_norm_a[l], branch_norm_c[l], cols=cols, **geo)
        h, *hm = _wmm(merged, merged_m, [w_out], l, d, out_tile, F32, res=x, auxres=xm if with_meta else None)
        if with_meta:
            hn_m = _rmsnorm(hm[0], norm2_g[l])
        t, *tm = _wmm(_rmsnorm(h, norm2_g[l]), hn_m, [w_ffn_gate, w_ffn_up], l, ffn_p, ffn_tile, BF16, act=True)
        x = _matmul_res(t, wd_b, h)
        if with_meta:
            xm = _matmul_res(tm[0], wd_b, hm[0])

    outs = []
    row = 0
    for xg in groups:
        rows = xg.shape[0] * xg.shape[1]
        outs.append(x[row:row + rows].reshape(xg.shape))
        row += rows
    return tuple(outs)
```

```python
import functools
import math

import jax
import jax.numpy as jnp
from jax import lax
from jax.experimental import pallas as pl
from jax.experimental.pallas import tpu as pltpu

HEAD_DIM = 128
GROUP = 4
BLOCK = 128
MAX_DISTANCE = 128
EPS = 1e-6

V7X_VMEM_BYTES = 64 << 20
V7X_LANES = 128
BF16_SUBLANES = 16
F32_SUBLANES = 8
HALO_ROWS = BF16_SUBLANES
VMEM_SLACK_BYTES = 10 << 20

F32 = jnp.float32
BF16 = jnp.bfloat16


def _params(dims, window_bytes):
    limit = min(2 * window_bytes + VMEM_SLACK_BYTES, V7X_VMEM_BYTES - (6 << 20))
    return pltpu.CompilerParams(dimension_semantics=dims, vmem_limit_bytes=int(limit))


def _divisor_tile(n, cap, unit):
    if n <= cap:
        return n
    t = (cap // unit) * unit
    while t >= unit:
        if n % t == 0:
            return t
        t -= unit
    raise ValueError(f"no tile for {n}")


def _nbytes(shape, dtype):
    return math.prod(shape) * jnp.dtype(dtype).itemsize


def _rms(x, g):
    ms = jnp.mean(x * x, axis=-1, keepdims=True)
    return x * lax.rsqrt(ms + EPS) * g


def _rmsnorm_kernel(*refs, starts):
    x_refs, g_ref, o_ref = refs[:len(starts)], refs[-2], refs[-1]
    i = pl.program_id(0)
    ends = starts[1:] + (None,)
    for x_ref, lo, hi in zip(x_refs, starts, ends):
        in_seg = i >= lo if hi is None else jnp.logical_and(i >= lo, i < hi)

        @pl.when(in_seg)
        def _():
            o_ref[...] = _rms(x_ref[...], g_ref[...]).astype(o_ref.dtype)


def _rmsnorm(xs, g):
    d = xs[0].shape[1]
    m = sum(x.shape[0] for x in xs)
    bm = _divisor_tile(math.gcd(*[x.shape[0] for x in xs]), 256, BF16_SUBLANES)
    blocks = [x.shape[0] // bm for x in xs]
    starts = tuple(sum(blocks[:s]) for s in range(len(xs)))
    in_specs = [pl.BlockSpec((bm, d), lambda i, start=start, nblk=nblk: (jnp.clip(i - start, 0, nblk - 1), 0))
                for start, nblk in zip(starts, blocks)]
    in_specs.append(pl.BlockSpec((1, d), lambda i: (0, 0)))
    return pl.pallas_call(
        functools.partial(_rmsnorm_kernel, starts=starts),
        grid=(m // bm,),
        in_specs=in_specs,
        out_specs=pl.BlockSpec((bm, d), lambda i: (i, 0)),
        out_shape=jax.ShapeDtypeStruct((m, d), BF16),
        compiler_params=_params(("parallel",), len(xs) * _nbytes((bm, d), F32) + _nbytes((bm, d), BF16)),
        name="rmsnorm",
    )(*xs, g.reshape(1, d))


def _wmm_kernel(*refs, n_w, res_starts, has_aux, has_side, act, n_tiles, rows, src_cols, side_rows):
    it = iter(refs)
    a_ref = next(it)
    aux_ref = next(it) if has_aux else None
    w_refs = [next(it) for _ in range(n_w)]
    res_refs = [next(it) for _ in res_starts]
    auxres_ref = next(it) if res_starts and has_aux else None
    side_ref = next(it) if has_side else None
    o_ref = next(it)
    oaux_ref = next(it) if has_aux else None
    oside_ref = next(it) if has_side else None
    w_bufs = (next(it), next(it))
    g = pl.program_id(0)
    i = pl.program_id(1)
    bn = w_bufs[0].shape[-1]

    if has_side:
        w = side_ref[...]
        row = g * w.shape[0] + lax.broadcasted_iota(jnp.int32, w.shape, 0)
        oside_ref[...] = jnp.where(row < side_rows, w, 0.0).astype(oside_ref.dtype)

    def stage(buf):
        row0 = pl.multiple_of(i * rows, rows)
        for t in range(n_w):
            w = w_refs[t][...]
            if src_cols % bn:
                col = jnp.minimum(g, n_tiles - 1) * bn + lax.broadcasted_iota(jnp.int32, w.shape, 1)
                w = jnp.where(col < src_cols, w, 0.0)
            buf[t, pl.ds(row0, rows), :] = w.astype(buf.dtype)

    def apply(buf, a, res):
        ys = [jnp.dot(a, buf[t], preferred_element_type=F32) for t in range(n_w)]
        y = ys[0] * jax.nn.sigmoid(ys[0]) * ys[1] if act else ys[0]
        return y if res is None else res + y

    def compute(buf):
        res = None
        if res_starts:
            res = res_refs[0][...]
            for start, ref in zip(res_starts[1:], res_refs[1:]):
                res = jnp.where(i >= start, ref[...], res)
        o_ref[...] = apply(buf, a_ref[...], res).astype(o_ref.dtype)
        if has_aux:
            @pl.when(i == 0)
            def _():
                oaux_ref[...] = apply(buf, aux_ref[...], auxres_ref[...] if res_starts else None).astype(oaux_ref.dtype)

    @pl.when(g == 0)
    def _():
        stage(w_bufs[0])

    for parity in range(2):
        @pl.when(jnp.logical_and(g > 0, g % 2 == parity))
        def _():
            stage(w_bufs[parity])
            compute(w_bufs[1 - parity])


def _wmm(a, aux, ws, layer, n_out, bn, out_dtype, *, src_block=None, act=False, res=None, auxres=None, side=None):
    m, k = a.shape
    src_cols = ws[0].shape[2]
    bm = _divisor_tile(m, 1024, BF16_SUBLANES)
    ni, n_tiles = m // bm, n_out // bn
    rows = k // ni
    assert k % ni == 0 and rows % BF16_SUBLANES == 0 and n_out % bn == 0
    assert src_block is None or src_cols % bn == 0
    src = src_block if src_block is not None else (lambda j: j)
    res = list(res) if res is not None else []
    has_aux, has_side = aux is not None, side is not None
    ma = aux.shape[0] if has_aux else 0
    res_blocks = [r.shape[0] // bm for r in res]
    assert all(r.shape[0] % bm == 0 for r in res) and sum(res_blocks) == (ni if res else 0)
    res_starts = tuple(sum(res_blocks[:s]) for s in range(len(res)))

    def row_blk(g, i):
        return jnp.where(g == 0, 0, i)

    def col_blk(g):
        return jnp.maximum(g - 1, 0)

    def last_tile(g):
        return jnp.minimum(g, n_tiles - 1)

    in_specs = [pl.BlockSpec((bm, k), lambda g, i: (row_blk(g, i), 0))]
    args = [a]
    if has_aux:
        in_specs.append(pl.BlockSpec((ma, k), lambda g, i: (0, 0)))
        args.append(aux)
    for w in ws:
        in_specs.append(pl.BlockSpec((None, rows, bn), lambda g, i: (layer, i, src(last_tile(g)))))
        args.append(w)
    for start, nblk, r in zip(res_starts, res_blocks, res):
        in_specs.append(pl.BlockSpec(
            (bm, bn), lambda g, i, start=start, nblk=nblk: (jnp.clip(row_blk(g, i) - start, 0, nblk - 1), col_blk(g))))
        args.append(r)
    if res and has_aux:
        in_specs.append(pl.BlockSpec((ma, bn), lambda g, i: (0, col_blk(g))))
        args.append(auxres)
    out_specs = [pl.BlockSpec((bm, bn), lambda g, i: (row_blk(g, i), col_blk(g)))]
    out_shape = [jax.ShapeDtypeStruct((m, n_out), out_dtype)]
    if has_aux:
        out_specs.append(pl.BlockSpec((ma, bn), lambda g, i: (0, col_blk(g))))
        out_shape.append(jax.ShapeDtypeStruct((ma, n_out), out_dtype))
    side_rows = side_bytes = 0
    if has_side:
        side_w, side_out_rows = side
        side_rows, side_cols = side_w.shape[1:]
        sr, sc = side_out_rows // n_tiles, side_cols // ni
        assert side_out_rows % n_tiles == 0 and side_cols % ni == 0 and sr % BF16_SUBLANES == 0 and sc % V7X_LANES == 0
        in_specs.append(pl.BlockSpec((None, sr, sc), lambda g, i: (layer, last_tile(g), i)))
        args.append(side_w)
        out_specs.append(pl.BlockSpec((sr, sc), lambda g, i: (g, i)))
        out_shape.append(jax.ShapeDtypeStruct((side_out_rows + sr, side_cols), BF16))
        side_bytes = _nbytes((sr, sc), F32) + _nbytes((sr, sc), BF16)
    win = (_nbytes((bm + ma, k), BF16) + len(ws) * _nbytes((rows, bn), F32) + side_bytes
           + _nbytes((bm + ma, bn), out_dtype) + len(res) * _nbytes((bm + ma, bn), F32))
    scratch = _nbytes((2, len(ws), k, bn), BF16)
    return pl.pallas_call(
        functools.partial(_wmm_kernel, n_w=len(ws), res_starts=res_starts, has_aux=has_aux, has_side=has_side,
                          act=act, n_tiles=n_tiles, rows=rows, src_cols=src_cols, side_rows=side_rows),
        grid=(n_tiles + 1, ni),
        in_specs=in_specs,
        out_specs=out_specs,
        out_shape=out_shape,
        scratch_shapes=[pltpu.VMEM((len(ws), k, bn), BF16), pltpu.VMEM((len(ws), k, bn), BF16)],
        compiler_params=_params(("arbitrary", "arbitrary"), win + scratch // 2),
        name="wmm",
    )(*args)


def _mm_res_kernel(a_ref, b_ref, r_ref, o_ref):
    @pl.when(pl.program_id(2) == 0)
    def _():
        o_ref[...] = r_ref[...]

    o_ref[...] += jnp.dot(a_ref[...], b_ref[...], preferred_element_type=F32)


def _matmul_res(a, b, res, row0=0, nrows=None):
    m, k = a.shape
    n = b.shape[1]
    nrows = m if nrows is None else nrows
    bm = _divisor_tile(nrows, 1024, BF16_SUBLANES)
    bn = _divisor_tile(n, 1024, V7X_LANES)
    bk = _divisor_tile(k, 4096, V7X_LANES)
    assert row0 % bm == 0
    off = row0 // bm
    win = _nbytes((bm, bk), BF16) + _nbytes((bk, bn), BF16) + 2 * _nbytes((bm, bn), F32)
    return pl.pallas_call(
        _mm_res_kernel,
        grid=(nrows // bm, n // bn, k // bk),
        in_specs=[pl.BlockSpec((bm, bk), lambda i, j, kk: (i + off, kk)),
                  pl.BlockSpec((bk, bn), lambda i, j, kk: (kk, j)),
                  pl.BlockSpec((bm, bn), lambda i, j, kk: (i + off, j))],
        out_specs=pl.BlockSpec((bm, bn), lambda i, j, kk: (i, j)),
        out_shape=jax.ShapeDtypeStruct((nrows, n), F32),
        compiler_params=_params(("parallel", "parallel", "arbitrary"), win),
        name="matmul_res",
    )(a, b, res)


def _t5_bucket(rel, n_buckets):
    half = n_buckets // 2
    exact = half // 2
    n = jnp.abs(rel)
    n_f = jnp.maximum(n, 1).astype(F32)
    large = exact + (jnp.log(n_f / exact) / math.log(MAX_DISTANCE / exact) * (half - exact)).astype(jnp.int32)
    large = jnp.minimum(large, half - 1)
    return jnp.where(rel > 0, half, 0) + jnp.where(n < exact, n, large)


def _bucket_tables(n_meta, n_buckets):
    qi = jnp.arange(BLOCK)[:, None]
    sj = jnp.arange(3 * BLOCK)[None, :]
    rel_band = sj - BLOCK - qi
    band = jnp.where(jnp.abs(rel_band) <= BLOCK, _t5_bucket(rel_band, n_buckets), -1)
    masked = jnp.full((BLOCK, BLOCK), -1, jnp.int32)
    sink = jnp.full((BLOCK, 1), n_buckets, jnp.int32)
    pad = jnp.full((BLOCK, BLOCK - n_meta - 1), -1, jnp.int32)
    mk = jnp.arange(n_meta)[None, :]
    meta_first = _t5_bucket(mk - (n_meta + qi), n_buckets)
    meta_far = jnp.full((BLOCK, n_meta), n_buckets // 2 - 1, jnp.int32)
    interior = jnp.concatenate([band, meta_far, sink, pad], axis=1)
    first = jnp.concatenate([masked, band[:, BLOCK:], meta_first, sink, pad], axis=1)
    last = jnp.concatenate([band[:, :2 * BLOCK], masked, meta_far, sink, pad], axis=1)
    tok = jnp.stack([interior, first, last]).astype(jnp.int32)

    mq = jnp.arange(n_meta)[:, None]
    tk = jnp.arange(BLOCK)[None, :]
    rel_tok = n_meta + tk - mq
    mband = jnp.where(jnp.abs(rel_tok) <= BLOCK, _t5_bucket(rel_tok, n_buckets), -1)
    mmeta = _t5_bucket(mk - mq, n_buckets)
    met = jnp.concatenate([mband, mmeta, sink[:n_meta], pad[:n_meta]], axis=1).astype(jnp.int32)
    return tok, met[None]


def _bias_kernel(tab_ref, bkt_ref, o_ref, *, n_ids):
    h = pl.program_id(0)
    bkt = bkt_ref[...]
    acc = jnp.full(bkt.shape, -jnp.inf, F32)
    for b in range(n_ids):
        acc = jnp.where(bkt == b, tab_ref[b, h], acc)
    o_ref[0, :, 0] = acc


def _bias_tiles(table, bkt):
    n_ids, n_heads = table.shape
    v, r, c = bkt.shape
    return pl.pallas_call(
        functools.partial(_bias_kernel, n_ids=n_ids),
        grid=(n_heads,),
        in_specs=[pl.BlockSpec(memory_space=pltpu.SMEM),
                  pl.BlockSpec((v, r, c), lambda h: (0, 0, 0))],
        out_specs=pl.BlockSpec((1, v, 1, r, c), lambda h: (h // GROUP, 0, h % GROUP, 0, 0)),
        out_shape=jax.ShapeDtypeStruct((n_heads // GROUP, v, GROUP, r, c), F32),
        compiler_params=_params(("parallel",), 2 * _nbytes(bkt.shape, F32)),
        name="bias_tiles",
    )(table, bkt)


def _softmax_pv(q, k, v, bias, scale):
    s = lax.dot_general(q, k, (((1,), (1,)), ((), ())), preferred_element_type=F32) * scale + bias
    m = jnp.max(s, axis=-1, keepdims=True)
    p = jnp.exp(s - m)
    denom = jnp.sum(p, axis=-1, keepdims=True)
    pn = (p * (1.0 / denom)).astype(BF16)
    return jnp.dot(pn, v, preferred_element_type=F32)


def _fill_meta_keys(km_scr, vm_scr, km_ref, vm_ref, kg, n_meta):
    km_scr[...] = jnp.zeros(km_scr.shape, BF16)
    vm_scr[...] = jnp.zeros(vm_scr.shape, BF16)
    km_scr[0:n_meta] = _rms(km_ref[...].astype(F32), kg).astype(BF16)
    vm_scr[0:n_meta] = vm_ref[...].astype(BF16)


def _attn_kernel(qg_ref, kg_ref, bias_ref, q_ref, kp_ref, kc_ref, kn_ref, vp_ref, vc_ref, vn_ref,
                 km_ref, vm_ref, o_ref, q_scr, k_scr, v_scr, km_scr, vm_scr, *, chunks, seq_chunks, n_meta, scale):
    r = pl.program_id(1)
    qg = qg_ref[...]
    kg = kg_ref[...]
    body = chunks * BLOCK
    stack = GROUP * BLOCK

    k_scr[0:BLOCK] = _rms(kp_ref[...].astype(F32), kg).astype(BF16)
    k_scr[BLOCK:BLOCK + body] = _rms(kc_ref[...].astype(F32), kg).astype(BF16)
    k_scr[BLOCK + body:2 * BLOCK + body] = _rms(kn_ref[...].astype(F32), kg).astype(BF16)
    v_scr[0:BLOCK] = vp_ref[...].astype(BF16)
    v_scr[BLOCK:BLOCK + body] = vc_ref[...].astype(BF16)
    v_scr[BLOCK + body:2 * BLOCK + body] = vn_ref[...].astype(BF16)
    _fill_meta_keys(km_scr, vm_scr, km_ref, vm_ref, kg, n_meta)
    for g in range(GROUP):
        qn = _rms(q_ref[:, g * HEAD_DIM:(g + 1) * HEAD_DIM].astype(F32), qg).astype(BF16)
        for c in range(chunks):
            q_scr[(c * GROUP + g) * BLOCK:(c * GROUP + g + 1) * BLOCK] = qn[c * BLOCK:(c + 1) * BLOCK]
    k_meta = km_scr[...]
    v_meta = vm_scr[...]

    def chunk(cc, carry):
        gch = r * chunks + cc
        is_first = functools.reduce(jnp.logical_or, [gch == s0 for s0, _ in seq_chunks])
        is_last = functools.reduce(jnp.logical_or, [gch == s0 + n - 1 for s0, n in seq_chunks])
        var = jnp.where(is_first, 1, jnp.where(is_last, 2, 0))
        row0 = pl.multiple_of(cc * BLOCK, BLOCK)
        k_all = jnp.concatenate([k_scr[pl.ds(row0, 3 * BLOCK), :], k_meta], axis=0)
        v_all = jnp.concatenate([v_scr[pl.ds(row0, 3 * BLOCK), :], v_meta], axis=0)
        q = q_scr[pl.ds(pl.multiple_of(cc * stack, stack), stack), :]
        bias = bias_ref[0, var].reshape(stack, 4 * BLOCK)
        o = _softmax_pv(q, k_all, v_all, bias, scale)
        for g in range(GROUP):
            o_ref[pl.ds(row0, BLOCK), g * HEAD_DIM:(g + 1) * HEAD_DIM] = o[g * BLOCK:(g + 1) * BLOCK]
        return carry

    lax.fori_loop(0, chunks, chunk, 0, unroll=True)


def _seq_of_chunk(c, seq_chunks):
    s = 0
    for s0, _ in seq_chunks[1:]:
        s = s + (c >= s0).astype(jnp.int32)
    return s


def _attention(z, zm, bias_tok, q_g, k_g, *, seq_chunks, n_meta, d_model, k_col, v_col):
    m = z.shape[0]
    n_chunks = m // BLOCK
    kv_heads = d_model // (GROUP * HEAD_DIM)
    chunks = math.gcd(8, *[n for _, n in seq_chunks])
    body = chunks * BLOCK
    qw = GROUP * HEAD_DIM
    kb, vb = k_col // HEAD_DIM, v_col // HEAD_DIM
    seq_of = functools.partial(_seq_of_chunk, seq_chunks=seq_chunks)

    def halo_prev(h, r):
        return jnp.maximum(r * chunks - 1, 0)

    def halo_next(h, r):
        return jnp.minimum(r * chunks + chunks, n_chunks - 1)

    in_specs = [
        pl.BlockSpec((1, HEAD_DIM), lambda h, r: (0, 0)),
        pl.BlockSpec((1, HEAD_DIM), lambda h, r: (0, 0)),
        pl.BlockSpec((1, 3, GROUP, BLOCK, 4 * BLOCK), lambda h, r: (h, 0, 0, 0, 0)),
        pl.BlockSpec((body, qw), lambda h, r: (r, h)),
        pl.BlockSpec((BLOCK, HEAD_DIM), lambda h, r: (halo_prev(h, r), kb + h)),
        pl.BlockSpec((body, HEAD_DIM), lambda h, r: (r, kb + h)),
        pl.BlockSpec((BLOCK, HEAD_DIM), lambda h, r: (halo_next(h, r), kb + h)),
        pl.BlockSpec((BLOCK, HEAD_DIM), lambda h, r: (halo_prev(h, r), vb + h)),
        pl.BlockSpec((body, HEAD_DIM), lambda h, r: (r, vb + h)),
        pl.BlockSpec((BLOCK, HEAD_DIM), lambda h, r: (halo_next(h, r), vb + h)),
        pl.BlockSpec((n_meta, HEAD_DIM), lambda h, r: (seq_of(r * chunks), kb + h)),
        pl.BlockSpec((n_meta, HEAD_DIM), lambda h, r: (seq_of(r * chunks), vb + h)),
    ]
    win = (_nbytes((3, GROUP, BLOCK, 4 * BLOCK), F32) + 2 * _nbytes((body, qw), F32)
           + 4 * _nbytes((body + 2 * BLOCK, HEAD_DIM), F32))
    return pl.pallas_call(
        functools.partial(_attn_kernel, chunks=chunks, seq_chunks=seq_chunks, n_meta=n_meta,
                          scale=HEAD_DIM ** -0.5),
        grid=(kv_heads, m // body),
        in_specs=in_specs,
        out_specs=pl.BlockSpec((body, qw), lambda h, r: (r, h)),
        out_shape=jax.ShapeDtypeStruct((m, d_model), F32),
        scratch_shapes=[pltpu.VMEM((body * GROUP, HEAD_DIM), BF16),
                        pltpu.VMEM((body + 2 * BLOCK, HEAD_DIM), BF16),
                        pltpu.VMEM((body + 2 * BLOCK, HEAD_DIM), BF16),
                        pltpu.VMEM((BLOCK, HEAD_DIM), BF16),
                        pltpu.VMEM((BLOCK, HEAD_DIM), BF16)],
        compiler_params=_params(("parallel", "arbitrary"), win),
        name="attention",
    )(q_g.reshape(1, HEAD_DIM), k_g.reshape(1, HEAD_DIM), bias_tok, z, z, z, z, z, z, z, zm, zm)


def _attn_meta_kernel(qg_ref, kg_ref, bias_ref, q_ref, k1_ref, v1_ref, km_ref, vm_ref, o_ref,
                      km_scr, vm_scr, *, n_meta, scale):
    qg = qg_ref[...]
    kg = kg_ref[...]
    _fill_meta_keys(km_scr, vm_scr, km_ref, vm_ref, kg, n_meta)
    k_all = jnp.concatenate([_rms(k1_ref[...].astype(F32), kg).astype(BF16), km_scr[...]], axis=0)
    v_all = jnp.concatenate([v1_ref[...].astype(BF16), vm_scr[...]], axis=0)
    q = q_ref[...].astype(F32)
    qn = jnp.concatenate([_rms(q[:, g * HEAD_DIM:(g + 1) * HEAD_DIM], qg).astype(BF16) for g in range(GROUP)], axis=0)
    o = _softmax_pv(qn, k_all, v_all, bias_ref[0, 0].reshape(GROUP * n_meta, 2 * BLOCK), scale)
    for g in range(GROUP):
        o_ref[:, g * HEAD_DIM:(g + 1) * HEAD_DIM] = o[g * n_meta:(g + 1) * n_meta]


def _attention_meta(z, zm, bias_met, q_g, k_g, *, seq_chunks, n_meta, d_model, k_col, v_col):
    n_seq = len(seq_chunks)
    kv_heads = d_model // (GROUP * HEAD_DIM)
    qw = GROUP * HEAD_DIM
    kb, vb = k_col // HEAD_DIM, v_col // HEAD_DIM

    def first_chunk(s):
        c = 0
        for i, (s0, _) in enumerate(seq_chunks):
            c = c + jnp.where(s == i, s0, 0)
        return c

    in_specs = [
        pl.BlockSpec((1, HEAD_DIM), lambda s, h: (0, 0)),
        pl.BlockSpec((1, HEAD_DIM), lambda s, h: (0, 0)),
        pl.BlockSpec((1, 1, GROUP, n_meta, 2 * BLOCK), lambda s, h: (h, 0, 0, 0, 0)),
        pl.BlockSpec((n_meta, qw), lambda s, h: (s, h)),
        pl.BlockSpec((BLOCK, HEAD_DIM), lambda s, h: (first_chunk(s), kb + h)),
        pl.BlockSpec((BLOCK, HEAD_DIM), lambda s, h: (first_chunk(s), vb + h)),
        pl.BlockSpec((n_meta, HEAD_DIM), lambda s, h: (s, kb + h)),
        pl.BlockSpec((n_meta, HEAD_DIM), lambda s, h: (s, vb + h)),
    ]
    return pl.pallas_call(
        functools.partial(_attn_meta_kernel, n_meta=n_meta, scale=HEAD_DIM ** -0.5),
        grid=(n_seq, kv_heads),
        in_specs=in_specs,
        out_specs=pl.BlockSpec((n_meta, qw), lambda s, h: (s, h)),
        out_shape=jax.ShapeDtypeStruct((n_seq * n_meta, d_model), F32),
        scratch_shapes=[pltpu.VMEM((BLOCK, HEAD_DIM), BF16), pltpu.VMEM((BLOCK, HEAD_DIM), BF16)],
        compiler_params=_params(("parallel", "parallel"), 1 << 20),
        name="attention_meta",
    )(q_g.reshape(1, HEAD_DIM), k_g.reshape(1, HEAD_DIM), bias_met, zm, z, z, zm, zm)


def _merge_math(attn, gb, gc, hc, ga, gcv, u_prev, u_next, cw, na, nc):
    u = gc * hc
    rows = u.shape[0]
    ridx = lax.broadcasted_iota(jnp.int32, u.shape, 0)
    up = jnp.where(ridx == 0, u_prev, pltpu.roll(u, 1, axis=0))
    un = jnp.where(ridx == rows - 1, u_next, pltpu.roll(u, rows - 1, axis=0))
    conv = gb * (cw[0:1] * up + cw[1:2] * u + cw[2:3] * un)
    return jax.nn.sigmoid(ga) * _rms(attn, na) + jax.nn.sigmoid(gcv) * _rms(conv, nc)


def _merge_kernel(attn_ref, gb_ref, gc_ref, hc_ref, ga_ref, gcv_ref, gcp_ref, hcp_ref, gcn_ref, hcn_ref,
                  gcm_ref, hcm_ref, cw_ref, na_ref, nc_ref, o_ref, conv_scr, *, seq_chunks):
    c = pl.program_id(0)
    is_first = functools.reduce(jnp.logical_or, [c == s0 for s0, _ in seq_chunks])
    is_last = functools.reduce(jnp.logical_or, [c == s0 + n - 1 for s0, n in seq_chunks])
    last = HALO_ROWS - 1
    rows, d = o_ref.shape
    n_tiles = d // V7X_LANES
    ridx = lax.broadcasted_iota(jnp.int32, (rows, V7X_LANES), 0)

    def cols(ct):
        return pl.ds(pl.multiple_of(ct * V7X_LANES, V7X_LANES), V7X_LANES)

    def f32(ref, sl):
        return ref[:, sl].astype(F32)

    def conv_pass(ct, carry):
        ss_attn, ss_conv = carry
        sl = cols(ct)
        u = f32(gc_ref, sl) * f32(hc_ref, sl)
        u_prev_tok = (f32(gcp_ref, sl) * f32(hcp_ref, sl))[last:last + 1]
        u_prev_meta = (f32(gcm_ref, sl) * f32(hcm_ref, sl))[last:last + 1]
        u_prev = jnp.where(is_first, u_prev_meta, u_prev_tok)
        u_next_tok = (f32(gcn_ref, sl) * f32(hcn_ref, sl))[0:1]
        u_next = jnp.where(is_last, jnp.zeros_like(u_next_tok), u_next_tok)
        up = jnp.where(ridx == 0, u_prev, pltpu.roll(u, 1, axis=0))
        un = jnp.where(ridx == rows - 1, u_next, pltpu.roll(u, rows - 1, axis=0))
        cw = cw_ref[:, sl]
        conv = f32(gb_ref, sl) * (cw[0:1] * up + cw[1:2] * u + cw[2:3] * un)
        conv_scr[:, sl] = conv
        attn = attn_ref[:, sl]
        return ss_attn + attn * attn, ss_conv + conv * conv

    zero = jnp.zeros((rows, V7X_LANES), F32)
    ss_attn, ss_conv = lax.fori_loop(0, n_tiles, conv_pass, (zero, zero))
    r_attn = lax.rsqrt(jnp.sum(ss_attn, axis=-1, keepdims=True) / d + EPS)
    r_conv = lax.rsqrt(jnp.sum(ss_conv, axis=-1, keepdims=True) / d + EPS)

    def gate_pass(ct, carry):
        sl = cols(ct)
        a = attn_ref[:, sl] * r_attn * na_ref[:, sl]
        c = conv_scr[:, sl] * r_conv * nc_ref[:, sl]
        o_ref[:, sl] = (jax.nn.sigmoid(f32(ga_ref, sl)) * a + jax.nn.sigmoid(f32(gcv_ref, sl)) * c).astype(o_ref.dtype)
        return carry

    lax.fori_loop(0, n_tiles, gate_pass, 0)


def _merge(attn, z, zm, conv_w, norm_a, norm_c, *, seq_chunks, n_meta, cols):
    m, d = attn.shape
    n_chunks = m // BLOCK
    per_chunk = BLOCK // HALO_ROWS
    per_meta = n_meta // HALO_ROWS
    gb, gc, hc, ga, gcv = [c // d for c in cols]
    seq_of = functools.partial(_seq_of_chunk, seq_chunks=seq_chunks)

    def main(col):
        return pl.BlockSpec((BLOCK, d), lambda c: (c, col))

    def prev_rows(col):
        return pl.BlockSpec((HALO_ROWS, d), lambda c: (jnp.maximum(c * per_chunk - 1, 0), col))

    def next_rows(col):
        return pl.BlockSpec((HALO_ROWS, d), lambda c: (jnp.minimum(c + 1, n_chunks - 1) * per_chunk, col))

    def meta_rows(col):
        return pl.BlockSpec((HALO_ROWS, d), lambda c: (seq_of(c) * per_meta + per_meta - 1, col))

    def row(nrows):
        return pl.BlockSpec((nrows, d), lambda c: (0, 0))

    win = 6 * _nbytes((BLOCK, d), F32) + _nbytes((BLOCK, d), BF16) + 8 * _nbytes((HALO_ROWS, d), F32)
    return pl.pallas_call(
        functools.partial(_merge_kernel, seq_chunks=seq_chunks),
        grid=(n_chunks,),
        in_specs=[main(0), main(gb), main(gc), main(hc), main(ga), main(gcv),
                  prev_rows(gc), prev_rows(hc), next_rows(gc), next_rows(hc), meta_rows(gc), meta_rows(hc),
                  row(3), row(1), row(1)],
        out_specs=pl.BlockSpec((BLOCK, d), lambda c: (c, 0)),
        out_shape=jax.ShapeDtypeStruct((m, d), BF16),
        scratch_shapes=[pltpu.VMEM((BLOCK, d), F32)],
        compiler_params=_params(("parallel",), win),
        name="merge",
    )(attn, z, z, z, z, z, z, z, z, z, zm, zm, conv_w, norm_a.reshape(1, d), norm_c.reshape(1, d))


def _merge_meta_kernel(attn_ref, gb_ref, gc_ref, hc_ref, ga_ref, gcv_ref, gcn_ref, hcn_ref,
                       cw_ref, na_ref, nc_ref, o_ref):
    f32 = lambda ref: ref[...].astype(F32)
    u_next = (f32(gcn_ref) * f32(hcn_ref))[0:1]
    u_prev = jnp.zeros_like(u_next)
    o_ref[...] = _merge_math(attn_ref[...], f32(gb_ref), f32(gc_ref), f32(hc_ref), f32(ga_ref), f32(gcv_ref),
                             u_prev, u_next, cw_ref[...], na_ref[...], nc_ref[...]).astype(o_ref.dtype)


def _merge_meta(attn_m, z, zm, conv_w, norm_a, norm_c, *, seq_chunks, n_meta, cols):
    mm, d = attn_m.shape
    per_chunk = BLOCK // HALO_ROWS
    gb, gc, hc, ga, gcv = [c // d for c in cols]

    def first_rows(s):
        r = 0
        for i, (s0, _) in enumerate(seq_chunks):
            r = r + jnp.where(s == i, s0 * per_chunk, 0)
        return r

    def main(col):
        return pl.BlockSpec((n_meta, d), lambda s: (s, col))

    def next_rows(col):
        return pl.BlockSpec((HALO_ROWS, d), lambda s: (first_rows(s), col))

    def row(nrows):
        return pl.BlockSpec((nrows, d), lambda s: (0, 0))

    win = 7 * _nbytes((n_meta, d), F32) + 2 * _nbytes((HALO_ROWS, d), F32)
    return pl.pallas_call(
        _merge_meta_kernel,
        grid=(len(seq_chunks),),
        in_specs=[main(0), main(gb), main(gc), main(hc), main(ga), main(gcv), next_rows(gc), next_rows(hc),
                  row(3), row(1), row(1)],
        out_specs=pl.BlockSpec((n_meta, d), lambda s: (s, 0)),
        out_shape=jax.ShapeDtypeStruct((mm, d), BF16),
        compiler_params=_params(("parallel",), win),
        name="merge_meta",
    )(attn_m, zm, zm, zm, zm, zm, z, z, conv_w, norm_a.reshape(1, d), norm_c.reshape(1, d))


def kernel(x_prompt, x_sample, meta_tokens, rel_bias, norm1_g, w_in, q_norm_g, k_norm_g, attn_sink, conv_w,
           branch_norm_a, branch_norm_c, w_out, norm2_g, w_ffn_gate, w_ffn_up, w_ffn_down):
    d = x_prompt.shape[-1]
    n_meta = meta_tokens.shape[0]
    n_buckets = rel_bias.shape[0]
    depth, _, in_dim = w_in.shape
    ffn = w_ffn_gate.shape[-1]
    kv_dim = (in_dim - 6 * d) // 2
    assert d % (GROUP * HEAD_DIM) == 0 and kv_dim == d // GROUP and attn_sink.shape[1] * HEAD_DIM == d
    assert n_meta % HALO_ROWS == 0 and n_meta < BLOCK

    groups = (x_prompt, x_sample)
    seq_chunks = []
    for xg in groups:
        assert xg.shape[1] % BLOCK == 0 and xg.shape[1] >= 2 * BLOCK
        for _ in range(xg.shape[0]):
            start = seq_chunks[-1][0] + seq_chunks[-1][1] if seq_chunks else 0
            seq_chunks.append((start, xg.shape[1] // BLOCK))
    seq_chunks = tuple(seq_chunks)
    n_seq = len(seq_chunks)

    x_parts = [xg.reshape(-1, d) for xg in groups]
    xm = jnp.tile(meta_tokens.astype(F32), (n_seq, 1))

    cols = tuple(d * i for i in range(1, 6))
    k_col, v_col = 6 * d, 6 * d + kv_dim
    ffn_tile = 512
    ffn_p = -(-ffn // ffn_tile) * ffn_tile
    out_tile = _divisor_tile(d, 512, V7X_LANES)
    n_q, n_rest = d // kv_dim, 5 * d // kv_dim

    def regroup(j):
        return jnp.where(j < n_q, j, jnp.where(j < n_q + n_rest, j + 2, j - n_rest))

    bkt_tok, bkt_met = _bucket_tables(n_meta, n_buckets)

    geo = dict(seq_chunks=seq_chunks, n_meta=n_meta)
    for l in range(depth):
        with_meta = l < depth - 1
        table = jnp.concatenate([rel_bias, attn_sink[l][None]], axis=0)
        z, zm = _wmm(_rmsnorm(x_parts, norm1_g[l]), _rmsnorm([xm], norm1_g[l]), [w_in], l, in_dim, kv_dim, BF16,
                     src_block=regroup)
        attn = _attention(z, zm, _bias_tiles(table, bkt_tok), q_norm_g[l], k_norm_g[l],
                          d_model=d, k_col=k_col, v_col=v_col, **geo)
        merged = _merge(attn, z, zm, conv_w[l], branch_norm_a[l], branch_norm_c[l], cols=cols, **geo)
        merged_m = hn_m = None
        if with_meta:
            attn_m = _attention_meta(z, zm, _bias_tiles(table, bkt_met), q_norm_g[l], k_norm_g[l],
                                     d_model=d, k_col=k_col, v_col=v_col, **geo)
            merged_m = _merge_meta(attn_m, z, zm, conv_w[l], branch_norm_a[l], branch_norm_c[l], cols=cols, **geo)
        h, *hm = _wmm(merged, merged_m, [w_out], l, d, out_tile, F32, res=x_parts,
                      auxres=xm if with_meta else None)
        if with_meta:
            hn_m = _rmsnorm([hm[0]], norm2_g[l])
        t, *tm, wd_b = _wmm(_rmsnorm([h], norm2_g[l]), hn_m, [w_ffn_gate, w_ffn_up], l, ffn_p, ffn_tile, BF16,
                            act=True, side=(w_ffn_down, ffn_p))
        if with_meta:
            x_parts = [_matmul_res(t, wd_b, h)]
            xm = _matmul_res(tm[0], wd_b, hm[0])

    outs = []
    row = 0
    for xg in groups:
        nrows = xg.shape[0] * xg.shape[1]
        outs.append(_matmul_res(t, wd_b, h, row, nrows).reshape(xg.shape))
        row += nrows
    return tuple(outs)
```

```python
import functools
import math

import jax
import jax.numpy as jnp
from jax import lax
from jax.experimental import pallas as pl
from jax.experimental.pallas import tpu as pltpu

HEAD_DIM = 128
GROUP = 4
BLOCK = 128
MAX_DISTANCE = 128
EPS = 1e-6
LOG2E = math.log2(math.e)

V7X_VMEM_BYTES = 64 << 20
V7X_LANES = 128
BF16_SUBLANES = 16
F32_SUBLANES = 8
HALO_ROWS = BF16_SUBLANES
VMEM_SLACK_BYTES = 10 << 20

F32 = jnp.float32
BF16 = jnp.bfloat16


def _params(dims, window_bytes):
    limit = min(2 * window_bytes + VMEM_SLACK_BYTES, V7X_VMEM_BYTES - (6 << 20))
    return pltpu.CompilerParams(dimension_semantics=dims, vmem_limit_bytes=int(limit))


def _divisor_tile(n, cap, unit):
    if n <= cap:
        return n
    t = (cap // unit) * unit
    while t >= unit:
        if n % t == 0:
            return t
        t -= unit
    raise ValueError(f"no tile for {n}")


def _nbytes(shape, dtype):
    return math.prod(shape) * jnp.dtype(dtype).itemsize


def _rms(x, g):
    ms = jnp.mean(x * x, axis=-1, keepdims=True)
    return x * lax.rsqrt(ms + EPS) * g


def _rmsnorm_kernel(*refs, starts):
    x_refs, g_ref, o_ref = refs[:len(starts)], refs[-2], refs[-1]
    i = pl.program_id(0)
    ends = starts[1:] + (None,)
    for x_ref, lo, hi in zip(x_refs, starts, ends):
        in_seg = i >= lo if hi is None else jnp.logical_and(i >= lo, i < hi)

        @pl.when(in_seg)
        def _():
            o_ref[...] = _rms(x_ref[...], g_ref[...]).astype(o_ref.dtype)


def _rmsnorm(xs, g):
    d = xs[0].shape[1]
    m = sum(x.shape[0] for x in xs)
    bm = _divisor_tile(math.gcd(*[x.shape[0] for x in xs]), 256, BF16_SUBLANES)
    blocks = [x.shape[0] // bm for x in xs]
    starts = tuple(sum(blocks[:s]) for s in range(len(xs)))
    in_specs = [pl.BlockSpec((bm, d), lambda i, start=start, nblk=nblk: (jnp.clip(i - start, 0, nblk - 1), 0))
                for start, nblk in zip(starts, blocks)]
    in_specs.append(pl.BlockSpec((1, d), lambda i: (0, 0)))
    return pl.pallas_call(
        functools.partial(_rmsnorm_kernel, starts=starts),
        grid=(m // bm,),
        in_specs=in_specs,
        out_specs=pl.BlockSpec((bm, d), lambda i: (i, 0)),
        out_shape=jax.ShapeDtypeStruct((m, d), BF16),
        compiler_params=_params(("parallel",), len(xs) * _nbytes((bm, d), F32) + _nbytes((bm, d), BF16)),
        name="rmsnorm",
    )(*xs, g.reshape(1, d))


def _wmm_kernel(*refs, n_w, res_starts, has_aux, has_side, act, n_tiles, rows, src_cols, side_rows):
    it = iter(refs)
    a_ref = next(it)
    aux_ref = next(it) if has_aux else None
    w_refs = [next(it) for _ in range(n_w)]
    res_refs = [next(it) for _ in res_starts]
    auxres_ref = next(it) if res_starts and has_aux else None
    side_ref = next(it) if has_side else None
    o_ref = next(it)
    oaux_ref = next(it) if has_aux else None
    oside_ref = next(it) if has_side else None
    w_bufs = (next(it), next(it))
    g = pl.program_id(0)
    i = pl.program_id(1)
    bn = w_bufs[0].shape[-1]

    if has_side:
        w = side_ref[...]
        row = g * w.shape[0] + lax.broadcasted_iota(jnp.int32, w.shape, 0)
        oside_ref[...] = jnp.where(row < side_rows, w, 0.0).astype(oside_ref.dtype)

    def stage(buf):
        row0 = pl.multiple_of(i * rows, rows)
        for t in range(n_w):
            w = w_refs[t][...]
            if src_cols % bn:
                col = jnp.minimum(g, n_tiles - 1) * bn + lax.broadcasted_iota(jnp.int32, w.shape, 1)
                w = jnp.where(col < src_cols, w, 0.0)
            buf[t, pl.ds(row0, rows), :] = w.astype(buf.dtype)

    def apply(buf, a, res):
        ys = [jnp.dot(a, buf[t], preferred_element_type=F32) for t in range(n_w)]
        y = ys[0] * jax.nn.sigmoid(ys[0]) * ys[1] if act else ys[0]
        return y if res is None else res + y

    def compute(buf):
        res = None
        if res_starts:
            res = res_refs[0][...]
            for start, ref in zip(res_starts[1:], res_refs[1:]):
                res = jnp.where(i >= start, ref[...], res)
        o_ref[...] = apply(buf, a_ref[...], res).astype(o_ref.dtype)
        if has_aux:
            @pl.when(i == 0)
            def _():
                oaux_ref[...] = apply(buf, aux_ref[...], auxres_ref[...] if res_starts else None).astype(oaux_ref.dtype)

    @pl.when(g == 0)
    def _():
        stage(w_bufs[0])

    for parity in range(2):
        @pl.when(jnp.logical_and(g > 0, g % 2 == parity))
        def _():
            stage(w_bufs[parity])
            compute(w_bufs[1 - parity])


def _wmm(a, aux, ws, layer, n_out, bn, out_dtype, *, src_block=None, act=False, res=None, auxres=None, side=None):
    m, k = a.shape
    src_cols = ws[0].shape[2]
    bm = _divisor_tile(m, 1024, BF16_SUBLANES)
    ni, n_tiles = m // bm, n_out // bn
    rows = k // ni
    assert k % ni == 0 and rows % BF16_SUBLANES == 0 and n_out % bn == 0
    assert src_block is None or src_cols % bn == 0
    src = src_block if src_block is not None else (lambda j: j)
    res = list(res) if res is not None else []
    has_aux, has_side = aux is not None, side is not None
    ma = aux.shape[0] if has_aux else 0
    res_blocks = [r.shape[0] // bm for r in res]
    assert all(r.shape[0] % bm == 0 for r in res) and sum(res_blocks) == (ni if res else 0)
    res_starts = tuple(sum(res_blocks[:s]) for s in range(len(res)))

    def row_blk(g, i):
        return jnp.where(g == 0, 0, i)

    def col_blk(g):
        return jnp.maximum(g - 1, 0)

    def last_tile(g):
        return jnp.minimum(g, n_tiles - 1)

    in_specs = [pl.BlockSpec((bm, k), lambda g, i: (row_blk(g, i), 0))]
    args = [a]
    if has_aux:
        in_specs.append(pl.BlockSpec((ma, k), lambda g, i: (0, 0)))
        args.append(aux)
    for w in ws:
        in_specs.append(pl.BlockSpec((None, rows, bn), lambda g, i: (layer, i, src(last_tile(g)))))
        args.append(w)
    for start, nblk, r in zip(res_starts, res_blocks, res):
        in_specs.append(pl.BlockSpec(
            (bm, bn), lambda g, i, start=start, nblk=nblk: (jnp.clip(row_blk(g, i) - start, 0, nblk - 1), col_blk(g))))
        args.append(r)
    if res and has_aux:
        in_specs.append(pl.BlockSpec((ma, bn), lambda g, i: (0, col_blk(g))))
        args.append(auxres)
    out_specs = [pl.BlockSpec((bm, bn), lambda g, i: (row_blk(g, i), col_blk(g)))]
    out_shape = [jax.ShapeDtypeStruct((m, n_out), out_dtype)]
    if has_aux:
        out_specs.append(pl.BlockSpec((ma, bn), lambda g, i: (0, col_blk(g))))
        out_shape.append(jax.ShapeDtypeStruct((ma, n_out), out_dtype))
    side_rows = side_bytes = 0
    if has_side:
        side_w, side_out_rows = side
        side_rows, side_cols = side_w.shape[1:]
        sr, sc = side_out_rows // n_tiles, side_cols // ni
        assert side_out_rows % n_tiles == 0 and side_cols % ni == 0 and sr % BF16_SUBLANES == 0 and sc % V7X_LANES == 0
        in_specs.append(pl.BlockSpec((None, sr, sc), lambda g, i: (layer, last_tile(g), i)))
        args.append(side_w)
        out_specs.append(pl.BlockSpec((sr, sc), lambda g, i: (g, i)))
        out_shape.append(jax.ShapeDtypeStruct((side_out_rows + sr, side_cols), BF16))
        side_bytes = _nbytes((sr, sc), F32) + _nbytes((sr, sc), BF16)
    win = (_nbytes((bm + ma, k), BF16) + len(ws) * _nbytes((rows, bn), F32) + side_bytes
           + _nbytes((bm + ma, bn), out_dtype) + len(res) * _nbytes((bm + ma, bn), F32))
    scratch = _nbytes((2, len(ws), k, bn), BF16)
    return pl.pallas_call(
        functools.partial(_wmm_kernel, n_w=len(ws), res_starts=res_starts, has_aux=has_aux, has_side=has_side,
                          act=act, n_tiles=n_tiles, rows=rows, src_cols=src_cols, side_rows=side_rows),
        grid=(n_tiles + 1, ni),
        in_specs=in_specs,
        out_specs=out_specs,
        out_shape=out_shape,
        scratch_shapes=[pltpu.VMEM((len(ws), k, bn), BF16), pltpu.VMEM((len(ws), k, bn), BF16)],
        compiler_params=_params(("arbitrary", "arbitrary"), win + scratch // 2),
        name="wmm",
    )(*args)


def _mm_res_kernel(a_ref, b_ref, r_ref, o_ref):
    @pl.when(pl.program_id(2) == 0)
    def _():
        o_ref[...] = r_ref[...]

    o_ref[...] += jnp.dot(a_ref[...], b_ref[...], preferred_element_type=F32)


def _matmul_res(a, b, res, row0=0, nrows=None):
    m, k = a.shape
    n = b.shape[1]
    nrows = m if nrows is None else nrows
    bm = _divisor_tile(nrows, 1024, BF16_SUBLANES)
    bn = _divisor_tile(n, 1024, V7X_LANES)
    bk = _divisor_tile(k, 4096, V7X_LANES)
    assert row0 % bm == 0
    off = row0 // bm
    win = _nbytes((bm, bk), BF16) + _nbytes((bk, bn), BF16) + 2 * _nbytes((bm, bn), F32)
    return pl.pallas_call(
        _mm_res_kernel,
        grid=(nrows // bm, n // bn, k // bk),
        in_specs=[pl.BlockSpec((bm, bk), lambda i, j, kk: (i + off, kk)),
                  pl.BlockSpec((bk, bn), lambda i, j, kk: (kk, j)),
                  pl.BlockSpec((bm, bn), lambda i, j, kk: (i + off, j))],
        out_specs=pl.BlockSpec((bm, bn), lambda i, j, kk: (i, j)),
        out_shape=jax.ShapeDtypeStruct((nrows, n), F32),
        compiler_params=_params(("parallel", "parallel", "arbitrary"), win),
        name="matmul_res",
    )(a, b, res)


def _t5_bucket(rel, n_buckets):
    half = n_buckets // 2
    exact = half // 2
    n = jnp.abs(rel)
    n_f = jnp.maximum(n, 1).astype(F32)
    large = exact + (jnp.log(n_f / exact) / math.log(MAX_DISTANCE / exact) * (half - exact)).astype(jnp.int32)
    large = jnp.minimum(large, half - 1)
    return jnp.where(rel > 0, half, 0) + jnp.where(n < exact, n, large)


def _bucket_tables(n_meta, n_buckets):
    qi = jnp.arange(BLOCK)[:, None]
    sj = jnp.arange(3 * BLOCK)[None, :]
    rel_band = sj - BLOCK - qi
    band = jnp.where(jnp.abs(rel_band) <= BLOCK, _t5_bucket(rel_band, n_buckets), -1)
    masked = jnp.full((BLOCK, BLOCK), -1, jnp.int32)
    sink = jnp.full((BLOCK, 1), n_buckets, jnp.int32)
    pad = jnp.full((BLOCK, BLOCK - n_meta - 1), -1, jnp.int32)
    mk = jnp.arange(n_meta)[None, :]
    meta_first = _t5_bucket(mk - (n_meta + qi), n_buckets)
    meta_far = jnp.full((BLOCK, n_meta), n_buckets // 2 - 1, jnp.int32)
    interior = jnp.concatenate([band, meta_far, sink, pad], axis=1)
    first = jnp.concatenate([masked, band[:, BLOCK:], meta_first, sink, pad], axis=1)
    last = jnp.concatenate([band[:, :2 * BLOCK], masked, meta_far, sink, pad], axis=1)
    tok = jnp.stack([interior, first, last]).astype(jnp.int32)

    mq = jnp.arange(n_meta)[:, None]
    tk = jnp.arange(BLOCK)[None, :]
    rel_tok = n_meta + tk - mq
    mband = jnp.where(jnp.abs(rel_tok) <= BLOCK, _t5_bucket(rel_tok, n_buckets), -1)
    mmeta = _t5_bucket(mk - mq, n_buckets)
    met = jnp.concatenate([mband, mmeta, sink[:n_meta], pad[:n_meta]], axis=1).astype(jnp.int32)
    return tok, met[None]


def _bias_kernel(tab_ref, bkt_ref, o_ref, *, n_ids):
    h = pl.program_id(0)
    bkt = bkt_ref[...]
    acc = jnp.full(bkt.shape, -jnp.inf, F32)
    for b in range(n_ids):
        acc = jnp.where(bkt == b, tab_ref[b, h], acc)
    o_ref[0, :, 0] = acc * LOG2E


def _bias_tiles(table, bkt):
    n_ids, n_heads = table.shape
    v, r, c = bkt.shape
    return pl.pallas_call(
        functools.partial(_bias_kernel, n_ids=n_ids),
        grid=(n_heads,),
        in_specs=[pl.BlockSpec(memory_space=pltpu.SMEM),
                  pl.BlockSpec((v, r, c), lambda h: (0, 0, 0))],
        out_specs=pl.BlockSpec((1, v, 1, r, c), lambda h: (h // GROUP, 0, h % GROUP, 0, 0)),
        out_shape=jax.ShapeDtypeStruct((n_heads // GROUP, v, GROUP, r, c), F32),
        compiler_params=_params(("parallel",), 2 * _nbytes(bkt.shape, F32)),
        name="bias_tiles",
    )(table, bkt)


def _softmax_pv(q, k, v, bias2):
    s = lax.dot_general(q, k, (((1,), (1,)), ((), ())), preferred_element_type=F32) + bias2
    m = jnp.max(s, axis=-1, keepdims=True)
    p = jnp.exp2(s - m)
    denom = jnp.sum(p, axis=-1, keepdims=True)
    return jnp.dot(p.astype(BF16), v, preferred_element_type=F32) * (1.0 / denom)


def _fill_meta_keys(km_scr, vm_scr, km_ref, vm_ref, kg, n_meta):
    km_scr[...] = jnp.zeros(km_scr.shape, BF16)
    vm_scr[...] = jnp.zeros(vm_scr.shape, BF16)
    km_scr[0:n_meta] = _rms(km_ref[...].astype(F32), kg).astype(BF16)
    vm_scr[0:n_meta] = vm_ref[...].astype(BF16)


def _attn_kernel(qg_ref, kg_ref, bias_ref, q_ref, kp_ref, kc_ref, kn_ref, vp_ref, vc_ref, vn_ref,
                 km_ref, vm_ref, o_ref, q_scr, k_scr, v_scr, km_scr, vm_scr, *, chunks, seq_chunks, n_meta, scale):
    r = pl.program_id(1)
    qg = qg_ref[...] * (scale * LOG2E)
    kg = kg_ref[...]
    body = chunks * BLOCK
    stack = GROUP * BLOCK

    k_scr[0:BLOCK] = _rms(kp_ref[...].astype(F32), kg).astype(BF16)
    k_scr[BLOCK:BLOCK + body] = _rms(kc_ref[...].astype(F32), kg).astype(BF16)
    k_scr[BLOCK + body:2 * BLOCK + body] = _rms(kn_ref[...].astype(F32), kg).astype(BF16)
    v_scr[0:BLOCK] = vp_ref[...].astype(BF16)
    v_scr[BLOCK:BLOCK + body] = vc_ref[...].astype(BF16)
    v_scr[BLOCK + body:2 * BLOCK + body] = vn_ref[...].astype(BF16)
    _fill_meta_keys(km_scr, vm_scr, km_ref, vm_ref, kg, n_meta)
    for g in range(GROUP):
        qn = _rms(q_ref[:, g * HEAD_DIM:(g + 1) * HEAD_DIM].astype(F32), qg).astype(BF16)
        for c in range(chunks):
            q_scr[(c * GROUP + g) * BLOCK:(c * GROUP + g + 1) * BLOCK] = qn[c * BLOCK:(c + 1) * BLOCK]
    k_meta = km_scr[...]
    v_meta = vm_scr[...]

    def chunk(cc, carry):
        gch = r * chunks + cc
        is_first = functools.reduce(jnp.logical_or, [gch == s0 for s0, _ in seq_chunks])
        is_last = functools.reduce(jnp.logical_or, [gch == s0 + n - 1 for s0, n in seq_chunks])
        var = jnp.where(is_first, 1, jnp.where(is_last, 2, 0))
        row0 = pl.multiple_of(cc * BLOCK, BLOCK)
        k_all = jnp.concatenate([k_scr[pl.ds(row0, 3 * BLOCK), :], k_meta], axis=0)
        v_all = jnp.concatenate([v_scr[pl.ds(row0, 3 * BLOCK), :], v_meta], axis=0)
        q = q_scr[pl.ds(pl.multiple_of(cc * stack, stack), stack), :]
        bias = bias_ref[0, var].reshape(stack, 4 * BLOCK)
        o = _softmax_pv(q, k_all, v_all, bias)
        for g in range(GROUP):
            o_ref[pl.ds(row0, BLOCK), g * HEAD_DIM:(g + 1) * HEAD_DIM] = o[g * BLOCK:(g + 1) * BLOCK]
        return carry

    lax.fori_loop(0, chunks, chunk, 0, unroll=True)


def _seq_of_chunk(c, seq_chunks):
    s = 0
    for s0, _ in seq_chunks[1:]:
        s = s + (c >= s0).astype(jnp.int32)
    return s


def _attention(z, zm, bias_tok, q_g, k_g, *, seq_chunks, n_meta, d_model, k_col, v_col):
    m = z.shape[0]
    n_chunks = m // BLOCK
    kv_heads = d_model // (GROUP * HEAD_DIM)
    chunks = math.gcd(8, *[n for _, n in seq_chunks])
    body = chunks * BLOCK
    qw = GROUP * HEAD_DIM
    kb, vb = k_col // HEAD_DIM, v_col // HEAD_DIM
    seq_of = functools.partial(_seq_of_chunk, seq_chunks=seq_chunks)

    def halo_prev(h, r):
        return jnp.maximum(r * chunks - 1, 0)

    def halo_next(h, r):
        return jnp.minimum(r * chunks + chunks, n_chunks - 1)

    in_specs = [
        pl.BlockSpec((1, HEAD_DIM), lambda h, r: (0, 0)),
        pl.BlockSpec((1, HEAD_DIM), lambda h, r: (0, 0)),
        pl.BlockSpec((1, 3, GROUP, BLOCK, 4 * BLOCK), lambda h, r: (h, 0, 0, 0, 0)),
        pl.BlockSpec((body, qw), lambda h, r: (r, h)),
        pl.BlockSpec((BLOCK, HEAD_DIM), lambda h, r: (halo_prev(h, r), kb + h)),
        pl.BlockSpec((body, HEAD_DIM), lambda h, r: (r, kb + h)),
        pl.BlockSpec((BLOCK, HEAD_DIM), lambda h, r: (halo_next(h, r), kb + h)),
        pl.BlockSpec((BLOCK, HEAD_DIM), lambda h, r: (halo_prev(h, r), vb + h)),
        pl.BlockSpec((body, HEAD_DIM), lambda h, r: (r, vb + h)),
        pl.BlockSpec((BLOCK, HEAD_DIM), lambda h, r: (halo_next(h, r), vb + h)),
        pl.BlockSpec((n_meta, HEAD_DIM), lambda h, r: (seq_of(r * chunks), kb + h)),
        pl.BlockSpec((n_meta, HEAD_DIM), lambda h, r: (seq_of(r * chunks), vb + h)),
    ]
    win = (_nbytes((3, GROUP, BLOCK, 4 * BLOCK), F32) + 2 * _nbytes((body, qw), F32)
           + 4 * _nbytes((body + 2 * BLOCK, HEAD_DIM), F32))
    return pl.pallas_call(
        functools.partial(_attn_kernel, chunks=chunks, seq_chunks=seq_chunks, n_meta=n_meta,
                          scale=HEAD_DIM ** -0.5),
        grid=(kv_heads, m // body),
        in_specs=in_specs,
        out_specs=pl.BlockSpec((body, qw), lambda h, r: (r, h)),
        out_shape=jax.ShapeDtypeStruct((m, d_model), F32),
        scratch_shapes=[pltpu.VMEM((body * GROUP, HEAD_DIM), BF16),
                        pltpu.VMEM((body + 2 * BLOCK, HEAD_DIM), BF16),
                        pltpu.VMEM((body + 2 * BLOCK, HEAD_DIM), BF16),
                        pltpu.VMEM((BLOCK, HEAD_DIM), BF16),
                        pltpu.VMEM((BLOCK, HEAD_DIM), BF16)],
        compiler_params=_params(("parallel", "arbitrary"), win),
        name="attention",
    )(q_g.reshape(1, HEAD_DIM), k_g.reshape(1, HEAD_DIM), bias_tok, z, z, z, z, z, z, z, zm, zm)


def _attn_meta_kernel(qg_ref, kg_ref, bias_ref, q_ref, k1_ref, v1_ref, km_ref, vm_ref, o_ref,
                      km_scr, vm_scr, *, n_meta, scale):
    qg = qg_ref[...] * (scale * LOG2E)
    kg = kg_ref[...]
    _fill_meta_keys(km_scr, vm_scr, km_ref, vm_ref, kg, n_meta)
    k_all = jnp.concatenate([_rms(k1_ref[...].astype(F32), kg).astype(BF16), km_scr[...]], axis=0)
    v_all = jnp.concatenate([v1_ref[...].astype(BF16), vm_scr[...]], axis=0)
    q = q_ref[...].astype(F32)
    qn = jnp.concatenate([_rms(q[:, g * HEAD_DIM:(g + 1) * HEAD_DIM], qg).astype(BF16) for g in range(GROUP)], axis=0)
    o = _softmax_pv(qn, k_all, v_all, bias_ref[0, 0].reshape(GROUP * n_meta, 2 * BLOCK))
    for g in range(GROUP):
        o_ref[:, g * HEAD_DIM:(g + 1) * HEAD_DIM] = o[g * n_meta:(g + 1) * n_meta]


def _attention_meta(z, zm, bias_met, q_g, k_g, *, seq_chunks, n_meta, d_model, k_col, v_col):
    n_seq = len(seq_chunks)
    kv_heads = d_model // (GROUP * HEAD_DIM)
    qw = GROUP * HEAD_DIM
    kb, vb = k_col // HEAD_DIM, v_col // HEAD_DIM

    def first_chunk(s):
        c = 0
        for i, (s0, _) in enumerate(seq_chunks):
            c = c + jnp.where(s == i, s0, 0)
        return c

    in_specs = [
        pl.BlockSpec((1, HEAD_DIM), lambda s, h: (0, 0)),
        pl.BlockSpec((1, HEAD_DIM), lambda s, h: (0, 0)),
        pl.BlockSpec((1, 1, GROUP, n_meta, 2 * BLOCK), lambda s, h: (h, 0, 0, 0, 0)),
        pl.BlockSpec((n_meta, qw), lambda s, h: (s, h)),
        pl.BlockSpec((BLOCK, HEAD_DIM), lambda s, h: (first_chunk(s), kb + h)),
        pl.BlockSpec((BLOCK, HEAD_DIM), lambda s, h: (first_chunk(s), vb + h)),
        pl.BlockSpec((n_meta, HEAD_DIM), lambda s, h: (s, kb + h)),
        pl.BlockSpec((n_meta, HEAD_DIM), lambda s, h: (s, vb + h)),
    ]
    return pl.pallas_call(
        functools.partial(_attn_meta_kernel, n_meta=n_meta, scale=HEAD_DIM ** -0.5),
        grid=(n_seq, kv_heads),
        in_specs=in_specs,
        out_specs=pl.BlockSpec((n_meta, qw), lambda s, h: (s, h)),
        out_shape=jax.ShapeDtypeStruct((n_seq * n_meta, d_model), F32),
        scratch_shapes=[pltpu.VMEM((BLOCK, HEAD_DIM), BF16), pltpu.VMEM((BLOCK, HEAD_DIM), BF16)],
        compiler_params=_params(("parallel", "parallel"), 1 << 20),
        name="attention_meta",
    )(q_g.reshape(1, HEAD_DIM), k_g.reshape(1, HEAD_DIM), bias_met, zm, z, z, zm, zm)


def _merge_math(attn, gb, gc, hc, ga, gcv, u_prev, u_next, cw, na, nc):
    u = gc * hc
    rows = u.shape[0]
    ridx = lax.broadcasted_iota(jnp.int32, u.shape, 0)
    up = jnp.where(ridx == 0, u_prev, pltpu.roll(u, 1, axis=0))
    un = jnp.where(ridx == rows - 1, u_next, pltpu.roll(u, rows - 1, axis=0))
    conv = gb * (cw[0:1] * up + cw[1:2] * u + cw[2:3] * un)
    return jax.nn.sigmoid(ga) * _rms(attn, na) + jax.nn.sigmoid(gcv) * _rms(conv, nc)


def _merge_kernel(attn_ref, gb_ref, gc_ref, hc_ref, ga_ref, gcv_ref, gcp_ref, hcp_ref, gcn_ref, hcn_ref,
                  gcm_ref, hcm_ref, cw_ref, na_ref, nc_ref, o_ref, conv_scr, *, seq_chunks):
    c = pl.program_id(0)
    is_first = functools.reduce(jnp.logical_or, [c == s0 for s0, _ in seq_chunks])
    is_last = functools.reduce(jnp.logical_or, [c == s0 + n - 1 for s0, n in seq_chunks])
    last = HALO_ROWS - 1
    rows, d = o_ref.shape
    n_tiles = d // V7X_LANES
    ridx = lax.broadcasted_iota(jnp.int32, (rows, V7X_LANES), 0)

    def cols(ct):
        return pl.ds(pl.multiple_of(ct * V7X_LANES, V7X_LANES), V7X_LANES)

    def f32(ref, sl):
        return ref[:, sl].astype(F32)

    def conv_pass(ct, carry):
        ss_attn, ss_conv = carry
        sl = cols(ct)
        u = f32(gc_ref, sl) * f32(hc_ref, sl)
        u_prev_tok = (f32(gcp_ref, sl) * f32(hcp_ref, sl))[last:last + 1]
        u_prev_meta = (f32(gcm_ref, sl) * f32(hcm_ref, sl))[last:last + 1]
        u_prev = jnp.where(is_first, u_prev_meta, u_prev_tok)
        u_next_tok = (f32(gcn_ref, sl) * f32(hcn_ref, sl))[0:1]
        u_next = jnp.where(is_last, jnp.zeros_like(u_next_tok), u_next_tok)
        up = jnp.where(ridx == 0, u_prev, pltpu.roll(u, 1, axis=0))
        un = jnp.where(ridx == rows - 1, u_next, pltpu.roll(u, rows - 1, axis=0))
        cw = cw_ref[:, sl]
        conv = f32(gb_ref, sl) * (cw[0:1] * up + cw[1:2] * u + cw[2:3] * un)
        conv_scr[:, sl] = conv
        attn = attn_ref[:, sl]
        return ss_attn + attn * attn, ss_conv + conv * conv

    zero = jnp.zeros((rows, V7X_LANES), F32)
    ss_attn, ss_conv = lax.fori_loop(0, n_tiles, conv_pass, (zero, zero))
    r_attn = lax.rsqrt(jnp.sum(ss_attn, axis=-1, keepdims=True) / d + EPS)
    r_conv = lax.rsqrt(jnp.sum(ss_conv, axis=-1, keepdims=True) / d + EPS)

    def gate_pass(ct, carry):
        sl = cols(ct)
        a = attn_ref[:, sl] * r_attn * na_ref[:, sl]
        c = conv_scr[:, sl] * r_conv * nc_ref[:, sl]
        o_ref[:, sl] = (jax.nn.sigmoid(f32(ga_ref, sl)) * a + jax.nn.sigmoid(f32(gcv_ref, sl)) * c).astype(o_ref.dtype)
        return carry

    lax.fori_loop(0, n_tiles, gate_pass, 0)


def _merge(attn, z, zm, conv_w, norm_a, norm_c, *, seq_chunks, n_meta, cols):
    m, d = attn.shape
    n_chunks = m // BLOCK
    per_chunk = BLOCK // HALO_ROWS
    per_meta = n_meta // HALO_ROWS
    gb, gc, hc, ga, gcv = [c // d for c in cols]
    seq_of = functools.partial(_seq_of_chunk, seq_chunks=seq_chunks)

    def main(col):
        return pl.BlockSpec((BLOCK, d), lambda c: (c, col))

    def prev_rows(col):
        return pl.BlockSpec((HALO_ROWS, d), lambda c: (jnp.maximum(c * per_chunk - 1, 0), col))

    def next_rows(col):
        return pl.BlockSpec((HALO_ROWS, d), lambda c: (jnp.minimum(c + 1, n_chunks - 1) * per_chunk, col))

    def meta_rows(col):
        return pl.BlockSpec((HALO_ROWS, d), lambda c: (seq_of(c) * per_meta + per_meta - 1, col))

    def row(nrows):
        return pl.BlockSpec((nrows, d), lambda c: (0, 0))

    win = 6 * _nbytes((BLOCK, d), F32) + _nbytes((BLOCK, d), BF16) + 8 * _nbytes((HALO_ROWS, d), F32)
    return pl.pallas_call(
        functools.partial(_merge_kernel, seq_chunks=seq_chunks),
        grid=(n_chunks,),
        in_specs=[main(0), main(gb), main(gc), main(hc), main(ga), main(gcv),
                  prev_rows(gc), prev_rows(hc), next_rows(gc), next_rows(hc), meta_rows(gc), meta_rows(hc),
                  row(3), row(1), row(1)],
        out_specs=pl.BlockSpec((BLOCK, d), lambda c: (c, 0)),
        out_shape=jax.ShapeDtypeStruct((m, d), BF16),
        scratch_shapes=[pltpu.VMEM((BLOCK, d), F32)],
        compiler_params=_params(("parallel",), win),
        name="merge",
    )(attn, z, z, z, z, z, z, z, z, z, zm, zm, conv_w, norm_a.reshape(1, d), norm_c.reshape(1, d))


def _merge_meta_kernel(attn_ref, gb_ref, gc_ref, hc_ref, ga_ref, gcv_ref, gcn_ref, hcn_ref,
                       cw_ref, na_ref, nc_ref, o_ref):
    f32 = lambda ref: ref[...].astype(F32)
    u_next = (f32(gcn_ref) * f32(hcn_ref))[0:1]
    u_prev = jnp.zeros_like(u_next)
    o_ref[...] = _merge_math(attn_ref[...], f32(gb_ref), f32(gc_ref), f32(hc_ref), f32(ga_ref), f32(gcv_ref),
                             u_prev, u_next, cw_ref[...], na_ref[...], nc_ref[...]).astype(o_ref.dtype)


def _merge_meta(attn_m, z, zm, conv_w, norm_a, norm_c, *, seq_chunks, n_meta, cols):
    mm, d = attn_m.shape
    per_chunk = BLOCK // HALO_ROWS
    gb, gc, hc, ga, gcv = [c // d for c in cols]

    def first_rows(s):
        r = 0
        for i, (s0, _) in enumerate(seq_chunks):
            r = r + jnp.where(s == i, s0 * per_chunk, 0)
        return r

    def main(col):
        return pl.BlockSpec((n_meta, d), lambda s: (s, col))

    def next_rows(col):
        return pl.BlockSpec((HALO_ROWS, d), lambda s: (first_rows(s), col))

    def row(nrows):
        return pl.BlockSpec((nrows, d), lambda s: (0, 0))

    win = 7 * _nbytes((n_meta, d), F32) + 2 * _nbytes((HALO_ROWS, d), F32)
    return pl.pallas_call(
        _merge_meta_kernel,
        grid=(len(seq_chunks),),
        in_specs=[main(0), main(gb), main(gc), main(hc), main(ga), main(gcv), next_rows(gc), next_rows(hc),
                  row(3), row(1), row(1)],
        out_specs=pl.BlockSpec((n_meta, d), lambda s: (s, 0)),
        out_shape=jax.ShapeDtypeStruct((mm, d), BF16),
        compiler_params=_params(("parallel",), win),
        name="merge_meta",
    )(attn_m, zm, zm, zm, zm, zm, z, z, conv_w, norm_a.reshape(1, d), norm_c.reshape(1, d))


def kernel(x_prompt, x_sample, meta_tokens, rel_bias, norm1_g, w_in, q_norm_g, k_norm_g, attn_sink, conv_w,
           branch_norm_a, branch_norm_c, w_out, norm2_g, w_ffn_gate, w_ffn_up, w_ffn_down):
    d = x_prompt.shape[-1]
    n_meta = meta_tokens.shape[0]
    n_buckets = rel_bias.shape[0]
    depth, _, in_dim = w_in.shape
    ffn = w_ffn_gate.shape[-1]
    kv_dim = (in_dim - 6 * d) // 2
    assert d % (GROUP * HEAD_DIM) == 0 and kv_dim == d // GROUP and attn_sink.shape[1] * HEAD_DIM == d
    assert n_meta % HALO_ROWS == 0 and n_meta < BLOCK

    groups = (x_prompt, x_sample)
    seq_chunks = []
    for xg in groups:
        assert xg.shape[1] % BLOCK == 0 and xg.shape[1] >= 2 * BLOCK
        for _ in range(xg.shape[0]):
            start = seq_chunks[-1][0] + seq_chunks[-1][1] if seq_chunks else 0
            seq_chunks.append((start, xg.shape[1] // BLOCK))
    seq_chunks = tuple(seq_chunks)
    n_seq = len(seq_chunks)

    x_parts = [xg.reshape(-1, d) for xg in groups]
    xm = jnp.tile(meta_tokens.astype(F32), (n_seq, 1))

    cols = tuple(d * i for i in range(1, 6))
    k_col, v_col = 6 * d, 6 * d + kv_dim
    ffn_tile = 512
    ffn_p = -(-ffn // ffn_tile) * ffn_tile
    out_tile = _divisor_tile(d, 512, V7X_LANES)
    n_q, n_rest = d // kv_dim, 5 * d // kv_dim

    def regroup(j):
        return jnp.where(j < n_q, j, jnp.where(j < n_q + n_rest, j + 2, j - n_rest))

    bkt_tok, bkt_met = _bucket_tables(n_meta, n_buckets)

    geo = dict(seq_chunks=seq_chunks, n_meta=n_meta)
    for l in range(depth):
        with_meta = l < depth - 1
        table = jnp.concatenate([rel_bias, attn_sink[l][None]], axis=0)
        z, zm = _wmm(_rmsnorm(x_parts, norm1_g[l]), _rmsnorm([xm], norm1_g[l]), [w_in], l, in_dim, kv_dim, BF16,
                     src_block=regroup)
        attn = _attention(z, zm, _bias_tiles(table, bkt_tok), q_norm_g[l], k_norm_g[l],
                          d_model=d, k_col=k_col, v_col=v_col, **geo)
        merged = _merge(attn, z, zm, conv_w[l], branch_norm_a[l], branch_norm_c[l], cols=cols, **geo)
        merged_m = hn_m = None
        if with_meta:
            attn_m = _attention_meta(z, zm, _bias_tiles(table, bkt_met), q_norm_g[l], k_norm_g[l],
                                     d_model=d, k_col=k_col, v_col=v_col, **geo)
            merged_m = _merge_meta(attn_m, z, zm, conv_w[l], branch_norm_a[l], branch_norm_c[l], cols=cols, **geo)
        h, *hm = _wmm(merged, merged_m, [w_out], l, d, out_tile, F32, res=x_parts,
                      auxres=xm if with_meta else None)
        if with_meta:
            hn_m = _rmsnorm([hm[0]], norm2_g[l])
        t, *tm, wd_b = _wmm(_rmsnorm([h], norm2_g[l]), hn_m, [w_ffn_gate, w_ffn_up], l, ffn_p, ffn_tile, BF16,
                            act=True, side=(w_ffn_down, ffn_p))
        if with_meta:
            x_parts = [_matmul_res(t, wd_b, h)]
            xm = _matmul_res(tm[0], wd_b, hm[0])

    outs = []
    row = 0
    for xg in groups:
        nrows = xg.shape[0] * xg.shape[1]
        outs.append(_matmul_res(t, wd_b, h, row, nrows).reshape(xg.shape))
        row += nrows
    return tuple(outs)
```

```python
import functools
import math

import jax
import jax.numpy as jnp
from jax import lax
from jax.experimental import pallas as pl
from jax.experimental.pallas import tpu as pltpu

HEAD_DIM = 128
GROUP = 4
BLOCK = 128
MAX_DISTANCE = 128
EPS = 1e-6
LOG2E = math.log2(math.e)

V7X_VMEM_BYTES = 64 << 20
V7X_LANES = 128
BF16_SUBLANES = 16
F32_SUBLANES = 8
HALO_ROWS = BF16_SUBLANES
VMEM_SLACK_BYTES = 10 << 20

F32 = jnp.float32
BF16 = jnp.bfloat16


def _params(dims, window_bytes):
    limit = min(2 * window_bytes + VMEM_SLACK_BYTES, V7X_VMEM_BYTES - (6 << 20))
    return pltpu.CompilerParams(dimension_semantics=dims, vmem_limit_bytes=int(limit))


def _divisor_tile(n, cap, unit):
    if n <= cap:
        return n
    t = (cap // unit) * unit
    while t >= unit:
        if n % t == 0:
            return t
        t -= unit
    raise ValueError(f"no tile for {n}")


def _nbytes(shape, dtype):
    return math.prod(shape) * jnp.dtype(dtype).itemsize


def _rms(x, g):
    ms = jnp.mean(x * x, axis=-1, keepdims=True)
    return x * lax.rsqrt(ms + EPS) * g


def _row_sumsq(x):
    return jnp.broadcast_to(jnp.sum(x * x, axis=-1, keepdims=True), (x.shape[0], V7X_LANES))


def _rstd(ss_ref, width):
    ss = ss_ref[0]
    for p in range(1, ss_ref.shape[0]):
        ss = ss + ss_ref[p]
    return lax.rsqrt(ss / width + EPS)


def _prep_kernel(*refs, starts):
    x_refs, o_ref, ss_ref = refs[:len(starts)], refs[-2], refs[-1]
    i = pl.program_id(0)
    ends = starts[1:] + (None,)
    for x_ref, lo, hi in zip(x_refs, starts, ends):
        in_seg = i >= lo if hi is None else jnp.logical_and(i >= lo, i < hi)

        @pl.when(in_seg)
        def _():
            x = x_ref[...]
            o_ref[...] = x.astype(o_ref.dtype)
            ss_ref[0] = _row_sumsq(x)


def _prep(xs):
    d = xs[0].shape[1]
    m = sum(x.shape[0] for x in xs)
    bm = _divisor_tile(math.gcd(*[x.shape[0] for x in xs]), 256, BF16_SUBLANES)
    blocks = [x.shape[0] // bm for x in xs]
    starts = tuple(sum(blocks[:s]) for s in range(len(xs)))
    in_specs = [pl.BlockSpec((bm, d), lambda i, start=start, nblk=nblk: (jnp.clip(i - start, 0, nblk - 1), 0))
                for start, nblk in zip(starts, blocks)]
    return pl.pallas_call(
        functools.partial(_prep_kernel, starts=starts),
        grid=(m // bm,),
        in_specs=in_specs,
        out_specs=[pl.BlockSpec((bm, d), lambda i: (i, 0)), pl.BlockSpec((1, bm, V7X_LANES), lambda i: (0, i, 0))],
        out_shape=[jax.ShapeDtypeStruct((m, d), BF16), jax.ShapeDtypeStruct((1, m, V7X_LANES), F32)],
        compiler_params=_params(("parallel",), len(xs) * _nbytes((bm, d), F32) + _nbytes((bm, d), BF16)),
        name="prep",
    )(*xs)


def _wmm_kernel(*refs, n_w, res_starts, has_aux, has_side, has_norm, emit_stats, act, n_tiles, rows, src_cols,
                side_rows):
    it = iter(refs)
    a_ref = next(it)
    aux_ref = next(it) if has_aux else None
    w_refs = [next(it) for _ in range(n_w)]
    gain_ref = next(it) if has_norm else None
    ss_ref = next(it) if has_norm else None
    auxss_ref = next(it) if has_norm and has_aux else None
    res_refs = [next(it) for _ in res_starts]
    auxres_ref = next(it) if res_starts and has_aux else None
    side_ref = next(it) if has_side else None
    o_ref = next(it)
    ob_ref, oss_ref = (next(it), next(it)) if emit_stats else (None, None)
    oaux_ref = next(it) if has_aux else None
    oauxb_ref, oauxss_ref = (next(it), next(it)) if emit_stats and has_aux else (None, None)
    oside_ref = next(it) if has_side else None
    w_bufs = (next(it), next(it))
    k_dim = w_bufs[0].shape[1]
    g = pl.program_id(0)
    i = pl.program_id(1)
    bn = w_bufs[0].shape[-1]

    if has_side:
        w = side_ref[...]
        row = g * w.shape[0] + lax.broadcasted_iota(jnp.int32, w.shape, 0)
        oside_ref[...] = jnp.where(row < side_rows, w, 0.0).astype(oside_ref.dtype)

    def stage(buf):
        row0 = pl.multiple_of(i * rows, rows)
        for t in range(n_w):
            w = w_refs[t][...]
            if has_norm:
                w = w * jnp.tile(gain_ref[...], (1, bn // V7X_LANES))
            if src_cols % bn:
                col = jnp.minimum(g, n_tiles - 1) * bn + lax.broadcasted_iota(jnp.int32, w.shape, 1)
                w = jnp.where(col < src_cols, w, 0.0)
            buf[t, pl.ds(row0, rows), :] = w.astype(buf.dtype)

    def apply(buf, a, ss, res, out, out_b, out_ss):
        ys = [jnp.dot(a, buf[t], preferred_element_type=F32) for t in range(n_w)]
        if has_norm:
            rstd = jnp.tile(_rstd(ss, k_dim), (1, bn // V7X_LANES))
            ys = [y * rstd for y in ys]
        y = ys[0] * jax.nn.sigmoid(ys[0]) * ys[1] if act else ys[0]
        y = y if res is None else res + y
        out[...] = y.astype(out.dtype)
        if emit_stats:
            out_b[...] = y.astype(out_b.dtype)
            out_ss[0] = _row_sumsq(y)

    def compute(buf):
        res = None
        if res_starts:
            res = res_refs[0][...]
            for start, ref in zip(res_starts[1:], res_refs[1:]):
                res = jnp.where(i >= start, ref[...], res)
        apply(buf, a_ref[...], ss_ref, res, o_ref, ob_ref, oss_ref)
        if has_aux:
            @pl.when(i == 0)
            def _():
                apply(buf, aux_ref[...], auxss_ref, auxres_ref[...] if res_starts else None,
                      oaux_ref, oauxb_ref, oauxss_ref)

    @pl.when(g == 0)
    def _():
        stage(w_bufs[0])

    for parity in range(2):
        @pl.when(jnp.logical_and(g > 0, g % 2 == parity))
        def _():
            stage(w_bufs[parity])
            compute(w_bufs[1 - parity])


def _wmm(a, aux, ws, layer, n_out, bn, out_dtype, *, src_block=None, act=False, res=None, auxres=None, side=None,
         norm=None, emit_stats=False):
    m, k = a.shape
    src_cols = ws[0].shape[2]
    bm = _divisor_tile(m, 1024, BF16_SUBLANES)
    ni, n_tiles = m // bm, n_out // bn
    rows = k // ni
    assert k % ni == 0 and rows % BF16_SUBLANES == 0 and n_out % bn == 0
    assert src_block is None or src_cols % bn == 0
    src = src_block if src_block is not None else (lambda j: j)
    res = list(res) if res is not None else []
    has_aux, has_side, has_norm = aux is not None, side is not None, norm is not None
    ma = aux.shape[0] if has_aux else 0
    res_blocks = [r.shape[0] // bm for r in res]
    assert all(r.shape[0] % bm == 0 for r in res) and sum(res_blocks) == (ni if res else 0)
    res_starts = tuple(sum(res_blocks[:s]) for s in range(len(res)))

    def row_blk(g, i):
        return jnp.where(g == 0, 0, i)

    def col_blk(g):
        return jnp.maximum(g - 1, 0)

    def last_tile(g):
        return jnp.minimum(g, n_tiles - 1)

    in_specs = [pl.BlockSpec((bm, k), lambda g, i: (row_blk(g, i), 0))]
    args = [a]
    if has_aux:
        in_specs.append(pl.BlockSpec((ma, k), lambda g, i: (0, 0)))
        args.append(aux)
    for w in ws:
        in_specs.append(pl.BlockSpec((None, rows, bn), lambda g, i: (layer, i, src(last_tile(g)))))
        args.append(w)
    norm_bytes = 0
    if has_norm:
        gain, ss, ss_aux = norm
        in_specs.append(pl.BlockSpec((rows, V7X_LANES), lambda g, i: (i, 0)))
        args.append(jnp.broadcast_to(gain[:, None], (k, V7X_LANES)))
        in_specs.append(pl.BlockSpec((ss.shape[0], bm, V7X_LANES), lambda g, i: (0, row_blk(g, i), 0)))
        args.append(ss)
        if has_aux:
            in_specs.append(pl.BlockSpec((ss_aux.shape[0], ma, V7X_LANES), lambda g, i: (0, 0, 0)))
            args.append(ss_aux)
        norm_bytes = _nbytes((rows + ss.shape[0] * (bm + ma), V7X_LANES), F32)
    for start, nblk, r in zip(res_starts, res_blocks, res):
        in_specs.append(pl.BlockSpec(
            (bm, bn), lambda g, i, start=start, nblk=nblk: (jnp.clip(row_blk(g, i) - start, 0, nblk - 1), col_blk(g))))
        args.append(r)
    if res and has_aux:
        in_specs.append(pl.BlockSpec((ma, bn), lambda g, i: (0, col_blk(g))))
        args.append(auxres)
    out_specs, out_shape = [], []

    def add_outputs(nrows, blk_rows, row_index):
        out_specs.append(pl.BlockSpec((blk_rows, bn), lambda g, i: (row_index(g, i), col_blk(g))))
        out_shape.append(jax.ShapeDtypeStruct((nrows, n_out), out_dtype))
        if emit_stats:
            out_specs.append(pl.BlockSpec((blk_rows, bn), lambda g, i: (row_index(g, i), col_blk(g))))
            out_shape.append(jax.ShapeDtypeStruct((nrows, n_out), BF16))
            out_specs.append(pl.BlockSpec((1, blk_rows, V7X_LANES), lambda g, i: (col_blk(g), row_index(g, i), 0)))
            out_shape.append(jax.ShapeDtypeStruct((n_tiles, nrows, V7X_LANES), F32))

    add_outputs(m, bm, row_blk)
    if has_aux:
        add_outputs(ma, ma, lambda g, i: 0)
    side_rows = side_bytes = 0
    if has_side:
        side_w, side_out_rows = side
        side_rows, side_cols = side_w.shape[1:]
        sr, sc = side_out_rows // n_tiles, side_cols // ni
        assert side_out_rows % n_tiles == 0 and side_cols % ni == 0 and sr % BF16_SUBLANES == 0 and sc % V7X_LANES == 0
        in_specs.append(pl.BlockSpec((None, sr, sc), lambda g, i: (layer, last_tile(g), i)))
        args.append(side_w)
        out_specs.append(pl.BlockSpec((sr, sc), lambda g, i: (g, i)))
        out_shape.append(jax.ShapeDtypeStruct((side_out_rows + sr, side_cols), BF16))
        side_bytes = _nbytes((sr, sc), F32) + _nbytes((sr, sc), BF16)
    win = (_nbytes((bm + ma, k), BF16) + len(ws) * _nbytes((rows, bn), F32) + side_bytes + norm_bytes
           + _nbytes((bm + ma, bn), out_dtype) + len(res) * _nbytes((bm + ma, bn), F32)
           + emit_stats * (_nbytes((bm + ma, bn), BF16) + _nbytes((bm + ma, V7X_LANES), F32)))
    scratch = _nbytes((2, len(ws), k, bn), BF16)
    return pl.pallas_call(
        functools.partial(_wmm_kernel, n_w=len(ws), res_starts=res_starts, has_aux=has_aux, has_side=has_side,
                          has_norm=has_norm, emit_stats=emit_stats, act=act, n_tiles=n_tiles, rows=rows,
                          src_cols=src_cols, side_rows=side_rows),
        grid=(n_tiles + 1, ni),
        in_specs=in_specs,
        out_specs=out_specs,
        out_shape=out_shape,
        scratch_shapes=[pltpu.VMEM((len(ws), k, bn), BF16), pltpu.VMEM((len(ws), k, bn), BF16)],
        compiler_params=_params(("arbitrary", "arbitrary"), win + scratch // 2),
        name="wmm",
    )(*args)


def _mm_res_kernel(a_ref, b_ref, r_ref, o_ref, *stats):
    j, kk = pl.program_id(1), pl.program_id(2)

    @pl.when(kk == 0)
    def _():
        o_ref[...] = r_ref[...]

    o_ref[...] += jnp.dot(a_ref[...], b_ref[...], preferred_element_type=F32)
    if stats:
        ob_ref, ss_ref = stats

        @pl.when(kk == pl.num_programs(2) - 1)
        def _():
            y = o_ref[...]
            ob_ref[...] = y.astype(ob_ref.dtype)

            @pl.when(j == 0)
            def _():
                ss_ref[0] = _row_sumsq(y)

            @pl.when(j > 0)
            def _():
                ss_ref[0] += _row_sumsq(y)


def _matmul_res(a, b, res, row0=0, nrows=None, emit_stats=False):
    m, k = a.shape
    n = b.shape[1]
    nrows = m if nrows is None else nrows
    bm = _divisor_tile(nrows, 1024, BF16_SUBLANES)
    bn = _divisor_tile(n, 1024, V7X_LANES)
    bk = _divisor_tile(k, 4096, V7X_LANES)
    assert row0 % bm == 0
    off = row0 // bm
    win = _nbytes((bm, bk), BF16) + _nbytes((bk, bn), BF16) + 2 * _nbytes((bm, bn), F32)
    return pl.pallas_call(
        _mm_res_kernel,
        grid=(nrows // bm, n // bn, k // bk),
        in_specs=[pl.BlockSpec((bm, bk), lambda i, j, kk: (i + off, kk)),
                  pl.BlockSpec((bk, bn), lambda i, j, kk: (kk, j)),
                  pl.BlockSpec((bm, bn), lambda i, j, kk: (i + off, j))],
        out_specs=[pl.BlockSpec((bm, bn), lambda i, j, kk: (i, j))] + emit_stats * [
            pl.BlockSpec((bm, bn), lambda i, j, kk: (i, j)),
            pl.BlockSpec((1, bm, V7X_LANES), lambda i, j, kk: (0, i, 0))],
        out_shape=[jax.ShapeDtypeStruct((nrows, n), F32)] + emit_stats * [
            jax.ShapeDtypeStruct((nrows, n), BF16), jax.ShapeDtypeStruct((1, nrows, V7X_LANES), F32)],
        compiler_params=_params(("parallel", "arbitrary", "arbitrary"), win + _nbytes((bm, bn), BF16)),
        name="matmul_res",
    )(a, b, res)


def _t5_bucket(rel, n_buckets):
    half = n_buckets // 2
    exact = half // 2
    n = jnp.abs(rel)
    n_f = jnp.maximum(n, 1).astype(F32)
    large = exact + (jnp.log(n_f / exact) / math.log(MAX_DISTANCE / exact) * (half - exact)).astype(jnp.int32)
    large = jnp.minimum(large, half - 1)
    return jnp.where(rel > 0, half, 0) + jnp.where(n < exact, n, large)


def _bucket_tables(n_meta, n_buckets):
    qi = jnp.arange(BLOCK)[:, None]
    sj = jnp.arange(3 * BLOCK)[None, :]
    rel_band = sj - BLOCK - qi
    band = jnp.where(jnp.abs(rel_band) <= BLOCK, _t5_bucket(rel_band, n_buckets), -1)
    masked = jnp.full((BLOCK, BLOCK), -1, jnp.int32)
    sink = jnp.full((BLOCK, 1), n_buckets, jnp.int32)
    pad = jnp.full((BLOCK, BLOCK - n_meta - 1), -1, jnp.int32)
    mk = jnp.arange(n_meta)[None, :]
    meta_first = _t5_bucket(mk - (n_meta + qi), n_buckets)
    meta_far = jnp.full((BLOCK, n_meta), n_buckets // 2 - 1, jnp.int32)
    interior = jnp.concatenate([band, meta_far, sink, pad], axis=1)
    first = jnp.concatenate([masked, band[:, BLOCK:], meta_first, sink, pad], axis=1)
    last = jnp.concatenate([band[:, :2 * BLOCK], masked, meta_far, sink, pad], axis=1)
    tok = jnp.stack([interior, first, last]).astype(jnp.int32)

    mq = jnp.arange(n_meta)[:, None]
    tk = jnp.arange(BLOCK)[None, :]
    rel_tok = n_meta + tk - mq
    mband = jnp.where(jnp.abs(rel_tok) <= BLOCK, _t5_bucket(rel_tok, n_buckets), -1)
    mmeta = _t5_bucket(mk - mq, n_buckets)
    met = jnp.concatenate([mband, mmeta, sink[:n_meta], pad[:n_meta]], axis=1).astype(jnp.int32)
    return tok, met[None]


def _bias_kernel(tab_ref, bkt_ref, o_ref, *, n_ids):
    h = pl.program_id(0)
    bkt = bkt_ref[...]
    acc = jnp.full(bkt.shape, -jnp.inf, F32)
    for b in range(n_ids):
        acc = jnp.where(bkt == b, tab_ref[b, h], acc)
    o_ref[0, :, 0] = acc * LOG2E


def _bias_tiles(table, bkt):
    n_ids, n_heads = table.shape
    v, r, c = bkt.shape
    return pl.pallas_call(
        functools.partial(_bias_kernel, n_ids=n_ids),
        grid=(n_heads,),
        in_specs=[pl.BlockSpec(memory_space=pltpu.SMEM),
                  pl.BlockSpec((v, r, c), lambda h: (0, 0, 0))],
        out_specs=pl.BlockSpec((1, v, 1, r, c), lambda h: (h // GROUP, 0, h % GROUP, 0, 0)),
        out_shape=jax.ShapeDtypeStruct((n_heads // GROUP, v, GROUP, r, c), F32),
        compiler_params=_params(("parallel",), 2 * _nbytes(bkt.shape, F32)),
        name="bias_tiles",
    )(table, bkt)


def _softmax_pv(q, k, v, bias2):
    s = lax.dot_general(q, k, (((1,), (1,)), ((), ())), preferred_element_type=F32) + bias2
    m = jnp.max(s, axis=-1, keepdims=True)
    p = jnp.exp2(s - m)
    denom = jnp.sum(p, axis=-1, keepdims=True)
    return jnp.dot(p.astype(BF16), v, preferred_element_type=F32) * (1.0 / denom)


def _fill_meta_keys(km_scr, vm_scr, km_ref, vm_ref, kg, n_meta):
    km_scr[...] = jnp.zeros(km_scr.shape, BF16)
    vm_scr[...] = jnp.zeros(vm_scr.shape, BF16)
    km_scr[0:n_meta] = _rms(km_ref[...].astype(F32), kg).astype(BF16)
    vm_scr[0:n_meta] = vm_ref[...].astype(BF16)


def _attn_kernel(qg_ref, kg_ref, bias_ref, q_ref, kp_ref, kc_ref, kn_ref, vp_ref, vc_ref, vn_ref,
                 km_ref, vm_ref, o_ref, q_scr, k_scr, v_scr, km_scr, vm_scr, *, chunks, seq_chunks, n_meta, scale):
    r = pl.program_id(1)
    qg = qg_ref[...] * (scale * LOG2E)
    kg = kg_ref[...]
    body = chunks * BLOCK
    stack = GROUP * BLOCK

    k_scr[0:BLOCK] = _rms(kp_ref[...].astype(F32), kg).astype(BF16)
    k_scr[BLOCK:BLOCK + body] = _rms(kc_ref[...].astype(F32), kg).astype(BF16)
    k_scr[BLOCK + body:2 * BLOCK + body] = _rms(kn_ref[...].astype(F32), kg).astype(BF16)
    v_scr[0:BLOCK] = vp_ref[...].astype(BF16)
    v_scr[BLOCK:BLOCK + body] = vc_ref[...].astype(BF16)
    v_scr[BLOCK + body:2 * BLOCK + body] = vn_ref[...].astype(BF16)
    _fill_meta_keys(km_scr, vm_scr, km_ref, vm_ref, kg, n_meta)
    for g in range(GROUP):
        qn = _rms(q_ref[:, g * HEAD_DIM:(g + 1) * HEAD_DIM].astype(F32), qg).astype(BF16)
        for c in range(chunks):
            q_scr[(c * GROUP + g) * BLOCK:(c * GROUP + g + 1) * BLOCK] = qn[c * BLOCK:(c + 1) * BLOCK]
    k_meta = km_scr[...]
    v_meta = vm_scr[...]

    def chunk(cc, carry):
        gch = r * chunks + cc
        is_first = functools.reduce(jnp.logical_or, [gch == s0 for s0, _ in seq_chunks])
        is_last = functools.reduce(jnp.logical_or, [gch == s0 + n - 1 for s0, n in seq_chunks])
        var = jnp.where(is_first, 1, jnp.where(is_last, 2, 0))
        row0 = pl.multiple_of(cc * BLOCK, BLOCK)
        k_all = jnp.concatenate([k_scr[pl.ds(row0, 3 * BLOCK), :], k_meta], axis=0)
        v_all = jnp.concatenate([v_scr[pl.ds(row0, 3 * BLOCK), :], v_meta], axis=0)
        q = q_scr[pl.ds(pl.multiple_of(cc * stack, stack), stack), :]
        bias = bias_ref[0, var].reshape(stack, 4 * BLOCK)
        o = _softmax_pv(q, k_all, v_all, bias)
        for g in range(GROUP):
            o_ref[pl.ds(row0, BLOCK), g * HEAD_DIM:(g + 1) * HEAD_DIM] = o[g * BLOCK:(g + 1) * BLOCK]
        return carry

    lax.fori_loop(0, chunks, chunk, 0, unroll=True)


def _seq_of_chunk(c, seq_chunks):
    s = 0
    for s0, _ in seq_chunks[1:]:
        s = s + (c >= s0).astype(jnp.int32)
    return s


def _attention(z, zm, bias_tok, q_g, k_g, *, seq_chunks, n_meta, d_model, k_col, v_col):
    m = z.shape[0]
    n_chunks = m // BLOCK
    kv_heads = d_model // (GROUP * HEAD_DIM)
    chunks = math.gcd(8, *[n for _, n in seq_chunks])
    body = chunks * BLOCK
    qw = GROUP * HEAD_DIM
    kb, vb = k_col // HEAD_DIM, v_col // HEAD_DIM
    seq_of = functools.partial(_seq_of_chunk, seq_chunks=seq_chunks)

    def halo_prev(h, r):
        return jnp.maximum(r * chunks - 1, 0)

    def halo_next(h, r):
        return jnp.minimum(r * chunks + chunks, n_chunks - 1)

    in_specs = [
        pl.BlockSpec((1, HEAD_DIM), lambda h, r: (0, 0)),
        pl.BlockSpec((1, HEAD_DIM), lambda h, r: (0, 0)),
        pl.BlockSpec((1, 3, GROUP, BLOCK, 4 * BLOCK), lambda h, r: (h, 0, 0, 0, 0)),
        pl.BlockSpec((body, qw), lambda h, r: (r, h)),
        pl.BlockSpec((BLOCK, HEAD_DIM), lambda h, r: (halo_prev(h, r), kb + h)),
        pl.BlockSpec((body, HEAD_DIM), lambda h, r: (r, kb + h)),
        pl.BlockSpec((BLOCK, HEAD_DIM), lambda h, r: (halo_next(h, r), kb + h)),
        pl.BlockSpec((BLOCK, HEAD_DIM), lambda h, r: (halo_prev(h, r), vb + h)),
        pl.BlockSpec((body, HEAD_DIM), lambda h, r: (r, vb + h)),
        pl.BlockSpec((BLOCK, HEAD_DIM), lambda h, r: (halo_next(h, r), vb + h)),
        pl.BlockSpec((n_meta, HEAD_DIM), lambda h, r: (seq_of(r * chunks), kb + h)),
        pl.BlockSpec((n_meta, HEAD_DIM), lambda h, r: (seq_of(r * chunks), vb + h)),
    ]
    win = (_nbytes((3, GROUP, BLOCK, 4 * BLOCK), F32) + 2 * _nbytes((body, qw), F32)
           + 4 * _nbytes((body + 2 * BLOCK, HEAD_DIM), F32))
    return pl.pallas_call(
        functools.partial(_attn_kernel, chunks=chunks, seq_chunks=seq_chunks, n_meta=n_meta,
                          scale=HEAD_DIM ** -0.5),
        grid=(kv_heads, m // body),
        in_specs=in_specs,
        out_specs=pl.BlockSpec((body, qw), lambda h, r: (r, h)),
        out_shape=jax.ShapeDtypeStruct((m, d_model), F32),
        scratch_shapes=[pltpu.VMEM((body * GROUP, HEAD_DIM), BF16),
                        pltpu.VMEM((body + 2 * BLOCK, HEAD_DIM), BF16),
                        pltpu.VMEM((body + 2 * BLOCK, HEAD_DIM), BF16),
                        pltpu.VMEM((BLOCK, HEAD_DIM), BF16),
                        pltpu.VMEM((BLOCK, HEAD_DIM), BF16)],
        compiler_params=_params(("parallel", "arbitrary"), win),
        name="attention",
    )(q_g.reshape(1, HEAD_DIM), k_g.reshape(1, HEAD_DIM), bias_tok, z, z, z, z, z, z, z, zm, zm)


def _attn_meta_kernel(qg_ref, kg_ref, bias_ref, q_ref, k1_ref, v1_ref, km_ref, vm_ref, o_ref,
                      km_scr, vm_scr, *, n_meta, scale):
    qg = qg_ref[...] * (scale * LOG2E)
    kg = kg_ref[...]
    _fill_meta_keys(km_scr, vm_scr, km_ref, vm_ref, kg, n_meta)
    k_all = jnp.concatenate([_rms(k1_ref[...].astype(F32), kg).astype(BF16), km_scr[...]], axis=0)
    v_all = jnp.concatenate([v1_ref[...].astype(BF16), vm_scr[...]], axis=0)
    q = q_ref[...].astype(F32)
    qn = jnp.concatenate([_rms(q[:, g * HEAD_DIM:(g + 1) * HEAD_DIM], qg).astype(BF16) for g in range(GROUP)], axis=0)
    o = _softmax_pv(qn, k_all, v_all, bias_ref[0, 0].reshape(GROUP * n_meta, 2 * BLOCK))
    for g in range(GROUP):
        o_ref[:, g * HEAD_DIM:(g + 1) * HEAD_DIM] = o[g * n_meta:(g + 1) * n_meta]


def _attention_meta(z, zm, bias_met, q_g, k_g, *, seq_chunks, n_meta, d_model, k_col, v_col):
    n_seq = len(seq_chunks)
    kv_heads = d_model // (GROUP * HEAD_DIM)
    qw = GROUP * HEAD_DIM
    kb, vb = k_col // HEAD_DIM, v_col // HEAD_DIM

    def first_chunk(s):
        c = 0
        for i, (s0, _) in enumerate(seq_chunks):
            c = c + jnp.where(s == i, s0, 0)
        return c

    in_specs = [
        pl.BlockSpec((1, HEAD_DIM), lambda s, h: (0, 0)),
        pl.BlockSpec((1, HEAD_DIM), lambda s, h: (0, 0)),
        pl.BlockSpec((1, 1, GROUP, n_meta, 2 * BLOCK), lambda s, h: (h, 0, 0, 0, 0)),
        pl.BlockSpec((n_meta, qw), lambda s, h: (s, h)),
        pl.BlockSpec((BLOCK, HEAD_DIM), lambda s, h: (first_chunk(s), kb + h)),
        pl.BlockSpec((BLOCK, HEAD_DIM), lambda s, h: (first_chunk(s), vb + h)),
        pl.BlockSpec((n_meta, HEAD_DIM), lambda s, h: (s, kb + h)),
        pl.BlockSpec((n_meta, HEAD_DIM), lambda s, h: (s, vb + h)),
    ]
    return pl.pallas_call(
        functools.partial(_attn_meta_kernel, n_meta=n_meta, scale=HEAD_DIM ** -0.5),
        grid=(n_seq, kv_heads),
        in_specs=in_specs,
        out_specs=pl.BlockSpec((n_meta, qw), lambda s, h: (s, h)),
        out_shape=jax.ShapeDtypeStruct((n_seq * n_meta, d_model), F32),
        scratch_shapes=[pltpu.VMEM((BLOCK, HEAD_DIM), BF16), pltpu.VMEM((BLOCK, HEAD_DIM), BF16)],
        compiler_params=_params(("parallel", "parallel"), 1 << 20),
        name="attention_meta",
    )(q_g.reshape(1, HEAD_DIM), k_g.reshape(1, HEAD_DIM), bias_met, zm, z, z, zm, zm)


def _merge_math(attn, gb, gc, hc, ga, gcv, u_prev, u_next, cw, na, nc):
    u = gc * hc
    rows = u.shape[0]
    ridx = lax.broadcasted_iota(jnp.int32, u.shape, 0)
    up = jnp.where(ridx == 0, u_prev, pltpu.roll(u, 1, axis=0))
    un = jnp.where(ridx == rows - 1, u_next, pltpu.roll(u, rows - 1, axis=0))
    conv = gb * (cw[0:1] * up + cw[1:2] * u + cw[2:3] * un)
    return jax.nn.sigmoid(ga) * _rms(attn, na) + jax.nn.sigmoid(gcv) * _rms(conv, nc)


def _merge_kernel(attn_ref, gb_ref, gc_ref, hc_ref, ga_ref, gcv_ref, gcp_ref, hcp_ref, gcn_ref, hcn_ref,
                  gcm_ref, hcm_ref, cw_ref, na_ref, nc_ref, o_ref, conv_scr, *, seq_chunks):
    c = pl.program_id(0)
    is_first = functools.reduce(jnp.logical_or, [c == s0 for s0, _ in seq_chunks])
    is_last = functools.reduce(jnp.logical_or, [c == s0 + n - 1 for s0, n in seq_chunks])
    last = HALO_ROWS - 1
    rows, d = o_ref.shape
    n_tiles = d // V7X_LANES
    ridx = lax.broadcasted_iota(jnp.int32, (rows, V7X_LANES), 0)

    def cols(ct):
        return pl.ds(pl.multiple_of(ct * V7X_LANES, V7X_LANES), V7X_LANES)

    def f32(ref, sl):
        return ref[:, sl].astype(F32)

    def conv_pass(ct, carry):
        ss_attn, ss_conv = carry
        sl = cols(ct)
        u = f32(gc_ref, sl) * f32(hc_ref, sl)
        u_prev_tok = (f32(gcp_ref, sl) * f32(hcp_ref, sl))[last:last + 1]
        u_prev_meta = (f32(gcm_ref, sl) * f32(hcm_ref, sl))[last:last + 1]
        u_prev = jnp.where(is_first, u_prev_meta, u_prev_tok)
        u_next_tok = (f32(gcn_ref, sl) * f32(hcn_ref, sl))[0:1]
        u_next = jnp.where(is_last, jnp.zeros_like(u_next_tok), u_next_tok)
        up = jnp.where(ridx == 0, u_prev, pltpu.roll(u, 1, axis=0))
        un = jnp.where(ridx == rows - 1, u_next, pltpu.roll(u, rows - 1, axis=0))
        cw = cw_ref[:, sl]
        conv = f32(gb_ref, sl) * (cw[0:1] * up + cw[1:2] * u + cw[2:3] * un)
        conv_scr[:, sl] = conv
        attn = attn_ref[:, sl]
        return ss_attn + attn * attn, ss_conv + conv * conv

    zero = jnp.zeros((rows, V7X_LANES), F32)
    ss_attn, ss_conv = lax.fori_loop(0, n_tiles, conv_pass, (zero, zero))
    r_attn = lax.rsqrt(jnp.sum(ss_attn, axis=-1, keepdims=True) / d + EPS)
    r_conv = lax.rsqrt(jnp.sum(ss_conv, axis=-1, keepdims=True) / d + EPS)

    def gate_pass(ct, carry):
        sl = cols(ct)
        a = attn_ref[:, sl] * r_attn * na_ref[:, sl]
        c = conv_scr[:, sl] * r_conv * nc_ref[:, sl]
        o_ref[:, sl] = (jax.nn.sigmoid(f32(ga_ref, sl)) * a + jax.nn.sigmoid(f32(gcv_ref, sl)) * c).astype(o_ref.dtype)
        return carry

    lax.fori_loop(0, n_tiles, gate_pass, 0)


def _merge(attn, z, zm, conv_w, norm_a, norm_c, *, seq_chunks, n_meta, cols):
    m, d = attn.shape
    n_chunks = m // BLOCK
    per_chunk = BLOCK // HALO_ROWS
    per_meta = n_meta // HALO_ROWS
    gb, gc, hc, ga, gcv = [c // d for c in cols]
    seq_of = functools.partial(_seq_of_chunk, seq_chunks=seq_chunks)

    def main(col):
        return pl.BlockSpec((BLOCK, d), lambda c: (c, col))

    def prev_rows(col):
        return pl.BlockSpec((HALO_ROWS, d), lambda c: (jnp.maximum(c * per_chunk - 1, 0), col))

    def next_rows(col):
        return pl.BlockSpec((HALO_ROWS, d), lambda c: (jnp.minimum(c + 1, n_chunks - 1) * per_chunk, col))

    def meta_rows(col):
        return pl.BlockSpec((HALO_ROWS, d), lambda c: (seq_of(c) * per_meta + per_meta - 1, col))

    def row(nrows):
        return pl.BlockSpec((nrows, d), lambda c: (0, 0))

    win = 6 * _nbytes((BLOCK, d), F32) + _nbytes((BLOCK, d), BF16) + 8 * _nbytes((HALO_ROWS, d), F32)
    return pl.pallas_call(
        functools.partial(_merge_kernel, seq_chunks=seq_chunks),
        grid=(n_chunks,),
        in_specs=[main(0), main(gb), main(gc), main(hc), main(ga), main(gcv),
                  prev_rows(gc), prev_rows(hc), next_rows(gc), next_rows(hc), meta_rows(gc), meta_rows(hc),
                  row(3), row(1), row(1)],
        out_specs=pl.BlockSpec((BLOCK, d), lambda c: (c, 0)),
        out_shape=jax.ShapeDtypeStruct((m, d), BF16),
        scratch_shapes=[pltpu.VMEM((BLOCK, d), F32)],
        compiler_params=_params(("parallel",), win),
        name="merge",
    )(attn, z, z, z, z, z, z, z, z, z, zm, zm, conv_w, norm_a.reshape(1, d), norm_c.reshape(1, d))


def _merge_meta_kernel(attn_ref, gb_ref, gc_ref, hc_ref, ga_ref, gcv_ref, gcn_ref, hcn_ref,
                       cw_ref, na_ref, nc_ref, o_ref):
    f32 = lambda ref: ref[...].astype(F32)
    u_next = (f32(gcn_ref) * f32(hcn_ref))[0:1]
    u_prev = jnp.zeros_like(u_next)
    o_ref[...] = _merge_math(attn_ref[...], f32(gb_ref), f32(gc_ref), f32(hc_ref), f32(ga_ref), f32(gcv_ref),
                             u_prev, u_next, cw_ref[...], na_ref[...], nc_ref[...]).astype(o_ref.dtype)


def _merge_meta(attn_m, z, zm, conv_w, norm_a, norm_c, *, seq_chunks, n_meta, cols):
    mm, d = attn_m.shape
    per_chunk = BLOCK // HALO_ROWS
    gb, gc, hc, ga, gcv = [c // d for c in cols]

    def first_rows(s):
        r = 0
        for i, (s0, _) in enumerate(seq_chunks):
            r = r + jnp.where(s == i, s0 * per_chunk, 0)
        return r

    def main(col):
        return pl.BlockSpec((n_meta, d), lambda s: (s, col))

    def next_rows(col):
        return pl.BlockSpec((HALO_ROWS, d), lambda s: (first_rows(s), col))

    def row(nrows):
        return pl.BlockSpec((nrows, d), lambda s: (0, 0))

    win = 7 * _nbytes((n_meta, d), F32) + 2 * _nbytes((HALO_ROWS, d), F32)
    return pl.pallas_call(
        _merge_meta_kernel,
        grid=(len(seq_chunks),),
        in_specs=[main(0), main(gb), main(gc), main(hc), main(ga), main(gcv), next_rows(gc), next_rows(hc),
                  row(3), row(1), row(1)],
        out_specs=pl.BlockSpec((n_meta, d), lambda s: (s, 0)),
        out_shape=jax.ShapeDtypeStruct((mm, d), BF16),
        compiler_params=_params(("parallel",), win),
        name="merge_meta",
    )(attn_m, zm, zm, zm, zm, zm, z, z, conv_w, norm_a.reshape(1, d), norm_c.reshape(1, d))


def kernel(x_prompt, x_sample, meta_tokens, rel_bias, norm1_g, w_in, q_norm_g, k_norm_g, attn_sink, conv_w,
           branch_norm_a, branch_norm_c, w_out, norm2_g, w_ffn_gate, w_ffn_up, w_ffn_down):
    d = x_prompt.shape[-1]
    n_meta = meta_tokens.shape[0]
    n_buckets = rel_bias.shape[0]
    depth, _, in_dim = w_in.shape
    ffn = w_ffn_gate.shape[-1]
    kv_dim = (in_dim - 6 * d) // 2
    assert d % (GROUP * HEAD_DIM) == 0 and kv_dim == d // GROUP and attn_sink.shape[1] * HEAD_DIM == d
    assert n_meta % HALO_ROWS == 0 and n_meta < BLOCK

    groups = (x_prompt, x_sample)
    seq_chunks = []
    for xg in groups:
        assert xg.shape[1] % BLOCK == 0 and xg.shape[1] >= 2 * BLOCK
        for _ in range(xg.shape[0]):
            start = seq_chunks[-1][0] + seq_chunks[-1][1] if seq_chunks else 0
            seq_chunks.append((start, xg.shape[1] // BLOCK))
    seq_chunks = tuple(seq_chunks)
    n_seq = len(seq_chunks)

    x_parts = [xg.reshape(-1, d) for xg in groups]
    xm = jnp.tile(meta_tokens.astype(F32), (n_seq, 1))

    cols = tuple(d * i for i in range(1, 6))
    k_col, v_col = 6 * d, 6 * d + kv_dim
    ffn_tile = 512
    ffn_p = -(-ffn // ffn_tile) * ffn_tile
    out_tile = _divisor_tile(d, 512, V7X_LANES)
    n_q, n_rest = d // kv_dim, 5 * d // kv_dim

    def regroup(j):
        return jnp.where(j < n_q, j, jnp.where(j < n_q + n_rest, j + 2, j - n_rest))

    bkt_tok, bkt_met = _bucket_tables(n_meta, n_buckets)

    geo = dict(seq_chunks=seq_chunks, n_meta=n_meta)
    xb, ss = _prep(x_parts)
    xmb, ssm = _prep([xm])
    for l in range(depth):
        with_meta = l < depth - 1
        table = jnp.concatenate([rel_bias, attn_sink[l][None]], axis=0)
        z, zm = _wmm(xb, xmb, [w_in], l, in_dim, kv_dim, BF16, src_block=regroup, norm=(norm1_g[l], ss, ssm))
        attn = _attention(z, zm, _bias_tiles(table, bkt_tok), q_norm_g[l], k_norm_g[l],
                          d_model=d, k_col=k_col, v_col=v_col, **geo)
        merged = _merge(attn, z, zm, conv_w[l], branch_norm_a[l], branch_norm_c[l], cols=cols, **geo)
        merged_m = None
        if with_meta:
            attn_m = _attention_meta(z, zm, _bias_tiles(table, bkt_met), q_norm_g[l], k_norm_g[l],
                                     d_model=d, k_col=k_col, v_col=v_col, **geo)
            merged_m = _merge_meta(attn_m, z, zm, conv_w[l], branch_norm_a[l], branch_norm_c[l], cols=cols, **geo)
        h, hb, hss, *hm = _wmm(merged, merged_m, [w_out], l, d, out_tile, F32, res=x_parts,
                               auxres=xm if with_meta else None, emit_stats=True)
        hm, hmb, hmss = hm if with_meta else (None, None, None)
        t, *tm, wd_b = _wmm(hb, hmb, [w_ffn_gate, w_ffn_up], l, ffn_p, ffn_tile, BF16, act=True,
                            side=(w_ffn_down, ffn_p), norm=(norm2_g[l], hss, hmss))
        if with_meta:
            x, xb, ss = _matmul_res(t, wd_b, h, emit_stats=True)
            x_parts = [x]
            xm, xmb, ssm = _matmul_res(tm[0], wd_b, hm, emit_stats=True)

    outs = []
    row = 0
    for xg in groups:
        nrows = xg.shape[0] * xg.shape[1]
        outs.append(_matmul_res(t, wd_b, h, row, nrows)[0].reshape(xg.shape))
        row += nrows
    return tuple(outs)
```

```python
import functools
import math

import jax
import jax.numpy as jnp
from jax import lax
from jax.experimental import pallas as pl
from jax.experimental.pallas import tpu as pltpu

HEAD_DIM = 128
GROUP = 4
BLOCK = 128
MAX_DISTANCE = 128
EPS = 1e-6
LOG2E = math.log2(math.e)

V7X_VMEM_BYTES = 64 << 20
V7X_LANES = 128
BF16_SUBLANES = 16
F32_SUBLANES = 8
HALO_ROWS = BF16_SUBLANES
VMEM_SLACK_BYTES = 10 << 20

F32 = jnp.float32
BF16 = jnp.bfloat16


def _params(dims, window_bytes):
    limit = min(2 * window_bytes + VMEM_SLACK_BYTES, V7X_VMEM_BYTES - (6 << 20))
    return pltpu.CompilerParams(dimension_semantics=dims, vmem_limit_bytes=int(limit))


def _divisor_tile(n, cap, unit):
    if n <= cap:
        return n
    t = (cap // unit) * unit
    while t >= unit:
        if n % t == 0:
            return t
        t -= unit
    raise ValueError(f"no tile for {n}")


def _nbytes(shape, dtype):
    return math.prod(shape) * jnp.dtype(dtype).itemsize


def _rms(x, g):
    ms = jnp.mean(x * x, axis=-1, keepdims=True)
    return x * lax.rsqrt(ms + EPS) * g


def _row_sumsq(x):
    return jnp.broadcast_to(jnp.sum(x * x, axis=-1, keepdims=True), (x.shape[0], V7X_LANES))


def _rstd(ss_ref, width):
    ss = ss_ref[0]
    for p in range(1, ss_ref.shape[0]):
        ss = ss + ss_ref[p]
    return lax.rsqrt(ss / width + EPS)


def _prep_kernel(*refs, starts):
    x_refs, o_ref, ss_ref = refs[:len(starts)], refs[-2], refs[-1]
    i = pl.program_id(0)
    ends = starts[1:] + (None,)
    for x_ref, lo, hi in zip(x_refs, starts, ends):
        in_seg = i >= lo if hi is None else jnp.logical_and(i >= lo, i < hi)

        @pl.when(in_seg)
        def _():
            x = x_ref[...]
            o_ref[...] = x.astype(o_ref.dtype)
            ss_ref[0] = _row_sumsq(x)


def _prep(xs):
    d = xs[0].shape[1]
    m = sum(x.shape[0] for x in xs)
    bm = _divisor_tile(math.gcd(*[x.shape[0] for x in xs]), 256, BF16_SUBLANES)
    blocks = [x.shape[0] // bm for x in xs]
    starts = tuple(sum(blocks[:s]) for s in range(len(xs)))
    in_specs = [pl.BlockSpec((bm, d), lambda i, start=start, nblk=nblk: (jnp.clip(i - start, 0, nblk - 1), 0))
                for start, nblk in zip(starts, blocks)]
    return pl.pallas_call(
        functools.partial(_prep_kernel, starts=starts),
        grid=(m // bm,),
        in_specs=in_specs,
        out_specs=[pl.BlockSpec((bm, d), lambda i: (i, 0)), pl.BlockSpec((1, bm, V7X_LANES), lambda i: (0, i, 0))],
        out_shape=[jax.ShapeDtypeStruct((m, d), BF16), jax.ShapeDtypeStruct((1, m, V7X_LANES), F32)],
        compiler_params=_params(("parallel",), len(xs) * _nbytes((bm, d), F32) + _nbytes((bm, d), BF16)),
        name="prep",
    )(*xs)


def _wmm_kernel(*refs, n_w, res_starts, has_aux, has_side, has_norm, emit_stats, act, n_tiles, rows, src_cols,
                side_rows):
    it = iter(refs)
    a_ref = next(it)
    aux_ref = next(it) if has_aux else None
    w_refs = [next(it) for _ in range(n_w)]
    gain_ref = next(it) if has_norm else None
    ss_ref = next(it) if has_norm else None
    auxss_ref = next(it) if has_norm and has_aux else None
    res_refs = [next(it) for _ in res_starts]
    auxres_ref = next(it) if res_starts and has_aux else None
    side_ref = next(it) if has_side else None
    o_ref = next(it)
    ob_ref, oss_ref = (next(it), next(it)) if emit_stats else (None, None)
    oaux_ref = next(it) if has_aux else None
    oauxb_ref, oauxss_ref = (next(it), next(it)) if emit_stats and has_aux else (None, None)
    oside_ref = next(it) if has_side else None
    w_bufs = (next(it), next(it))
    k_dim = w_bufs[0].shape[1]
    g = pl.program_id(0)
    i = pl.program_id(1)
    bn = w_bufs[0].shape[-1]

    if has_side:
        w = side_ref[...]
        row = g * w.shape[0] + lax.broadcasted_iota(jnp.int32, w.shape, 0)
        oside_ref[...] = jnp.where(row < side_rows, w, 0.0).astype(oside_ref.dtype)

    def stage(buf):
        row0 = pl.multiple_of(i * rows, rows)
        for t in range(n_w):
            w = w_refs[t][...]
            if has_norm:
                w = w * jnp.tile(gain_ref[...], (1, bn // V7X_LANES))
            if src_cols % bn:
                col = jnp.minimum(g, n_tiles - 1) * bn + lax.broadcasted_iota(jnp.int32, w.shape, 1)
                w = jnp.where(col < src_cols, w, 0.0)
            buf[t, pl.ds(row0, rows), :] = w.astype(buf.dtype)

    def apply(buf, a, ss, res, out, out_b, out_ss):
        ys = [jnp.dot(a, buf[t], preferred_element_type=F32) for t in range(n_w)]
        if has_norm:
            rstd = jnp.tile(_rstd(ss, k_dim), (1, bn // V7X_LANES))
            ys = [y * rstd for y in ys]
        y = ys[0] * jax.nn.sigmoid(ys[0]) * ys[1] if act else ys[0]
        y = y if res is None else res + y
        out[...] = y.astype(out.dtype)
        if emit_stats:
            out_b[...] = y.astype(out_b.dtype)
            out_ss[0] = _row_sumsq(y)

    def compute(buf):
        res = None
        if res_starts:
            res = res_refs[0][...]
            for start, ref in zip(res_starts[1:], res_refs[1:]):
                res = jnp.where(i >= start, ref[...], res)
        apply(buf, a_ref[...], ss_ref, res, o_ref, ob_ref, oss_ref)
        if has_aux:
            @pl.when(i == 0)
            def _():
                apply(buf, aux_ref[...], auxss_ref, auxres_ref[...] if res_starts else None,
                      oaux_ref, oauxb_ref, oauxss_ref)

    @pl.when(g == 0)
    def _():
        stage(w_bufs[0])

    for parity in range(2):
        @pl.when(jnp.logical_and(g > 0, g % 2 == parity))
        def _():
            stage(w_bufs[parity])
            compute(w_bufs[1 - parity])


def _wmm(a, aux, ws, layer, n_out, bn, out_dtype, *, src_block=None, act=False, res=None, auxres=None, side=None,
         norm=None, emit_stats=False):
    m, k = a.shape
    src_cols = ws[0].shape[2]
    bm = _divisor_tile(m, 1024, BF16_SUBLANES)
    ni, n_tiles = m // bm, n_out // bn
    rows = k // ni
    assert k % ni == 0 and rows % BF16_SUBLANES == 0 and n_out % bn == 0
    assert src_block is None or src_cols % bn == 0
    src = src_block if src_block is not None else (lambda j: j)
    res = list(res) if res is not None else []
    has_aux, has_side, has_norm = aux is not None, side is not None, norm is not None
    ma = aux.shape[0] if has_aux else 0
    res_blocks = [r.shape[0] // bm for r in res]
    assert all(r.shape[0] % bm == 0 for r in res) and sum(res_blocks) == (ni if res else 0)
    res_starts = tuple(sum(res_blocks[:s]) for s in range(len(res)))

    def row_blk(g, i):
        return jnp.where(g == 0, 0, i)

    def col_blk(g):
        return jnp.maximum(g - 1, 0)

    def last_tile(g):
        return jnp.minimum(g, n_tiles - 1)

    in_specs = [pl.BlockSpec((bm, k), lambda g, i: (row_blk(g, i), 0))]
    args = [a]
    if has_aux:
        in_specs.append(pl.BlockSpec((ma, k), lambda g, i: (0, 0)))
        args.append(aux)
    for w in ws:
        in_specs.append(pl.BlockSpec((None, rows, bn), lambda g, i: (layer, i, src(last_tile(g)))))
        args.append(w)
    norm_bytes = 0
    if has_norm:
        gain, ss, ss_aux = norm
        in_specs.append(pl.BlockSpec((rows, V7X_LANES), lambda g, i: (i, 0)))
        args.append(jnp.broadcast_to(gain[:, None], (k, V7X_LANES)))
        in_specs.append(pl.BlockSpec((ss.shape[0], bm, V7X_LANES), lambda g, i: (0, row_blk(g, i), 0)))
        args.append(ss)
        if has_aux:
            in_specs.append(pl.BlockSpec((ss_aux.shape[0], ma, V7X_LANES), lambda g, i: (0, 0, 0)))
            args.append(ss_aux)
        norm_bytes = _nbytes((rows + ss.shape[0] * (bm + ma), V7X_LANES), F32)
    for start, nblk, r in zip(res_starts, res_blocks, res):
        in_specs.append(pl.BlockSpec(
            (bm, bn), lambda g, i, start=start, nblk=nblk: (jnp.clip(row_blk(g, i) - start, 0, nblk - 1), col_blk(g))))
        args.append(r)
    if res and has_aux:
        in_specs.append(pl.BlockSpec((ma, bn), lambda g, i: (0, col_blk(g))))
        args.append(auxres)
    out_specs, out_shape = [], []

    def add_outputs(nrows, blk_rows, row_index):
        out_specs.append(pl.BlockSpec((blk_rows, bn), lambda g, i: (row_index(g, i), col_blk(g))))
        out_shape.append(jax.ShapeDtypeStruct((nrows, n_out), out_dtype))
        if emit_stats:
            out_specs.append(pl.BlockSpec((blk_rows, bn), lambda g, i: (row_index(g, i), col_blk(g))))
            out_shape.append(jax.ShapeDtypeStruct((nrows, n_out), BF16))
            out_specs.append(pl.BlockSpec((1, blk_rows, V7X_LANES), lambda g, i: (col_blk(g), row_index(g, i), 0)))
            out_shape.append(jax.ShapeDtypeStruct((n_tiles, nrows, V7X_LANES), F32))

    add_outputs(m, bm, row_blk)
    if has_aux:
        add_outputs(ma, ma, lambda g, i: 0)
    side_rows = side_bytes = 0
    if has_side:
        side_w, side_out_rows = side
        side_rows, side_cols = side_w.shape[1:]
        sr, sc = side_out_rows // n_tiles, side_cols // ni
        assert side_out_rows % n_tiles == 0 and side_cols % ni == 0 and sr % BF16_SUBLANES == 0 and sc % V7X_LANES == 0
        in_specs.append(pl.BlockSpec((None, sr, sc), lambda g, i: (layer, last_tile(g), i)))
        args.append(side_w)
        out_specs.append(pl.BlockSpec((sr, sc), lambda g, i: (g, i)))
        out_shape.append(jax.ShapeDtypeStruct((side_out_rows + sr, side_cols), BF16))
        side_bytes = _nbytes((sr, sc), F32) + _nbytes((sr, sc), BF16)
    win = (_nbytes((bm + ma, k), BF16) + len(ws) * _nbytes((rows, bn), F32) + side_bytes + norm_bytes
           + _nbytes((bm + ma, bn), out_dtype) + len(res) * _nbytes((bm + ma, bn), F32)
           + emit_stats * (_nbytes((bm + ma, bn), BF16) + _nbytes((bm + ma, V7X_LANES), F32)))
    scratch = _nbytes((2, len(ws), k, bn), BF16)
    return pl.pallas_call(
        functools.partial(_wmm_kernel, n_w=len(ws), res_starts=res_starts, has_aux=has_aux, has_side=has_side,
                          has_norm=has_norm, emit_stats=emit_stats, act=act, n_tiles=n_tiles, rows=rows,
                          src_cols=src_cols, side_rows=side_rows),
        grid=(n_tiles + 1, ni),
        in_specs=in_specs,
        out_specs=out_specs,
        out_shape=out_shape,
        scratch_shapes=[pltpu.VMEM((len(ws), k, bn), BF16), pltpu.VMEM((len(ws), k, bn), BF16)],
        compiler_params=_params(("arbitrary", "arbitrary"), win + scratch // 2),
        name="wmm",
    )(*args)


def _mm_res_kernel(a_ref, b_ref, r_ref, o_ref, *stats):
    j, kk = pl.program_id(1), pl.program_id(2)

    @pl.when(kk == 0)
    def _():
        o_ref[...] = r_ref[...]

    o_ref[...] += jnp.dot(a_ref[...], b_ref[...], preferred_element_type=F32)
    if stats:
        ob_ref, ss_ref = stats

        @pl.when(kk == pl.num_programs(2) - 1)
        def _():
            y = o_ref[...]
            ob_ref[...] = y.astype(ob_ref.dtype)

            @pl.when(j == 0)
            def _():
                ss_ref[0] = _row_sumsq(y)

            @pl.when(j > 0)
            def _():
                ss_ref[0] += _row_sumsq(y)


def _matmul_res(a, b, res, row0=0, nrows=None, emit_stats=False):
    m, k = a.shape
    n = b.shape[1]
    nrows = m if nrows is None else nrows
    bm = _divisor_tile(nrows, 1024, BF16_SUBLANES)
    bn = _divisor_tile(n, 1024, V7X_LANES)
    bk = _divisor_tile(k, 4096, V7X_LANES)
    assert row0 % bm == 0
    off = row0 // bm
    win = _nbytes((bm, bk), BF16) + _nbytes((bk, bn), BF16) + 2 * _nbytes((bm, bn), F32)
    return pl.pallas_call(
        _mm_res_kernel,
        grid=(nrows // bm, n // bn, k // bk),
        in_specs=[pl.BlockSpec((bm, bk), lambda i, j, kk: (i + off, kk)),
                  pl.BlockSpec((bk, bn), lambda i, j, kk: (kk, j)),
                  pl.BlockSpec((bm, bn), lambda i, j, kk: (i + off, j))],
        out_specs=[pl.BlockSpec((bm, bn), lambda i, j, kk: (i, j))] + emit_stats * [
            pl.BlockSpec((bm, bn), lambda i, j, kk: (i, j)),
            pl.BlockSpec((1, bm, V7X_LANES), lambda i, j, kk: (0, i, 0))],
        out_shape=[jax.ShapeDtypeStruct((nrows, n), F32)] + emit_stats * [
            jax.ShapeDtypeStruct((nrows, n), BF16), jax.ShapeDtypeStruct((1, nrows, V7X_LANES), F32)],
        compiler_params=_params(("parallel", "arbitrary", "arbitrary"), win + _nbytes((bm, bn), BF16)),
        name="matmul_res",
    )(a, b, res)


def _t5_bucket(rel, n_buckets):
    half = n_buckets // 2
    exact = half // 2
    n = jnp.abs(rel)
    n_f = jnp.maximum(n, 1).astype(F32)
    large = exact + (jnp.log(n_f / exact) / math.log(MAX_DISTANCE / exact) * (half - exact)).astype(jnp.int32)
    large = jnp.minimum(large, half - 1)
    return jnp.where(rel > 0, half, 0) + jnp.where(n < exact, n, large)


def _bucket_tables(n_meta, n_buckets):
    qi = jnp.arange(BLOCK)[:, None]
    sj = jnp.arange(3 * BLOCK)[None, :]
    rel_band = sj - BLOCK - qi
    band = jnp.where(jnp.abs(rel_band) <= BLOCK, _t5_bucket(rel_band, n_buckets), -1)
    sink = jnp.full((BLOCK, 1), n_buckets, jnp.int32)
    pad = jnp.full((BLOCK, BLOCK - n_meta - 1), -1, jnp.int32)
    mk = jnp.arange(n_meta)[None, :]
    meta_first = _t5_bucket(mk - (n_meta + qi), n_buckets)
    meta_far = jnp.full((BLOCK, n_meta), n_buckets // 2 - 1, jnp.int32)
    tok = jnp.concatenate([band, meta_far, sink, pad, meta_first, sink, pad], axis=1)[None].astype(jnp.int32)

    mq = jnp.arange(n_meta)[:, None]
    tk = jnp.arange(BLOCK)[None, :]
    rel_tok = n_meta + tk - mq
    mband = jnp.where(jnp.abs(rel_tok) <= BLOCK, _t5_bucket(rel_tok, n_buckets), -1)
    mmeta = _t5_bucket(mk - mq, n_buckets)
    met = jnp.concatenate([mband, mmeta, sink[:n_meta], pad[:n_meta]], axis=1).astype(jnp.int32)
    return tok, met[None]


def _bias_kernel(tab_ref, bkt_ref, o_ref, *, n_ids, token_variants):
    h = pl.program_id(0)
    bkt = bkt_ref[...]
    acc = jnp.full(bkt.shape, -jnp.inf, F32)
    for b in range(n_ids):
        acc = jnp.where(bkt == b, tab_ref[b, h], acc)
    acc = acc * LOG2E
    if token_variants:
        band, far, first = acc[0, :, :3 * BLOCK], acc[0, :, 3 * BLOCK:4 * BLOCK], acc[0, :, 4 * BLOCK:]
        masked = jnp.full((BLOCK, BLOCK), -jnp.inf, F32)
        o_ref[0, 0, 0] = jnp.concatenate([band, far], axis=1)
        o_ref[0, 1, 0] = jnp.concatenate([masked, band[:, BLOCK:], first], axis=1)
        o_ref[0, 2, 0] = jnp.concatenate([band[:, :2 * BLOCK], masked, far], axis=1)
    else:
        o_ref[0, :, 0] = acc


def _bias_tiles(table, bkt, token_variants=False):
    n_ids, n_heads = table.shape
    v, r, c = (3, BLOCK, 4 * BLOCK) if token_variants else bkt.shape
    return pl.pallas_call(
        functools.partial(_bias_kernel, n_ids=n_ids, token_variants=token_variants),
        grid=(n_heads,),
        in_specs=[pl.BlockSpec(memory_space=pltpu.SMEM),
                  pl.BlockSpec(bkt.shape, lambda h: (0, 0, 0))],
        out_specs=pl.BlockSpec((1, v, 1, r, c), lambda h: (h // GROUP, 0, h % GROUP, 0, 0)),
        out_shape=jax.ShapeDtypeStruct((n_heads // GROUP, v, GROUP, r, c), F32),
        compiler_params=_params(("parallel",), 2 * _nbytes(bkt.shape, F32)),
        name="bias_tiles",
    )(table, bkt)


def _softmax_pv(q, k, v, bias2):
    s = lax.dot_general(q, k, (((1,), (1,)), ((), ())), preferred_element_type=F32) + bias2
    m = jnp.max(s, axis=-1, keepdims=True)
    p = jnp.exp2(s - m)
    denom = jnp.sum(p, axis=-1, keepdims=True)
    return jnp.dot(p.astype(BF16), v, preferred_element_type=F32) * (1.0 / denom)


def _fill_meta_keys(km_scr, vm_scr, km_ref, vm_ref, kg, n_meta):
    km_scr[...] = jnp.zeros(km_scr.shape, BF16)
    vm_scr[...] = jnp.zeros(vm_scr.shape, BF16)
    km_scr[0:n_meta] = _rms(km_ref[...].astype(F32), kg).astype(BF16)
    vm_scr[0:n_meta] = vm_ref[...].astype(BF16)


def _attn_kernel(qg_ref, kg_ref, bias_ref, q_ref, kp_ref, kc_ref, kn_ref, vp_ref, vc_ref, vn_ref,
                 km_ref, vm_ref, o_ref, q_scr, k_scr, v_scr, km_scr, vm_scr, *, chunks, seq_chunks, n_meta, scale):
    r = pl.program_id(1)
    qg = qg_ref[...] * (scale * LOG2E)
    kg = kg_ref[...]
    body = chunks * BLOCK
    stack = GROUP * BLOCK

    k_scr[0:BLOCK] = _rms(kp_ref[...].astype(F32), kg).astype(BF16)
    k_scr[BLOCK:BLOCK + body] = _rms(kc_ref[...].astype(F32), kg).astype(BF16)
    k_scr[BLOCK + body:2 * BLOCK + body] = _rms(kn_ref[...].astype(F32), kg).astype(BF16)
    v_scr[0:BLOCK] = vp_ref[...].astype(BF16)
    v_scr[BLOCK:BLOCK + body] = vc_ref[...].astype(BF16)
    v_scr[BLOCK + body:2 * BLOCK + body] = vn_ref[...].astype(BF16)
    _fill_meta_keys(km_scr, vm_scr, km_ref, vm_ref, kg, n_meta)
    for g in range(GROUP):
        qn = _rms(q_ref[:, g * HEAD_DIM:(g + 1) * HEAD_DIM].astype(F32), qg).astype(BF16)
        for c in range(chunks):
            q_scr[(c * GROUP + g) * BLOCK:(c * GROUP + g + 1) * BLOCK] = qn[c * BLOCK:(c + 1) * BLOCK]
    k_meta = km_scr[...]
    v_meta = vm_scr[...]

    def chunk(cc, carry):
        gch = r * chunks + cc
        is_first = functools.reduce(jnp.logical_or, [gch == s0 for s0, _ in seq_chunks])
        is_last = functools.reduce(jnp.logical_or, [gch == s0 + n - 1 for s0, n in seq_chunks])
        var = jnp.where(is_first, 1, jnp.where(is_last, 2, 0))
        row0 = pl.multiple_of(cc * BLOCK, BLOCK)
        k_all = jnp.concatenate([k_scr[pl.ds(row0, 3 * BLOCK), :], k_meta], axis=0)
        v_all = jnp.concatenate([v_scr[pl.ds(row0, 3 * BLOCK), :], v_meta], axis=0)
        q = q_scr[pl.ds(pl.multiple_of(cc * stack, stack), stack), :]
        bias = bias_ref[0, var].reshape(stack, 4 * BLOCK)
        o = _softmax_pv(q, k_all, v_all, bias)
        for g in range(GROUP):
            o_ref[pl.ds(row0, BLOCK), g * HEAD_DIM:(g + 1) * HEAD_DIM] = o[g * BLOCK:(g + 1) * BLOCK]
        return carry

    lax.fori_loop(0, chunks, chunk, 0, unroll=True)


def _seq_of_chunk(c, seq_chunks):
    s = 0
    for s0, _ in seq_chunks[1:]:
        s = s + (c >= s0).astype(jnp.int32)
    return s


def _attention(z, zm, bias_tok, q_g, k_g, *, seq_chunks, n_meta, d_model, k_col, v_col):
    m = z.shape[0]
    n_chunks = m // BLOCK
    kv_heads = d_model // (GROUP * HEAD_DIM)
    chunks = math.gcd(16, *[n for _, n in seq_chunks])
    body = chunks * BLOCK
    qw = GROUP * HEAD_DIM
    kb, vb = k_col // HEAD_DIM, v_col // HEAD_DIM
    seq_of = functools.partial(_seq_of_chunk, seq_chunks=seq_chunks)

    def halo_prev(h, r):
        return jnp.maximum(r * chunks - 1, 0)

    def halo_next(h, r):
        return jnp.minimum(r * chunks + chunks, n_chunks - 1)

    in_specs = [
        pl.BlockSpec((1, HEAD_DIM), lambda h, r: (0, 0)),
        pl.BlockSpec((1, HEAD_DIM), lambda h, r: (0, 0)),
        pl.BlockSpec((1, 3, GROUP, BLOCK, 4 * BLOCK), lambda h, r: (h, 0, 0, 0, 0)),
        pl.BlockSpec((body, qw), lambda h, r: (r, h)),
        pl.BlockSpec((BLOCK, HEAD_DIM), lambda h, r: (halo_prev(h, r), kb + h)),
        pl.BlockSpec((body, HEAD_DIM), lambda h, r: (r, kb + h)),
        pl.BlockSpec((BLOCK, HEAD_DIM), lambda h, r: (halo_next(h, r), kb + h)),
        pl.BlockSpec((BLOCK, HEAD_DIM), lambda h, r: (halo_prev(h, r), vb + h)),
        pl.BlockSpec((body, HEAD_DIM), lambda h, r: (r, vb + h)),
        pl.BlockSpec((BLOCK, HEAD_DIM), lambda h, r: (halo_next(h, r), vb + h)),
        pl.BlockSpec((n_meta, HEAD_DIM), lambda h, r: (seq_of(r * chunks), kb + h)),
        pl.BlockSpec((n_meta, HEAD_DIM), lambda h, r: (seq_of(r * chunks), vb + h)),
    ]
    win = (_nbytes((3, GROUP, BLOCK, 4 * BLOCK), F32) + 2 * _nbytes((body, qw), F32)
           + 4 * _nbytes((body + 2 * BLOCK, HEAD_DIM), F32))
    return pl.pallas_call(
        functools.partial(_attn_kernel, chunks=chunks, seq_chunks=seq_chunks, n_meta=n_meta,
                          scale=HEAD_DIM ** -0.5),
        grid=(kv_heads, m // body),
        in_specs=in_specs,
        out_specs=pl.BlockSpec((body, qw), lambda h, r: (r, h)),
        out_shape=jax.ShapeDtypeStruct((m, d_model), F32),
        scratch_shapes=[pltpu.VMEM((body * GROUP, HEAD_DIM), BF16),
                        pltpu.VMEM((body + 2 * BLOCK, HEAD_DIM), BF16),
                        pltpu.VMEM((body + 2 * BLOCK, HEAD_DIM), BF16),
                        pltpu.VMEM((BLOCK, HEAD_DIM), BF16),
                        pltpu.VMEM((BLOCK, HEAD_DIM), BF16)],
        compiler_params=_params(("parallel", "arbitrary"), win),
        name="attention",
    )(q_g.reshape(1, HEAD_DIM), k_g.reshape(1, HEAD_DIM), bias_tok, z, z, z, z, z, z, z, zm, zm)


def _attn_meta_kernel(qg_ref, kg_ref, bias_ref, q_ref, k1_ref, v1_ref, km_ref, vm_ref, o_ref,
                      km_scr, vm_scr, *, n_meta, scale):
    qg = qg_ref[...] * (scale * LOG2E)
    kg = kg_ref[...]
    _fill_meta_keys(km_scr, vm_scr, km_ref, vm_ref, kg, n_meta)
    k_all = jnp.concatenate([_rms(k1_ref[...].astype(F32), kg).astype(BF16), km_scr[...]], axis=0)
    v_all = jnp.concatenate([v1_ref[...].astype(BF16), vm_scr[...]], axis=0)
    q = q_ref[...].astype(F32)
    qn = jnp.concatenate([_rms(q[:, g * HEAD_DIM:(g + 1) * HEAD_DIM], qg).astype(BF16) for g in range(GROUP)], axis=0)
    o = _softmax_pv(qn, k_all, v_all, bias_ref[0, 0].reshape(GROUP * n_meta, 2 * BLOCK))
    for g in range(GROUP):
        o_ref[:, g * HEAD_DIM:(g + 1) * HEAD_DIM] = o[g * n_meta:(g + 1) * n_meta]


def _attention_meta(z, zm, bias_met, q_g, k_g, *, seq_chunks, n_meta, d_model, k_col, v_col):
    n_seq = len(seq_chunks)
    kv_heads = d_model // (GROUP * HEAD_DIM)
    qw = GROUP * HEAD_DIM
    kb, vb = k_col // HEAD_DIM, v_col // HEAD_DIM

    def first_chunk(s):
        c = 0
        for i, (s0, _) in enumerate(seq_chunks):
            c = c + jnp.where(s == i, s0, 0)
        return c

    in_specs = [
        pl.BlockSpec((1, HEAD_DIM), lambda s, h: (0, 0)),
        pl.BlockSpec((1, HEAD_DIM), lambda s, h: (0, 0)),
        pl.BlockSpec((1, 1, GROUP, n_meta, 2 * BLOCK), lambda s, h: (h, 0, 0, 0, 0)),
        pl.BlockSpec((n_meta, qw), lambda s, h: (s, h)),
        pl.BlockSpec((BLOCK, HEAD_DIM), lambda s, h: (first_chunk(s), kb + h)),
        pl.BlockSpec((BLOCK, HEAD_DIM), lambda s, h: (first_chunk(s), vb + h)),
        pl.BlockSpec((n_meta, HEAD_DIM), lambda s, h: (s, kb + h)),
        pl.BlockSpec((n_meta, HEAD_DIM), lambda s, h: (s, vb + h)),
    ]
    return pl.pallas_call(
        functools.partial(_attn_meta_kernel, n_meta=n_meta, scale=HEAD_DIM ** -0.5),
        grid=(n_seq, kv_heads),
        in_specs=in_specs,
        out_specs=pl.BlockSpec((n_meta, qw), lambda s, h: (s, h)),
        out_shape=jax.ShapeDtypeStruct((n_seq * n_meta, d_model), F32),
        scratch_shapes=[pltpu.VMEM((BLOCK, HEAD_DIM), BF16), pltpu.VMEM((BLOCK, HEAD_DIM), BF16)],
        compiler_params=_params(("parallel", "parallel"), 1 << 20),
        name="attention_meta",
    )(q_g.reshape(1, HEAD_DIM), k_g.reshape(1, HEAD_DIM), bias_met, zm, z, z, zm, zm)


def _merge_math(attn, gb, gc, hc, ga, gcv, u_prev, u_next, cw, na, nc):
    u = gc * hc
    rows = u.shape[0]
    ridx = lax.broadcasted_iota(jnp.int32, u.shape, 0)
    up = jnp.where(ridx == 0, u_prev, pltpu.roll(u, 1, axis=0))
    un = jnp.where(ridx == rows - 1, u_next, pltpu.roll(u, rows - 1, axis=0))
    conv = gb * (cw[0:1] * up + cw[1:2] * u + cw[2:3] * un)
    return jax.nn.sigmoid(ga) * _rms(attn, na) + jax.nn.sigmoid(gcv) * _rms(conv, nc)


def _merge_kernel(attn_ref, gb_ref, gc_ref, hc_ref, ga_ref, gcv_ref, gcp_ref, hcp_ref, gcn_ref, hcn_ref,
                  gcm_ref, hcm_ref, cw_ref, na_ref, nc_ref, o_ref, conv_scr, *, seq_chunks):
    c = pl.program_id(0)
    is_first = functools.reduce(jnp.logical_or, [c == s0 for s0, _ in seq_chunks])
    is_last = functools.reduce(jnp.logical_or, [c == s0 + n - 1 for s0, n in seq_chunks])
    last = HALO_ROWS - 1
    rows, d = o_ref.shape
    n_tiles = d // V7X_LANES
    ridx = lax.broadcasted_iota(jnp.int32, (rows, V7X_LANES), 0)

    def cols(ct):
        return pl.ds(pl.multiple_of(ct * V7X_LANES, V7X_LANES), V7X_LANES)

    def f32(ref, sl):
        return ref[:, sl].astype(F32)

    def conv_pass(ct, carry):
        ss_attn, ss_conv = carry
        sl = cols(ct)
        u = f32(gc_ref, sl) * f32(hc_ref, sl)
        u_prev_tok = (f32(gcp_ref, sl) * f32(hcp_ref, sl))[last:last + 1]
        u_prev_meta = (f32(gcm_ref, sl) * f32(hcm_ref, sl))[last:last + 1]
        u_prev = jnp.where(is_first, u_prev_meta, u_prev_tok)
        u_next_tok = (f32(gcn_ref, sl) * f32(hcn_ref, sl))[0:1]
        u_next = jnp.where(is_last, jnp.zeros_like(u_next_tok), u_next_tok)
        up = jnp.where(ridx == 0, u_prev, pltpu.roll(u, 1, axis=0))
        un = jnp.where(ridx == rows - 1, u_next, pltpu.roll(u, rows - 1, axis=0))
        cw = cw_ref[:, sl]
        conv = f32(gb_ref, sl) * (cw[0:1] * up + cw[1:2] * u + cw[2:3] * un)
        conv_scr[:, sl] = conv
        attn = attn_ref[:, sl]
        return ss_attn + attn * attn, ss_conv + conv * conv

    zero = jnp.zeros((rows, V7X_LANES), F32)
    ss_attn, ss_conv = lax.fori_loop(0, n_tiles, conv_pass, (zero, zero), unroll=2)
    r_attn = lax.rsqrt(jnp.sum(ss_attn, axis=-1, keepdims=True) / d + EPS)
    r_conv = lax.rsqrt(jnp.sum(ss_conv, axis=-1, keepdims=True) / d + EPS)

    def gate_pass(ct, carry):
        sl = cols(ct)
        a = attn_ref[:, sl] * r_attn * na_ref[:, sl]
        c = conv_scr[:, sl] * r_conv * nc_ref[:, sl]
        o_ref[:, sl] = (jax.nn.sigmoid(f32(ga_ref, sl)) * a + jax.nn.sigmoid(f32(gcv_ref, sl)) * c).astype(o_ref.dtype)
        return carry

    lax.fori_loop(0, n_tiles, gate_pass, 0, unroll=2)


def _merge(attn, z, zm, conv_w, norm_a, norm_c, *, seq_chunks, n_meta, cols):
    m, d = attn.shape
    n_chunks = m // BLOCK
    per_chunk = BLOCK // HALO_ROWS
    per_meta = n_meta // HALO_ROWS
    gb, gc, hc, ga, gcv = [c // d for c in cols]
    seq_of = functools.partial(_seq_of_chunk, seq_chunks=seq_chunks)

    def main(col):
        return pl.BlockSpec((BLOCK, d), lambda c: (c, col))

    def prev_rows(col):
        return pl.BlockSpec((HALO_ROWS, d), lambda c: (jnp.maximum(c * per_chunk - 1, 0), col))

    def next_rows(col):
        return pl.BlockSpec((HALO_ROWS, d), lambda c: (jnp.minimum(c + 1, n_chunks - 1) * per_chunk, col))

    def meta_rows(col):
        return pl.BlockSpec((HALO_ROWS, d), lambda c: (seq_of(c) * per_meta + per_meta - 1, col))

    def row(nrows):
        return pl.BlockSpec((nrows, d), lambda c: (0, 0))

    win = 6 * _nbytes((BLOCK, d), F32) + _nbytes((BLOCK, d), BF16) + 8 * _nbytes((HALO_ROWS, d), F32)
    return pl.pallas_call(
        functools.partial(_merge_kernel, seq_chunks=seq_chunks),
        grid=(n_chunks,),
        in_specs=[main(0), main(gb), main(gc), main(hc), main(ga), main(gcv),
                  prev_rows(gc), prev_rows(hc), next_rows(gc), next_rows(hc), meta_rows(gc), meta_rows(hc),
                  row(3), row(1), row(1)],
        out_specs=pl.BlockSpec((BLOCK, d), lambda c: (c, 0)),
        out_shape=jax.ShapeDtypeStruct((m, d), BF16),
        scratch_shapes=[pltpu.VMEM((BLOCK, d), F32)],
        compiler_params=_params(("parallel",), win),
        name="merge",
    )(attn, z, z, z, z, z, z, z, z, z, zm, zm, conv_w, norm_a.reshape(1, d), norm_c.reshape(1, d))


def _merge_meta_kernel(attn_ref, gb_ref, gc_ref, hc_ref, ga_ref, gcv_ref, gcn_ref, hcn_ref,
                       cw_ref, na_ref, nc_ref, o_ref):
    f32 = lambda ref: ref[...].astype(F32)
    u_next = (f32(gcn_ref) * f32(hcn_ref))[0:1]
    u_prev = jnp.zeros_like(u_next)
    o_ref[...] = _merge_math(attn_ref[...], f32(gb_ref), f32(gc_ref), f32(hc_ref), f32(ga_ref), f32(gcv_ref),
                             u_prev, u_next, cw_ref[...], na_ref[...], nc_ref[...]).astype(o_ref.dtype)


def _merge_meta(attn_m, z, zm, conv_w, norm_a, norm_c, *, seq_chunks, n_meta, cols):
    mm, d = attn_m.shape
    per_chunk = BLOCK // HALO_ROWS
    gb, gc, hc, ga, gcv = [c // d for c in cols]

    def first_rows(s):
        r = 0
        for i, (s0, _) in enumerate(seq_chunks):
            r = r + jnp.where(s == i, s0 * per_chunk, 0)
        return r

    def main(col):
        return pl.BlockSpec((n_meta, d), lambda s: (s, col))

    def next_rows(col):
        return pl.BlockSpec((HALO_ROWS, d), lambda s: (first_rows(s), col))

    def row(nrows):
        return pl.BlockSpec((nrows, d), lambda s: (0, 0))

    win = 7 * _nbytes((n_meta, d), F32) + 2 * _nbytes((HALO_ROWS, d), F32)
    return pl.pallas_call(
        _merge_meta_kernel,
        grid=(len(seq_chunks),),
        in_specs=[main(0), main(gb), main(gc), main(hc), main(ga), main(gcv), next_rows(gc), next_rows(hc),
                  row(3), row(1), row(1)],
        out_specs=pl.BlockSpec((n_meta, d), lambda s: (s, 0)),
        out_shape=jax.ShapeDtypeStruct((mm, d), BF16),
        compiler_params=_params(("parallel",), win),
        name="merge_meta",
    )(attn_m, zm, zm, zm, zm, zm, z, z, conv_w, norm_a.reshape(1, d), norm_c.reshape(1, d))


def kernel(x_prompt, x_sample, meta_tokens, rel_bias, norm1_g, w_in, q_norm_g, k_norm_g, attn_sink, conv_w,
           branch_norm_a, branch_norm_c, w_out, norm2_g, w_ffn_gate, w_ffn_up, w_ffn_down):
    d = x_prompt.shape[-1]
    n_meta = meta_tokens.shape[0]
    n_buckets = rel_bias.shape[0]
    depth, _, in_dim = w_in.shape
    ffn = w_ffn_gate.shape[-1]
    kv_dim = (in_dim - 6 * d) // 2
    assert d % (GROUP * HEAD_DIM) == 0 and kv_dim == d // GROUP and attn_sink.shape[1] * HEAD_DIM == d
    assert n_meta % HALO_ROWS == 0 and n_meta < BLOCK

    groups = (x_prompt, x_sample)
    seq_chunks = []
    for xg in groups:
        assert xg.shape[1] % BLOCK == 0 and xg.shape[1] >= 2 * BLOCK
        for _ in range(xg.shape[0]):
            start = seq_chunks[-1][0] + seq_chunks[-1][1] if seq_chunks else 0
            seq_chunks.append((start, xg.shape[1] // BLOCK))
    seq_chunks = tuple(seq_chunks)
    n_seq = len(seq_chunks)

    x_parts = [xg.reshape(-1, d) for xg in groups]
    xm = jnp.tile(meta_tokens.astype(F32), (n_seq, 1))

    cols = tuple(d * i for i in range(1, 6))
    k_col, v_col = 6 * d, 6 * d + kv_dim
    ffn_tile = 512
    ffn_p = -(-ffn // ffn_tile) * ffn_tile
    out_tile = _divisor_tile(d, 512, V7X_LANES)
    n_q, n_rest = d // kv_dim, 5 * d // kv_dim

    def regroup(j):
        return jnp.where(j < n_q, j, jnp.where(j < n_q + n_rest, j + 2, j - n_rest))

    bkt_tok, bkt_met = _bucket_tables(n_meta, n_buckets)

    geo = dict(seq_chunks=seq_chunks, n_meta=n_meta)
    xb, ss = _prep(x_parts)
    xmb, ssm = _prep([xm])
    for l in range(depth):
        with_meta = l < depth - 1
        table = jnp.concatenate([rel_bias, attn_sink[l][None]], axis=0)
        z, zm = _wmm(xb, xmb, [w_in], l, in_dim, kv_dim, BF16, src_block=regroup, norm=(norm1_g[l], ss, ssm))
        attn = _attention(z, zm, _bias_tiles(table, bkt_tok, token_variants=True), q_norm_g[l], k_norm_g[l],
                          d_model=d, k_col=k_col, v_col=v_col, **geo)
        merged = _merge(attn, z, zm, conv_w[l], branch_norm_a[l], branch_norm_c[l], cols=cols, **geo)
        merged_m = None
        if with_meta:
            attn_m = _attention_meta(z, zm, _bias_tiles(table, bkt_met), q_norm_g[l], k_norm_g[l],
                                     d_model=d, k_col=k_col, v_col=v_col, **geo)
            merged_m = _merge_meta(attn_m, z, zm, conv_w[l], branch_norm_a[l], branch_norm_c[l], cols=cols, **geo)
        h, hb, hss, *hm = _wmm(merged, merged_m, [w_out], l, d, out_tile, F32, res=x_parts,
                               auxres=xm if with_meta else None, emit_stats=True)
        hm, hmb, hmss = hm if with_meta else (None, None, None)
        t, *tm, wd_b = _wmm(hb, hmb, [w_ffn_gate, w_ffn_up], l, ffn_p, ffn_tile, BF16, act=True,
                            side=(w_ffn_down, ffn_p), norm=(norm2_g[l], hss, hmss))
        if with_meta:
            x, xb, ss = _matmul_res(t, wd_b, h, emit_stats=True)
            x_parts = [x]
            xm, xmb, ssm = _matmul_res(tm[0], wd_b, hm, emit_stats=True)

    outs = []
    row = 0
    for xg in groups:
        nrows = xg.shape[0] * xg.shape[1]
        outs.append(_matmul_res(t, wd_b, h, row, nrows)[0].reshape(xg.shape))
        row += nrows
    return tuple(outs)
```

```python
import functools
import math

import jax
import jax.numpy as jnp
from jax import lax
from jax.experimental import pallas as pl
from jax.experimental.pallas import tpu as pltpu

HEAD_DIM = 128
GROUP = 4
BLOCK = 128
MAX_DISTANCE = 128
EPS = 1e-6
LOG2E = math.log2(math.e)

V7X_VMEM_BYTES = 64 << 20
V7X_LANES = 128
BF16_SUBLANES = 16
F32_SUBLANES = 8
HALO_ROWS = BF16_SUBLANES
VMEM_SLACK_BYTES = 10 << 20

F32 = jnp.float32
BF16 = jnp.bfloat16


def _params(dims, window_bytes):
    limit = min(2 * window_bytes + VMEM_SLACK_BYTES, V7X_VMEM_BYTES - (6 << 20))
    return pltpu.CompilerParams(dimension_semantics=dims, vmem_limit_bytes=int(limit))


def _divisor_tile(n, cap, unit):
    if n <= cap:
        return n
    t = (cap // unit) * unit
    while t >= unit:
        if n % t == 0:
            return t
        t -= unit
    raise ValueError(f"no tile for {n}")


def _nbytes(shape, dtype):
    return math.prod(shape) * jnp.dtype(dtype).itemsize


def _rms(x, g):
    ms = jnp.mean(x * x, axis=-1, keepdims=True)
    return x * lax.rsqrt(ms + EPS) * g


def _row_sumsq(x):
    sq = x * x
    part = sq[:, :V7X_LANES]
    for c in range(V7X_LANES, x.shape[1], V7X_LANES):
        part = part + sq[:, c:c + V7X_LANES]
    return part


def _rstd(ss_ref, width):
    ss = ss_ref[0]
    for p in range(1, ss_ref.shape[0]):
        ss = ss + ss_ref[p]
    return lax.rsqrt(jnp.sum(ss, axis=-1, keepdims=True) / width + EPS)


def _prep_kernel(*refs, starts):
    x_refs, o_ref, ss_ref = refs[:len(starts)], refs[-2], refs[-1]
    i = pl.program_id(0)
    ends = starts[1:] + (None,)
    for x_ref, lo, hi in zip(x_refs, starts, ends):
        in_seg = i >= lo if hi is None else jnp.logical_and(i >= lo, i < hi)

        @pl.when(in_seg)
        def _():
            x = x_ref[...]
            o_ref[...] = x.astype(o_ref.dtype)
            ss_ref[0] = _row_sumsq(x)


def _prep(xs):
    d = xs[0].shape[1]
    m = sum(x.shape[0] for x in xs)
    bm = _divisor_tile(math.gcd(*[x.shape[0] for x in xs]), 256, BF16_SUBLANES)
    blocks = [x.shape[0] // bm for x in xs]
    starts = tuple(sum(blocks[:s]) for s in range(len(xs)))
    in_specs = [pl.BlockSpec((bm, d), lambda i, start=start, nblk=nblk: (jnp.clip(i - start, 0, nblk - 1), 0))
                for start, nblk in zip(starts, blocks)]
    return pl.pallas_call(
        functools.partial(_prep_kernel, starts=starts),
        grid=(m // bm,),
        in_specs=in_specs,
        out_specs=[pl.BlockSpec((bm, d), lambda i: (i, 0)), pl.BlockSpec((1, bm, V7X_LANES), lambda i: (0, i, 0))],
        out_shape=[jax.ShapeDtypeStruct((m, d), BF16), jax.ShapeDtypeStruct((1, m, V7X_LANES), F32)],
        compiler_params=_params(("parallel",), len(xs) * _nbytes((bm, d), F32) + _nbytes((bm, d), BF16)),
        name="prep",
    )(*xs)


def _wmm_kernel(*refs, n_w, res_starts, has_aux, has_side, has_norm, emit_stats, act, n_tiles, rows, src_cols,
                side_rows):
    it = iter(refs)
    a_ref = next(it)
    aux_ref = next(it) if has_aux else None
    w_refs = [next(it) for _ in range(n_w)]
    gain_ref = next(it) if has_norm else None
    ss_ref = next(it) if has_norm else None
    auxss_ref = next(it) if has_norm and has_aux else None
    res_refs = [next(it) for _ in res_starts]
    auxres_ref = next(it) if res_starts and has_aux else None
    side_ref = next(it) if has_side else None
    o_ref = next(it)
    ob_ref, oss_ref = (next(it), next(it)) if emit_stats else (None, None)
    oaux_ref = next(it) if has_aux else None
    oauxb_ref, oauxss_ref = (next(it), next(it)) if emit_stats and has_aux else (None, None)
    oside_ref = next(it) if has_side else None
    w_bufs = (next(it), next(it))
    k_dim = w_bufs[0].shape[1]
    g = pl.program_id(0)
    i = pl.program_id(1)
    bn = w_bufs[0].shape[-1]

    if has_side:
        w = side_ref[...]
        row = g * w.shape[0] + lax.broadcasted_iota(jnp.int32, w.shape, 0)
        oside_ref[...] = jnp.where(row < side_rows, w, 0.0).astype(oside_ref.dtype)

    def stage(buf):
        row0 = pl.multiple_of(i * rows, rows)
        for t in range(n_w):
            w = w_refs[t][...]
            if has_norm:
                w = w * jnp.tile(gain_ref[...], (1, bn // V7X_LANES))
            if src_cols % bn:
                col = jnp.minimum(g, n_tiles - 1) * bn + lax.broadcasted_iota(jnp.int32, w.shape, 1)
                w = jnp.where(col < src_cols, w, 0.0)
            buf[t, pl.ds(row0, rows), :] = w.astype(buf.dtype)

    def apply(buf, a, ss, res, out, out_b, out_ss):
        ys = [jnp.dot(a, buf[t], preferred_element_type=F32) for t in range(n_w)]
        if has_norm:
            rstd = _rstd(ss, k_dim)
            ys = [y * rstd for y in ys]
        y = ys[0] * jax.nn.sigmoid(ys[0]) * ys[1] if act else ys[0]
        y = y if res is None else res + y
        out[...] = y.astype(out.dtype)
        if emit_stats:
            out_b[...] = y.astype(out_b.dtype)
            out_ss[0] = _row_sumsq(y)

    def compute(buf):
        res = None
        if res_starts:
            res = res_refs[0][...]
            for start, ref in zip(res_starts[1:], res_refs[1:]):
                res = jnp.where(i >= start, ref[...], res)
        apply(buf, a_ref[...], ss_ref, res, o_ref, ob_ref, oss_ref)
        if has_aux:
            @pl.when(i == 0)
            def _():
                apply(buf, aux_ref[...], auxss_ref, auxres_ref[...] if res_starts else None,
                      oaux_ref, oauxb_ref, oauxss_ref)

    @pl.when(g == 0)
    def _():
        stage(w_bufs[0])

    for parity in range(2):
        @pl.when(jnp.logical_and(g > 0, g % 2 == parity))
        def _():
            stage(w_bufs[parity])
            compute(w_bufs[1 - parity])


def _wmm(a, aux, ws, layer, n_out, bn, out_dtype, *, src_block=None, act=False, res=None, auxres=None, side=None,
         norm=None, emit_stats=False):
    m, k = a.shape
    src_cols = ws[0].shape[2]
    bm = _divisor_tile(m, 1024, BF16_SUBLANES)
    ni, n_tiles = m // bm, n_out // bn
    rows = k // ni
    assert k % ni == 0 and rows % BF16_SUBLANES == 0 and n_out % bn == 0
    assert src_block is None or src_cols % bn == 0
    src = src_block if src_block is not None else (lambda j: j)
    res = list(res) if res is not None else []
    has_aux, has_side, has_norm = aux is not None, side is not None, norm is not None
    ma = aux.shape[0] if has_aux else 0
    res_blocks = [r.shape[0] // bm for r in res]
    assert all(r.shape[0] % bm == 0 for r in res) and sum(res_blocks) == (ni if res else 0)
    res_starts = tuple(sum(res_blocks[:s]) for s in range(len(res)))

    def row_blk(g, i):
        return jnp.where(g == 0, 0, i)

    def col_blk(g):
        return jnp.maximum(g - 1, 0)

    def last_tile(g):
        return jnp.minimum(g, n_tiles - 1)

    in_specs = [pl.BlockSpec((bm, k), lambda g, i: (row_blk(g, i), 0))]
    args = [a]
    if has_aux:
        in_specs.append(pl.BlockSpec((ma, k), lambda g, i: (0, 0)))
        args.append(aux)
    for w in ws:
        in_specs.append(pl.BlockSpec((None, rows, bn), lambda g, i: (layer, i, src(last_tile(g)))))
        args.append(w)
    norm_bytes = 0
    if has_norm:
        gain, ss, ss_aux = norm
        in_specs.append(pl.BlockSpec((rows, V7X_LANES), lambda g, i: (i, 0)))
        args.append(jnp.broadcast_to(gain[:, None], (k, V7X_LANES)))
        in_specs.append(pl.BlockSpec((ss.shape[0], bm, V7X_LANES), lambda g, i: (0, row_blk(g, i), 0)))
        args.append(ss)
        if has_aux:
            in_specs.append(pl.BlockSpec((ss_aux.shape[0], ma, V7X_LANES), lambda g, i: (0, 0, 0)))
            args.append(ss_aux)
        norm_bytes = _nbytes((rows + ss.shape[0] * (bm + ma), V7X_LANES), F32)
    for start, nblk, r in zip(res_starts, res_blocks, res):
        in_specs.append(pl.BlockSpec(
            (bm, bn), lambda g, i, start=start, nblk=nblk: (jnp.clip(row_blk(g, i) - start, 0, nblk - 1), col_blk(g))))
        args.append(r)
    if res and has_aux:
        in_specs.append(pl.BlockSpec((ma, bn), lambda g, i: (0, col_blk(g))))
        args.append(auxres)
    out_specs, out_shape = [], []

    def add_outputs(nrows, blk_rows, row_index):
        out_specs.append(pl.BlockSpec((blk_rows, bn), lambda g, i: (row_index(g, i), col_blk(g))))
        out_shape.append(jax.ShapeDtypeStruct((nrows, n_out), out_dtype))
        if emit_stats:
            out_specs.append(pl.BlockSpec((blk_rows, bn), lambda g, i: (row_index(g, i), col_blk(g))))
            out_shape.append(jax.ShapeDtypeStruct((nrows, n_out), BF16))
            out_specs.append(pl.BlockSpec((1, blk_rows, V7X_LANES), lambda g, i: (col_blk(g), row_index(g, i), 0)))
            out_shape.append(jax.ShapeDtypeStruct((n_tiles, nrows, V7X_LANES), F32))

    add_outputs(m, bm, row_blk)
    if has_aux:
        add_outputs(ma, ma, lambda g, i: 0)
    side_rows = side_bytes = 0
    if has_side:
        side_w, side_out_rows = side
        side_rows, side_cols = side_w.shape[1:]
        sr, sc = side_out_rows // n_tiles, side_cols // ni
        assert side_out_rows % n_tiles == 0 and side_cols % ni == 0 and sr % BF16_SUBLANES == 0 and sc % V7X_LANES == 0
        side_last = -(-side_rows // sr) - 1
        in_specs.append(pl.BlockSpec((None, sr, sc), lambda g, i: (layer, jnp.minimum(g, side_last), i)))
        args.append(side_w)
        out_specs.append(pl.BlockSpec((sr, sc), lambda g, i: (g, i)))
        out_shape.append(jax.ShapeDtypeStruct((side_out_rows + sr, side_cols), BF16))
        side_bytes = _nbytes((sr, sc), F32) + _nbytes((sr, sc), BF16)
    win = (_nbytes((bm + ma, k), BF16) + len(ws) * _nbytes((rows, bn), F32) + side_bytes + norm_bytes
           + _nbytes((bm + ma, bn), out_dtype) + len(res) * _nbytes((bm + ma, bn), F32)
           + emit_stats * (_nbytes((bm + ma, bn), BF16) + _nbytes((bm + ma, V7X_LANES), F32)))
    scratch = _nbytes((2, len(ws), k, bn), BF16)
    return pl.pallas_call(
        functools.partial(_wmm_kernel, n_w=len(ws), res_starts=res_starts, has_aux=has_aux, has_side=has_side,
                          has_norm=has_norm, emit_stats=emit_stats, act=act, n_tiles=n_tiles, rows=rows,
                          src_cols=src_cols, side_rows=side_rows),
        grid=(n_tiles + 1, ni),
        in_specs=in_specs,
        out_specs=out_specs,
        out_shape=out_shape,
        scratch_shapes=[pltpu.VMEM((len(ws), k, bn), BF16), pltpu.VMEM((len(ws), k, bn), BF16)],
        compiler_params=_params(("arbitrary", "arbitrary"), win + scratch // 2),
        name="wmm",
    )(*args)


def _mm_res_kernel(*refs, res_starts, emit_stats, single_k):
    a_ref, b_ref = refs[:2]
    r_refs = refs[2:2 + len(res_starts)]
    o_ref = refs[2 + len(res_starts)]
    stats = refs[3 + len(res_starts):]
    i, j, kk = pl.program_id(0), pl.program_id(1), pl.program_id(2)

    def residual():
        res = r_refs[0][...]
        for start, ref in zip(res_starts[1:], r_refs[1:]):
            res = jnp.where(i >= start, ref[...], res)
        return res

    if single_k:
        o_ref[...] = residual() + jnp.dot(a_ref[...], b_ref[...], preferred_element_type=F32)
    else:
        @pl.when(kk == 0)
        def _():
            o_ref[...] = residual()

        o_ref[...] += jnp.dot(a_ref[...], b_ref[...], preferred_element_type=F32)
    if emit_stats:
        ob_ref, ss_ref = stats

        @pl.when(kk == pl.num_programs(2) - 1)
        def _():
            y = o_ref[...]
            ob_ref[...] = y.astype(ob_ref.dtype)

            @pl.when(j == 0)
            def _():
                ss_ref[0] = _row_sumsq(y)

            @pl.when(j > 0)
            def _():
                ss_ref[0] += _row_sumsq(y)


def _matmul_res(a, b, res, row0=0, nrows=None, emit_stats=False, bn_cap=1024):
    m, k = a.shape
    n = b.shape[1]
    nrows = m if nrows is None else nrows
    res = list(res) if isinstance(res, (list, tuple)) else [res]
    bm = _divisor_tile(math.gcd(nrows, *[r.shape[0] for r in res]), 1024, BF16_SUBLANES)
    bn = _divisor_tile(n, bn_cap, V7X_LANES)
    bk = _divisor_tile(k, 4096, V7X_LANES)
    assert row0 % bm == 0 and (len(res) == 1 or (row0 == 0 and nrows == m))
    off = row0 // bm
    res_blocks = [r.shape[0] // bm for r in res]
    res_starts = tuple(sum(res_blocks[:s]) for s in range(len(res)))

    def res_spec(start, nblk):
        def index(i, j, kk):
            inside = jnp.logical_and(i + off >= start, i + off < start + nblk)
            return jnp.where(inside, i + off - start, 0), jnp.where(inside, j, 0)
        return pl.BlockSpec((bm, bn), index)

    win = _nbytes((bm, bk), BF16) + _nbytes((bk, bn), BF16) + (1 + len(res)) * _nbytes((bm, bn), F32)
    return pl.pallas_call(
        functools.partial(_mm_res_kernel, res_starts=res_starts, emit_stats=emit_stats, single_k=k == bk),
        grid=(nrows // bm, n // bn, k // bk),
        in_specs=[pl.BlockSpec((bm, bk), lambda i, j, kk: (i + off, kk)),
                  pl.BlockSpec((bk, bn), lambda i, j, kk: (kk, j))]
        + [res_spec(start, nblk) for start, nblk in zip(res_starts, res_blocks)],
        out_specs=[pl.BlockSpec((bm, bn), lambda i, j, kk: (i, j))] + emit_stats * [
            pl.BlockSpec((bm, bn), lambda i, j, kk: (i, j)),
            pl.BlockSpec((1, bm, V7X_LANES), lambda i, j, kk: (0, i, 0))],
        out_shape=[jax.ShapeDtypeStruct((nrows, n), F32)] + emit_stats * [
            jax.ShapeDtypeStruct((nrows, n), BF16), jax.ShapeDtypeStruct((1, nrows, V7X_LANES), F32)],
        compiler_params=_params(("parallel", "arbitrary", "arbitrary"), win + _nbytes((bm, bn), BF16)),
        name="matmul_res",
    )(a, b, *res)


def _t5_bucket(rel, n_buckets):
    half = n_buckets // 2
    exact = half // 2
    n = jnp.abs(rel)
    n_f = jnp.maximum(n, 1).astype(F32)
    large = exact + (jnp.log(n_f / exact) / math.log(MAX_DISTANCE / exact) * (half - exact)).astype(jnp.int32)
    large = jnp.minimum(large, half - 1)
    return jnp.where(rel > 0, half, 0) + jnp.where(n < exact, n, large)


def _bucket_tables(n_meta, n_buckets):
    qi = jnp.arange(BLOCK)[:, None]
    sj = jnp.arange(3 * BLOCK)[None, :]
    rel_band = sj - BLOCK - qi
    band = jnp.where(jnp.abs(rel_band) <= BLOCK, _t5_bucket(rel_band, n_buckets), -1)
    sink = jnp.full((BLOCK, 1), n_buckets, jnp.int32)
    pad = jnp.full((BLOCK, BLOCK - n_meta - 1), -1, jnp.int32)
    mk = jnp.arange(n_meta)[None, :]
    meta_first = _t5_bucket(mk - (n_meta + qi), n_buckets)
    meta_far = jnp.full((BLOCK, n_meta), n_buckets // 2 - 1, jnp.int32)
    tok = jnp.concatenate([band, meta_far, sink, pad, meta_first, sink, pad], axis=1)[None].astype(jnp.int32)

    mq = jnp.arange(n_meta)[:, None]
    tk = jnp.arange(BLOCK)[None, :]
    rel_tok = n_meta + tk - mq
    mband = jnp.where(jnp.abs(rel_tok) <= BLOCK, _t5_bucket(rel_tok, n_buckets), -1)
    mmeta = _t5_bucket(mk - mq, n_buckets)
    met = jnp.concatenate([mband, mmeta, sink[:n_meta], pad[:n_meta]], axis=1).astype(jnp.int32)
    return tok, met[None]


def _bias_kernel(tab_ref, bkt_ref, o_ref, *, n_ids, token_variants):
    h = pl.program_id(0)
    bkt = bkt_ref[...]
    acc = jnp.full(bkt.shape, -jnp.inf, F32)
    for b in range(n_ids):
        acc = jnp.where(bkt == b, tab_ref[b, h], acc)
    acc = acc * LOG2E
    if token_variants:
        band, far, first = acc[0, :, :3 * BLOCK], acc[0, :, 3 * BLOCK:4 * BLOCK], acc[0, :, 4 * BLOCK:]
        masked = jnp.full((BLOCK, BLOCK), -jnp.inf, F32)
        o_ref[0, 0, 0] = jnp.concatenate([band, far], axis=1)
        o_ref[0, 1, 0] = jnp.concatenate([masked, band[:, BLOCK:], first], axis=1)
        o_ref[0, 2, 0] = jnp.concatenate([band[:, :2 * BLOCK], masked, far], axis=1)
    else:
        o_ref[0, :, 0] = acc


def _bias_tiles(table, bkt, token_variants=False):
    n_ids, n_heads = table.shape
    v, r, c = (3, BLOCK, 4 * BLOCK) if token_variants else bkt.shape
    return pl.pallas_call(
        functools.partial(_bias_kernel, n_ids=n_ids, token_variants=token_variants),
        grid=(n_heads,),
        in_specs=[pl.BlockSpec(memory_space=pltpu.SMEM),
                  pl.BlockSpec(bkt.shape, lambda h: (0, 0, 0))],
        out_specs=pl.BlockSpec((1, v, 1, r, c), lambda h: (h // GROUP, 0, h % GROUP, 0, 0)),
        out_shape=jax.ShapeDtypeStruct((n_heads // GROUP, v, GROUP, r, c), F32),
        compiler_params=_params(("parallel",), 2 * _nbytes(bkt.shape, F32)),
        name="bias_tiles",
    )(table, bkt)


def _softmax_pv(q, k, v, bias2):
    s = lax.dot_general(q, k, (((1,), (1,)), ((), ())), preferred_element_type=F32) + bias2
    m = jnp.max(s, axis=-1, keepdims=True)
    p = jnp.exp2(s - m)
    denom = jnp.sum(p, axis=-1, keepdims=True)
    return jnp.dot(p.astype(BF16), v, preferred_element_type=F32) * (1.0 / denom)


def _fill_meta_keys(km_scr, vm_scr, km_ref, vm_ref, kg, n_meta):
    km_scr[...] = jnp.zeros(km_scr.shape, BF16)
    vm_scr[...] = jnp.zeros(vm_scr.shape, BF16)
    km_scr[0:n_meta] = _rms(km_ref[...].astype(F32), kg).astype(BF16)
    vm_scr[0:n_meta] = vm_ref[...].astype(BF16)


def _attn_kernel(qg_ref, kg_ref, bias_ref, q_ref, kp_ref, kc_ref, kn_ref, vp_ref, vc_ref, vn_ref,
                 km_ref, vm_ref, o_ref, q_scr, k_scr, v_scr, km_scr, vm_scr, *, chunks, seq_chunks, n_meta, scale):
    r = pl.program_id(1)
    qg = qg_ref[...] * (scale * LOG2E)
    kg = kg_ref[...]
    body = chunks * BLOCK
    stack = GROUP * BLOCK

    k_scr[0:BLOCK] = _rms(kp_ref[...].astype(F32), kg).astype(BF16)
    k_scr[BLOCK:BLOCK + body] = _rms(kc_ref[...].astype(F32), kg).astype(BF16)
    k_scr[BLOCK + body:2 * BLOCK + body] = _rms(kn_ref[...].astype(F32), kg).astype(BF16)
    v_scr[0:BLOCK] = vp_ref[...].astype(BF16)
    v_scr[BLOCK:BLOCK + body] = vc_ref[...].astype(BF16)
    v_scr[BLOCK + body:2 * BLOCK + body] = vn_ref[...].astype(BF16)
    _fill_meta_keys(km_scr, vm_scr, km_ref, vm_ref, kg, n_meta)
    for g in range(GROUP):
        qn = _rms(q_ref[:, g * HEAD_DIM:(g + 1) * HEAD_DIM].astype(F32), qg).astype(BF16)
        for c in range(chunks):
            q_scr[(c * GROUP + g) * BLOCK:(c * GROUP + g + 1) * BLOCK] = qn[c * BLOCK:(c + 1) * BLOCK]
    k_meta = km_scr[...]
    v_meta = vm_scr[...]

    def chunk(cc, carry):
        gch = r * chunks + cc
        is_first = functools.reduce(jnp.logical_or, [gch == s0 for s0, _ in seq_chunks])
        is_last = functools.reduce(jnp.logical_or, [gch == s0 + n - 1 for s0, n in seq_chunks])
        var = jnp.where(is_first, 1, jnp.where(is_last, 2, 0))
        row0 = pl.multiple_of(cc * BLOCK, BLOCK)
        k_all = jnp.concatenate([k_scr[pl.ds(row0, 3 * BLOCK), :], k_meta], axis=0)
        v_all = jnp.concatenate([v_scr[pl.ds(row0, 3 * BLOCK), :], v_meta], axis=0)
        q = q_scr[pl.ds(pl.multiple_of(cc * stack, stack), stack), :]
        bias = bias_ref[0, var].reshape(stack, 4 * BLOCK)
        o = _softmax_pv(q, k_all, v_all, bias)
        for g in range(GROUP):
            o_ref[pl.ds(row0, BLOCK), g * HEAD_DIM:(g + 1) * HEAD_DIM] = o[g * BLOCK:(g + 1) * BLOCK]
        return carry

    lax.fori_loop(0, chunks, chunk, 0, unroll=True)


def _seq_of_chunk(c, seq_chunks):
    s = 0
    for s0, _ in seq_chunks[1:]:
        s = s + (c >= s0).astype(jnp.int32)
    return s


def _attention(z, zm, bias_tok, q_g, k_g, *, seq_chunks, n_meta, d_model, k_col, v_col):
    m = z.shape[0]
    n_chunks = m // BLOCK
    kv_heads = d_model // (GROUP * HEAD_DIM)
    chunks = math.gcd(16, *[n for _, n in seq_chunks])
    body = chunks * BLOCK
    qw = GROUP * HEAD_DIM
    kb, vb = k_col // HEAD_DIM, v_col // HEAD_DIM
    seq_of = functools.partial(_seq_of_chunk, seq_chunks=seq_chunks)

    def halo_prev(h, r):
        return jnp.maximum(r * chunks - 1, 0)

    def halo_next(h, r):
        return jnp.minimum(r * chunks + chunks, n_chunks - 1)

    in_specs = [
        pl.BlockSpec((1, HEAD_DIM), lambda h, r: (0, 0)),
        pl.BlockSpec((1, HEAD_DIM), lambda h, r: (0, 0)),
        pl.BlockSpec((1, 3, GROUP, BLOCK, 4 * BLOCK), lambda h, r: (h, 0, 0, 0, 0)),
        pl.BlockSpec((body, qw), lambda h, r: (r, h)),
        pl.BlockSpec((BLOCK, HEAD_DIM), lambda h, r: (halo_prev(h, r), kb + h)),
        pl.BlockSpec((body, HEAD_DIM), lambda h, r: (r, kb + h)),
        pl.BlockSpec((BLOCK, HEAD_DIM), lambda h, r: (halo_next(h, r), kb + h)),
        pl.BlockSpec((BLOCK, HEAD_DIM), lambda h, r: (halo_prev(h, r), vb + h)),
        pl.BlockSpec((body, HEAD_DIM), lambda h, r: (r, vb + h)),
        pl.BlockSpec((BLOCK, HEAD_DIM), lambda h, r: (halo_next(h, r), vb + h)),
        pl.BlockSpec((n_meta, HEAD_DIM), lambda h, r: (seq_of(r * chunks), kb + h)),
        pl.BlockSpec((n_meta, HEAD_DIM), lambda h, r: (seq_of(r * chunks), vb + h)),
    ]
    win = (_nbytes((3, GROUP, BLOCK, 4 * BLOCK), F32) + 2 * _nbytes((body, qw), F32)
           + 4 * _nbytes((body + 2 * BLOCK, HEAD_DIM), F32))
    return pl.pallas_call(
        functools.partial(_attn_kernel, chunks=chunks, seq_chunks=seq_chunks, n_meta=n_meta,
                          scale=HEAD_DIM ** -0.5),
        grid=(kv_heads, m // body),
        in_specs=in_specs,
        out_specs=pl.BlockSpec((body, qw), lambda h, r: (r, h)),
        out_shape=jax.ShapeDtypeStruct((m, d_model), F32),
        scratch_shapes=[pltpu.VMEM((body * GROUP, HEAD_DIM), BF16),
                        pltpu.VMEM((body + 2 * BLOCK, HEAD_DIM), BF16),
                        pltpu.VMEM((body + 2 * BLOCK, HEAD_DIM), BF16),
                        pltpu.VMEM((BLOCK, HEAD_DIM), BF16),
                        pltpu.VMEM((BLOCK, HEAD_DIM), BF16)],
        compiler_params=_params(("parallel", "arbitrary"), win),
        name="attention",
    )(q_g.reshape(1, HEAD_DIM), k_g.reshape(1, HEAD_DIM), bias_tok, z, z, z, z, z, z, z, zm, zm)


def _attn_meta_kernel(qg_ref, kg_ref, bias_ref, q_ref, k1_ref, v1_ref, km_ref, vm_ref, o_ref,
                      km_scr, vm_scr, *, n_meta, scale):
    qg = qg_ref[...] * (scale * LOG2E)
    kg = kg_ref[...]
    _fill_meta_keys(km_scr, vm_scr, km_ref, vm_ref, kg, n_meta)
    k_all = jnp.concatenate([_rms(k1_ref[...].astype(F32), kg).astype(BF16), km_scr[...]], axis=0)
    v_all = jnp.concatenate([v1_ref[...].astype(BF16), vm_scr[...]], axis=0)
    q = q_ref[...].astype(F32)
    qn = jnp.concatenate([_rms(q[:, g * HEAD_DIM:(g + 1) * HEAD_DIM], qg).astype(BF16) for g in range(GROUP)], axis=0)
    o = _softmax_pv(qn, k_all, v_all, bias_ref[0, 0].reshape(GROUP * n_meta, 2 * BLOCK))
    for g in range(GROUP):
        o_ref[:, g * HEAD_DIM:(g + 1) * HEAD_DIM] = o[g * n_meta:(g + 1) * n_meta]


def _attention_meta(z, zm, bias_met, q_g, k_g, *, seq_chunks, n_meta, d_model, k_col, v_col):
    n_seq = len(seq_chunks)
    kv_heads = d_model // (GROUP * HEAD_DIM)
    qw = GROUP * HEAD_DIM
    kb, vb = k_col // HEAD_DIM, v_col // HEAD_DIM

    def first_chunk(s):
        c = 0
        for i, (s0, _) in enumerate(seq_chunks):
            c = c + jnp.where(s == i, s0, 0)
        return c

    in_specs = [
        pl.BlockSpec((1, HEAD_DIM), lambda s, h: (0, 0)),
        pl.BlockSpec((1, HEAD_DIM), lambda s, h: (0, 0)),
        pl.BlockSpec((1, 1, GROUP, n_meta, 2 * BLOCK), lambda s, h: (h, 0, 0, 0, 0)),
        pl.BlockSpec((n_meta, qw), lambda s, h: (s, h)),
        pl.BlockSpec((BLOCK, HEAD_DIM), lambda s, h: (first_chunk(s), kb + h)),
        pl.BlockSpec((BLOCK, HEAD_DIM), lambda s, h: (first_chunk(s), vb + h)),
        pl.BlockSpec((n_meta, HEAD_DIM), lambda s, h: (s, kb + h)),
        pl.BlockSpec((n_meta, HEAD_DIM), lambda s, h: (s, vb + h)),
    ]
    return pl.pallas_call(
        functools.partial(_attn_meta_kernel, n_meta=n_meta, scale=HEAD_DIM ** -0.5),
        grid=(n_seq, kv_heads),
        in_specs=in_specs,
        out_specs=pl.BlockSpec((n_meta, qw), lambda s, h: (s, h)),
        out_shape=jax.ShapeDtypeStruct((n_seq * n_meta, d_model), F32),
        scratch_shapes=[pltpu.VMEM((BLOCK, HEAD_DIM), BF16), pltpu.VMEM((BLOCK, HEAD_DIM), BF16)],
        compiler_params=_params(("parallel", "parallel"), 1 << 20),
        name="attention_meta",
    )(q_g.reshape(1, HEAD_DIM), k_g.reshape(1, HEAD_DIM), bias_met, zm, z, z, zm, zm)


def _merge_math(attn, gb, gc, hc, ga, gcv, u_prev, u_next, cw, na, nc):
    u = gc * hc
    rows = u.shape[0]
    ridx = lax.broadcasted_iota(jnp.int32, u.shape, 0)
    up = jnp.where(ridx == 0, u_prev, pltpu.roll(u, 1, axis=0))
    un = jnp.where(ridx == rows - 1, u_next, pltpu.roll(u, rows - 1, axis=0))
    conv = gb * (cw[0:1] * up + cw[1:2] * u + cw[2:3] * un)
    return jax.nn.sigmoid(ga) * _rms(attn, na) + jax.nn.sigmoid(gcv) * _rms(conv, nc)


def _merge_kernel(attn_ref, gb_ref, gc_ref, hc_ref, ga_ref, gcv_ref, gcp_ref, hcp_ref, gcn_ref, hcn_ref,
                  gcm_ref, hcm_ref, cw_ref, na_ref, nc_ref, o_ref, conv_scr, *, seq_chunks):
    c = pl.program_id(0)
    is_first = functools.reduce(jnp.logical_or, [c == s0 for s0, _ in seq_chunks])
    is_last = functools.reduce(jnp.logical_or, [c == s0 + n - 1 for s0, n in seq_chunks])
    last = HALO_ROWS - 1
    rows, d = o_ref.shape
    n_tiles = d // V7X_LANES
    ridx = lax.broadcasted_iota(jnp.int32, (rows, V7X_LANES), 0)

    def cols(ct):
        return pl.ds(pl.multiple_of(ct * V7X_LANES, V7X_LANES), V7X_LANES)

    def f32(ref, sl):
        return ref[:, sl].astype(F32)

    def conv_pass(ct, carry):
        ss_attn, ss_conv = carry
        sl = cols(ct)
        u = f32(gc_ref, sl) * f32(hc_ref, sl)
        u_prev_tok = (f32(gcp_ref, sl) * f32(hcp_ref, sl))[last:last + 1]
        u_prev_meta = (f32(gcm_ref, sl) * f32(hcm_ref, sl))[last:last + 1]
        u_prev = jnp.where(is_first, u_prev_meta, u_prev_tok)
        u_next_tok = (f32(gcn_ref, sl) * f32(hcn_ref, sl))[0:1]
        u_next = jnp.where(is_last, jnp.zeros_like(u_next_tok), u_next_tok)
        up = jnp.where(ridx == 0, u_prev, pltpu.roll(u, 1, axis=0))
        un = jnp.where(ridx == rows - 1, u_next, pltpu.roll(u, rows - 1, axis=0))
        cw = cw_ref[:, sl]
        conv = f32(gb_ref, sl) * (cw[0:1] * up + cw[1:2] * u + cw[2:3] * un)
        conv_scr[:, sl] = conv
        attn = attn_ref[:, sl]
        return ss_attn + attn * attn, ss_conv + conv * conv

    zero = jnp.zeros((rows, V7X_LANES), F32)
    ss_attn, ss_conv = lax.fori_loop(0, n_tiles, conv_pass, (zero, zero), unroll=2)
    r_attn = lax.rsqrt(jnp.sum(ss_attn, axis=-1, keepdims=True) / d + EPS)
    r_conv = lax.rsqrt(jnp.sum(ss_conv, axis=-1, keepdims=True) / d + EPS)

    def gate_pass(ct, carry):
        sl = cols(ct)
        a = attn_ref[:, sl] * r_attn * na_ref[:, sl]
        c = conv_scr[:, sl] * r_conv * nc_ref[:, sl]
        o_ref[:, sl] = (jax.nn.sigmoid(f32(ga_ref, sl)) * a + jax.nn.sigmoid(f32(gcv_ref, sl)) * c).astype(o_ref.dtype)
        return carry

    lax.fori_loop(0, n_tiles, gate_pass, 0, unroll=2)


def _merge(attn, z, zm, conv_w, norm_a, norm_c, *, seq_chunks, n_meta, cols):
    m, d = attn.shape
    n_chunks = m // BLOCK
    per_chunk = BLOCK // HALO_ROWS
    per_meta = n_meta // HALO_ROWS
    gb, gc, hc, ga, gcv = [c // d for c in cols]
    seq_of = functools.partial(_seq_of_chunk, seq_chunks=seq_chunks)

    def main(col):
        return pl.BlockSpec((BLOCK, d), lambda c: (c, col))

    def prev_rows(col):
        return pl.BlockSpec((HALO_ROWS, d), lambda c: (jnp.maximum(c * per_chunk - 1, 0), col))

    def next_rows(col):
        return pl.BlockSpec((HALO_ROWS, d), lambda c: (jnp.minimum(c + 1, n_chunks - 1) * per_chunk, col))

    def meta_rows(col):
        return pl.BlockSpec((HALO_ROWS, d), lambda c: (seq_of(c) * per_meta + per_meta - 1, col))

    def row(nrows):
        return pl.BlockSpec((nrows, d), lambda c: (0, 0))

    win = 6 * _nbytes((BLOCK, d), F32) + _nbytes((BLOCK, d), BF16) + 8 * _nbytes((HALO_ROWS, d), F32)
    return pl.pallas_call(
        functools.partial(_merge_kernel, seq_chunks=seq_chunks),
        grid=(n_chunks,),
        in_specs=[main(0), main(gb), main(gc), main(hc), main(ga), main(gcv),
                  prev_rows(gc), prev_rows(hc), next_rows(gc), next_rows(hc), meta_rows(gc), meta_rows(hc),
                  row(3), row(1), row(1)],
        out_specs=pl.BlockSpec((BLOCK, d), lambda c: (c, 0)),
        out_shape=jax.ShapeDtypeStruct((m, d), BF16),
        scratch_shapes=[pltpu.VMEM((BLOCK, d), F32)],
        compiler_params=_params(("parallel",), win),
        name="merge",
    )(attn, z, z, z, z, z, z, z, z, z, zm, zm, conv_w, norm_a.reshape(1, d), norm_c.reshape(1, d))


def _merge_meta_kernel(attn_ref, gb_ref, gc_ref, hc_ref, ga_ref, gcv_ref, gcn_ref, hcn_ref,
                       cw_ref, na_ref, nc_ref, o_ref):
    f32 = lambda ref: ref[...].astype(F32)
    u_next = (f32(gcn_ref) * f32(hcn_ref))[0:1]
    u_prev = jnp.zeros_like(u_next)
    o_ref[...] = _merge_math(attn_ref[...], f32(gb_ref), f32(gc_ref), f32(hc_ref), f32(ga_ref), f32(gcv_ref),
                             u_prev, u_next, cw_ref[...], na_ref[...], nc_ref[...]).astype(o_ref.dtype)


def _merge_meta(attn_m, z, zm, conv_w, norm_a, norm_c, *, seq_chunks, n_meta, cols):
    mm, d = attn_m.shape
    per_chunk = BLOCK // HALO_ROWS
    gb, gc, hc, ga, gcv = [c // d for c in cols]

    def first_rows(s):
        r = 0
        for i, (s0, _) in enumerate(seq_chunks):
            r = r + jnp.where(s == i, s0 * per_chunk, 0)
        return r

    def main(col):
        return pl.BlockSpec((n_meta, d), lambda s: (s, col))

    def next_rows(col):
        return pl.BlockSpec((HALO_ROWS, d), lambda s: (first_rows(s), col))

    def row(nrows):
        return pl.BlockSpec((nrows, d), lambda s: (0, 0))

    win = 7 * _nbytes((n_meta, d), F32) + 2 * _nbytes((HALO_ROWS, d), F32)
    return pl.pallas_call(
        _merge_meta_kernel,
        grid=(len(seq_chunks),),
        in_specs=[main(0), main(gb), main(gc), main(hc), main(ga), main(gcv), next_rows(gc), next_rows(hc),
                  row(3), row(1), row(1)],
        out_specs=pl.BlockSpec((n_meta, d), lambda s: (s, 0)),
        out_shape=jax.ShapeDtypeStruct((mm, d), BF16),
        compiler_params=_params(("parallel",), win),
        name="merge_meta",
    )(attn_m, zm, zm, zm, zm, zm, z, z, conv_w, norm_a.reshape(1, d), norm_c.reshape(1, d))


def kernel(x_prompt, x_sample, meta_tokens, rel_bias, norm1_g, w_in, q_norm_g, k_norm_g, attn_sink, conv_w,
           branch_norm_a, branch_norm_c, w_out, norm2_g, w_ffn_gate, w_ffn_up, w_ffn_down):
    d = x_prompt.shape[-1]
    n_meta = meta_tokens.shape[0]
    n_buckets = rel_bias.shape[0]
    depth, _, in_dim = w_in.shape
    ffn = w_ffn_gate.shape[-1]
    kv_dim = (in_dim - 6 * d) // 2
    assert d % (GROUP * HEAD_DIM) == 0 and kv_dim == d // GROUP and attn_sink.shape[1] * HEAD_DIM == d
    assert n_meta % HALO_ROWS == 0 and n_meta < BLOCK

    groups = (x_prompt, x_sample)
    seq_chunks = []
    for xg in groups:
        assert xg.shape[1] % BLOCK == 0 and xg.shape[1] >= 2 * BLOCK
        for _ in range(xg.shape[0]):
            start = seq_chunks[-1][0] + seq_chunks[-1][1] if seq_chunks else 0
            seq_chunks.append((start, xg.shape[1] // BLOCK))
    seq_chunks = tuple(seq_chunks)
    n_seq = len(seq_chunks)

    x_parts = [xg.reshape(-1, d) for xg in groups]
    xm = jnp.tile(meta_tokens.astype(F32), (n_seq, 1))

    cols = tuple(d * i for i in range(1, 6))
    k_col, v_col = 6 * d, 6 * d + kv_dim
    ffn_tile = 512
    ffn_p = -(-ffn // ffn_tile) * ffn_tile
    out_tile = 512
    in_tiles = in_dim // kv_dim
    wo_rows = -(-d // (in_tiles * BF16_SUBLANES)) * BF16_SUBLANES * in_tiles
    n_q, n_rest = d // kv_dim, 5 * d // kv_dim

    def regroup(j):
        return jnp.where(j < n_q, j, jnp.where(j < n_q + n_rest, j + 2, j - n_rest))

    bkt_tok, bkt_met = _bucket_tables(n_meta, n_buckets)

    geo = dict(seq_chunks=seq_chunks, n_meta=n_meta)
    xb, ss = _prep(x_parts)
    xmb, ssm = _prep([xm])
    for l in range(depth):
        with_meta = l < depth - 1
        table = jnp.concatenate([rel_bias, attn_sink[l][None]], axis=0)
        z, zm, wo_b = _wmm(xb, xmb, [w_in], l, in_dim, kv_dim, BF16, src_block=regroup, norm=(norm1_g[l], ss, ssm),
                           side=(w_out, wo_rows))
        attn = _attention(z, zm, _bias_tiles(table, bkt_tok, token_variants=True), q_norm_g[l], k_norm_g[l],
                          d_model=d, k_col=k_col, v_col=v_col, **geo)
        merged = _merge(attn, z, zm, conv_w[l], branch_norm_a[l], branch_norm_c[l], cols=cols, **geo)
        merged_m = None
        if with_meta:
            attn_m = _attention_meta(z, zm, _bias_tiles(table, bkt_met), q_norm_g[l], k_norm_g[l],
                                     d_model=d, k_col=k_col, v_col=v_col, **geo)
            merged_m = _merge_meta(attn_m, z, zm, conv_w[l], branch_norm_a[l], branch_norm_c[l], cols=cols, **geo)
        h, hb, hss = _matmul_res(merged, wo_b, x_parts, emit_stats=True, bn_cap=out_tile)
        hm, hmb, hmss = _matmul_res(merged_m, wo_b, xm, emit_stats=True) if with_meta else (None, None, None)
        t, *tm, wd_b = _wmm(hb, hmb, [w_ffn_gate, w_ffn_up], l, ffn_p, ffn_tile, BF16, act=True,
                            side=(w_ffn_down, ffn_p), norm=(norm2_g[l], hss, hmss))
        if with_meta:
            x, xb, ss = _matmul_res(t, wd_b, h, emit_stats=True)
            x_parts = [x]
            xm, xmb, ssm = _matmul_res(tm[0], wd_b, hm, emit_stats=True)

    outs = []
    row = 0
    for xg in groups:
        nrows = xg.shape[0] * xg.shape[1]
        outs.append(_matmul_res(t, wd_b, h, row, nrows)[0].reshape(xg.shape))
        row += nrows
    return tuple(outs)
```

```python
import functools
import math

import jax
import jax.numpy as jnp
from jax import lax
from jax.experimental import pallas as pl
from jax.experimental.pallas import tpu as pltpu

HEAD_DIM = 128
GROUP = 4
BLOCK = 128
MAX_DISTANCE = 128
EPS = 1e-6
LOG2E = math.log2(math.e)

V7X_VMEM_BYTES = 64 << 20
V7X_LANES = 128
BF16_SUBLANES = 16
F32_SUBLANES = 8
HALO_ROWS = BF16_SUBLANES
VMEM_SLACK_BYTES = 10 << 20

F32 = jnp.float32
BF16 = jnp.bfloat16


def _params(dims, window_bytes):
    limit = min(2 * window_bytes + VMEM_SLACK_BYTES, V7X_VMEM_BYTES - (6 << 20))
    return pltpu.CompilerParams(dimension_semantics=dims, vmem_limit_bytes=int(limit))


def _divisor_tile(n, cap, unit):
    if n <= cap:
        return n
    t = (cap // unit) * unit
    while t >= unit:
        if n % t == 0:
            return t
        t -= unit
    raise ValueError(f"no tile for {n}")


def _nbytes(shape, dtype):
    return math.prod(shape) * jnp.dtype(dtype).itemsize


def _rms(x, g):
    ms = jnp.mean(x * x, axis=-1, keepdims=True)
    return x * lax.rsqrt(ms + EPS) * g


def _row_sumsq(x):
    return jnp.broadcast_to(jnp.sum(x * x, axis=-1, keepdims=True), (x.shape[0], V7X_LANES))


def _rstd(ss_ref, width):
    ss = ss_ref[0]
    for p in range(1, ss_ref.shape[0]):
        ss = ss + ss_ref[p]
    return lax.rsqrt(ss / width + EPS)


def _prep_kernel(*refs, starts):
    x_refs, o_ref, ss_ref = refs[:len(starts)], refs[-2], refs[-1]
    i = pl.program_id(0)
    ends = starts[1:] + (None,)
    for x_ref, lo, hi in zip(x_refs, starts, ends):
        in_seg = i >= lo if hi is None else jnp.logical_and(i >= lo, i < hi)

        @pl.when(in_seg)
        def _():
            x = x_ref[...]
            o_ref[...] = x.astype(o_ref.dtype)
            ss_ref[0] = _row_sumsq(x)


def _prep(xs):
    d = xs[0].shape[1]
    m = sum(x.shape[0] for x in xs)
    bm = _divisor_tile(math.gcd(*[x.shape[0] for x in xs]), 256, BF16_SUBLANES)
    blocks = [x.shape[0] // bm for x in xs]
    starts = tuple(sum(blocks[:s]) for s in range(len(xs)))
    in_specs = [pl.BlockSpec((bm, d), lambda i, start=start, nblk=nblk: (jnp.clip(i - start, 0, nblk - 1), 0))
                for start, nblk in zip(starts, blocks)]
    return pl.pallas_call(
        functools.partial(_prep_kernel, starts=starts),
        grid=(m // bm,),
        in_specs=in_specs,
        out_specs=[pl.BlockSpec((bm, d), lambda i: (i, 0)), pl.BlockSpec((1, bm, V7X_LANES), lambda i: (0, i, 0))],
        out_shape=[jax.ShapeDtypeStruct((m, d), BF16), jax.ShapeDtypeStruct((1, m, V7X_LANES), F32)],
        compiler_params=_params(("parallel",), len(xs) * _nbytes((bm, d), F32) + _nbytes((bm, d), BF16)),
        name="prep",
    )(*xs)


def _wmm_kernel(*refs, n_w, res_starts, has_aux, has_side, has_norm, emit_stats, act, n_tiles, rows, src_cols,
                side_rows):
    it = iter(refs)
    a_ref = next(it)
    aux_ref = next(it) if has_aux else None
    w_refs = [next(it) for _ in range(n_w)]
    gain_ref = next(it) if has_norm else None
    ss_ref = next(it) if has_norm else None
    auxss_ref = next(it) if has_norm and has_aux else None
    res_refs = [next(it) for _ in res_starts]
    auxres_ref = next(it) if res_starts and has_aux else None
    side_ref = next(it) if has_side else None
    o_ref = next(it)
    ob_ref, oss_ref = (next(it), next(it)) if emit_stats else (None, None)
    oaux_ref = next(it) if has_aux else None
    oauxb_ref, oauxss_ref = (next(it), next(it)) if emit_stats and has_aux else (None, None)
    oside_ref = next(it) if has_side else None
    w_bufs = (next(it), next(it))
    k_dim = w_bufs[0].shape[1]
    g = pl.program_id(0)
    i = pl.program_id(1)
    bn = w_bufs[0].shape[-1]

    if has_side:
        w = side_ref[...]
        row = g * w.shape[0] + lax.broadcasted_iota(jnp.int32, w.shape, 0)
        oside_ref[...] = jnp.where(row < side_rows, w, 0.0).astype(oside_ref.dtype)

    def stage(buf):
        row0 = pl.multiple_of(i * rows, rows)
        for t in range(n_w):
            w = w_refs[t][...]
            if has_norm:
                w = w * jnp.tile(gain_ref[...], (1, bn // V7X_LANES))
            if src_cols % bn:
                col = jnp.minimum(g, n_tiles - 1) * bn + lax.broadcasted_iota(jnp.int32, w.shape, 1)
                w = jnp.where(col < src_cols, w, 0.0)
            buf[t, pl.ds(row0, rows), :] = w.astype(buf.dtype)

    def apply(buf, ncols, a, ss, res, out, out_b, out_ss):
        ys = [jnp.dot(a, buf[t, :, 0:ncols], preferred_element_type=F32) for t in range(n_w)]
        if has_norm:
            rstd = jnp.tile(_rstd(ss, k_dim), (1, ncols // V7X_LANES))
            ys = [y * rstd for y in ys]
        y = ys[0] * jax.nn.sigmoid(ys[0]) * ys[1] if act else ys[0]
        y = y if res is None else res + y
        out[:, 0:ncols] = y.astype(out.dtype)
        if ncols < bn:
            out[:, ncols:] = jnp.zeros((y.shape[0], bn - ncols), out.dtype)
        if emit_stats:
            out_b[...] = y.astype(out_b.dtype)
            out_ss[0] = _row_sumsq(y)

    def compute(buf, ncols):
        res = None
        if res_starts:
            res = res_refs[0][...]
            for start, ref in zip(res_starts[1:], res_refs[1:]):
                res = jnp.where(i >= start, ref[...], res)
        apply(buf, ncols, a_ref[...], ss_ref, res, o_ref, ob_ref, oss_ref)
        if has_aux:
            @pl.when(i == 0)
            def _():
                apply(buf, ncols, aux_ref[...], auxss_ref, auxres_ref[...] if res_starts else None,
                      oaux_ref, oauxb_ref, oauxss_ref)

    @pl.when(g == 0)
    def _():
        stage(w_bufs[0])

    last_cols = src_cols % bn
    full_steps = n_tiles - 1 if last_cols else n_tiles
    for parity in range(2):
        @pl.when(jnp.logical_and(jnp.logical_and(g > 0, g <= full_steps), g % 2 == parity))
        def _():
            stage(w_bufs[parity])
            compute(w_bufs[1 - parity], bn)

    if last_cols:
        assert last_cols % V7X_LANES == 0 and not res_starts and not emit_stats

        @pl.when(g == n_tiles)
        def _():
            compute(w_bufs[(n_tiles - 1) % 2], last_cols)


def _wmm(a, aux, ws, layer, n_out, bn, out_dtype, *, src_block=None, act=False, res=None, auxres=None, side=None,
         norm=None, emit_stats=False):
    m, k = a.shape
    src_cols = ws[0].shape[2]
    bm = _divisor_tile(m, 1024, BF16_SUBLANES)
    ni, n_tiles = m // bm, n_out // bn
    rows = k // ni
    assert k % ni == 0 and rows % BF16_SUBLANES == 0 and n_out % bn == 0
    assert src_block is None or src_cols % bn == 0
    src = src_block if src_block is not None else (lambda j: j)
    res = list(res) if res is not None else []
    has_aux, has_side, has_norm = aux is not None, side is not None, norm is not None
    ma = aux.shape[0] if has_aux else 0
    res_blocks = [r.shape[0] // bm for r in res]
    assert all(r.shape[0] % bm == 0 for r in res) and sum(res_blocks) == (ni if res else 0)
    res_starts = tuple(sum(res_blocks[:s]) for s in range(len(res)))

    def row_blk(g, i):
        return jnp.where(g == 0, 0, i)

    def col_blk(g):
        return jnp.maximum(g - 1, 0)

    def last_tile(g):
        return jnp.minimum(g, n_tiles - 1)

    in_specs = [pl.BlockSpec((bm, k), lambda g, i: (row_blk(g, i), 0))]
    args = [a]
    if has_aux:
        in_specs.append(pl.BlockSpec((ma, k), lambda g, i: (0, 0)))
        args.append(aux)
    for w in ws:
        in_specs.append(pl.BlockSpec((None, rows, bn), lambda g, i: (layer, i, src(last_tile(g)))))
        args.append(w)
    norm_bytes = 0
    if has_norm:
        gain, ss, ss_aux = norm
        in_specs.append(pl.BlockSpec((rows, V7X_LANES), lambda g, i: (i, 0)))
        args.append(jnp.broadcast_to(gain[:, None], (k, V7X_LANES)))
        in_specs.append(pl.BlockSpec((ss.shape[0], bm, V7X_LANES), lambda g, i: (0, row_blk(g, i), 0)))
        args.append(ss)
        if has_aux:
            in_specs.append(pl.BlockSpec((ss_aux.shape[0], ma, V7X_LANES), lambda g, i: (0, 0, 0)))
            args.append(ss_aux)
        norm_bytes = _nbytes((rows + ss.shape[0] * (bm + ma), V7X_LANES), F32)
    for start, nblk, r in zip(res_starts, res_blocks, res):
        in_specs.append(pl.BlockSpec(
            (bm, bn), lambda g, i, start=start, nblk=nblk: (jnp.clip(row_blk(g, i) - start, 0, nblk - 1), col_blk(g))))
        args.append(r)
    if res and has_aux:
        in_specs.append(pl.BlockSpec((ma, bn), lambda g, i: (0, col_blk(g))))
        args.append(auxres)
    out_specs, out_shape = [], []

    def add_outputs(nrows, blk_rows, row_index):
        out_specs.append(pl.BlockSpec((blk_rows, bn), lambda g, i: (row_index(g, i), col_blk(g))))
        out_shape.append(jax.ShapeDtypeStruct((nrows, n_out), out_dtype))
        if emit_stats:
            out_specs.append(pl.BlockSpec((blk_rows, bn), lambda g, i: (row_index(g, i), col_blk(g))))
            out_shape.append(jax.ShapeDtypeStruct((nrows, n_out), BF16))
            out_specs.append(pl.BlockSpec((1, blk_rows, V7X_LANES), lambda g, i: (col_blk(g), row_index(g, i), 0)))
            out_shape.append(jax.ShapeDtypeStruct((n_tiles, nrows, V7X_LANES), F32))

    add_outputs(m, bm, row_blk)
    if has_aux:
        add_outputs(ma, ma, lambda g, i: 0)
    side_rows = side_bytes = 0
    if has_side:
        side_w, side_out_rows = side
        side_rows, side_cols = side_w.shape[1:]
        sr, sc = side_out_rows // n_tiles, side_cols // ni
        assert side_out_rows % n_tiles == 0 and side_cols % ni == 0 and sr % BF16_SUBLANES == 0 and sc % V7X_LANES == 0
        in_specs.append(pl.BlockSpec((None, sr, sc), lambda g, i: (layer, last_tile(g), i)))
        args.append(side_w)
        out_specs.append(pl.BlockSpec((sr, sc), lambda g, i: (g, i)))
        out_shape.append(jax.ShapeDtypeStruct((side_out_rows + sr, side_cols), BF16))
        side_bytes = _nbytes((sr, sc), F32) + _nbytes((sr, sc), BF16)
    win = (_nbytes((bm + ma, k), BF16) + len(ws) * _nbytes((rows, bn), F32) + side_bytes + norm_bytes
           + _nbytes((bm + ma, bn), out_dtype) + len(res) * _nbytes((bm + ma, bn), F32)
           + emit_stats * (_nbytes((bm + ma, bn), BF16) + _nbytes((bm + ma, V7X_LANES), F32)))
    scratch = _nbytes((2, len(ws), k, bn), BF16)
    return pl.pallas_call(
        functools.partial(_wmm_kernel, n_w=len(ws), res_starts=res_starts, has_aux=has_aux, has_side=has_side,
                          has_norm=has_norm, emit_stats=emit_stats, act=act, n_tiles=n_tiles, rows=rows,
                          src_cols=src_cols, side_rows=side_rows),
        grid=(n_tiles + 1, ni),
        in_specs=in_specs,
        out_specs=out_specs,
        out_shape=out_shape,
        scratch_shapes=[pltpu.VMEM((len(ws), k, bn), BF16), pltpu.VMEM((len(ws), k, bn), BF16)],
        compiler_params=_params(("arbitrary", "arbitrary"), win + scratch // 2),
        name="wmm",
    )(*args)


def _mm_res_kernel(a_ref, b_ref, r_ref, o_ref, *stats, last_k):
    j, kk = pl.program_id(1), pl.program_id(2)
    n_k = pl.num_programs(2)

    @pl.when(kk == 0)
    def _():
        o_ref[...] = r_ref[...]

    if last_k == a_ref.shape[1]:
        o_ref[...] += jnp.dot(a_ref[...], b_ref[...], preferred_element_type=F32)
    else:
        @pl.when(kk < n_k - 1)
        def _():
            o_ref[...] += jnp.dot(a_ref[...], b_ref[...], preferred_element_type=F32)

        @pl.when(kk == n_k - 1)
        def _():
            o_ref[...] += jnp.dot(a_ref[:, 0:last_k], b_ref[0:last_k, :], preferred_element_type=F32)
    if stats:
        ob_ref, ss_ref = stats

        @pl.when(kk == pl.num_programs(2) - 1)
        def _():
            y = o_ref[...]
            ob_ref[...] = y.astype(ob_ref.dtype)

            @pl.when(j == 0)
            def _():
                ss_ref[0] = _row_sumsq(y)

            @pl.when(j > 0)
            def _():
                ss_ref[0] += _row_sumsq(y)


def _matmul_res(a, b, res, row0=0, nrows=None, emit_stats=False, k_real=None):
    m, k = a.shape
    n = b.shape[1]
    nrows = m if nrows is None else nrows
    bm = _divisor_tile(nrows, 1024, BF16_SUBLANES)
    bn = _divisor_tile(n, 1024, V7X_LANES)
    bk = _divisor_tile(k, 4096, V7X_LANES)
    assert row0 % bm == 0
    off = row0 // bm
    win = _nbytes((bm, bk), BF16) + _nbytes((bk, bn), BF16) + 2 * _nbytes((bm, bn), F32)
    last_k = bk if k_real is None else k_real - (k // bk - 1) * bk
    assert 0 < last_k <= bk and last_k % V7X_LANES == 0
    return pl.pallas_call(
        functools.partial(_mm_res_kernel, last_k=last_k),
        grid=(nrows // bm, n // bn, k // bk),
        in_specs=[pl.BlockSpec((bm, bk), lambda i, j, kk: (i + off, kk)),
                  pl.BlockSpec((bk, bn), lambda i, j, kk: (kk, j)),
                  pl.BlockSpec((bm, bn), lambda i, j, kk: (i + off, j))],
        out_specs=[pl.BlockSpec((bm, bn), lambda i, j, kk: (i, j))] + emit_stats * [
            pl.BlockSpec((bm, bn), lambda i, j, kk: (i, j)),
            pl.BlockSpec((1, bm, V7X_LANES), lambda i, j, kk: (0, i, 0))],
        out_shape=[jax.ShapeDtypeStruct((nrows, n), F32)] + emit_stats * [
            jax.ShapeDtypeStruct((nrows, n), BF16), jax.ShapeDtypeStruct((1, nrows, V7X_LANES), F32)],
        compiler_params=_params(("parallel", "arbitrary", "arbitrary"), win + _nbytes((bm, bn), BF16)),
        name="matmul_res",
    )(a, b, res)


def _t5_bucket(rel, n_buckets):
    half = n_buckets // 2
    exact = half // 2
    n = jnp.abs(rel)
    n_f = jnp.maximum(n, 1).astype(F32)
    large = exact + (jnp.log(n_f / exact) / math.log(MAX_DISTANCE / exact) * (half - exact)).astype(jnp.int32)
    large = jnp.minimum(large, half - 1)
    return jnp.where(rel > 0, half, 0) + jnp.where(n < exact, n, large)


def _bucket_tables(n_meta, n_buckets):
    qi = jnp.arange(BLOCK)[:, None]
    sj = jnp.arange(3 * BLOCK)[None, :]
    rel_band = sj - BLOCK - qi
    band = jnp.where(jnp.abs(rel_band) <= BLOCK, _t5_bucket(rel_band, n_buckets), -1)
    sink = jnp.full((BLOCK, 1), n_buckets, jnp.int32)
    pad = jnp.full((BLOCK, BLOCK - n_meta - 1), -1, jnp.int32)
    mk = jnp.arange(n_meta)[None, :]
    meta_first = _t5_bucket(mk - (n_meta + qi), n_buckets)
    meta_far = jnp.full((BLOCK, n_meta), n_buckets // 2 - 1, jnp.int32)
    tok = jnp.concatenate([band, meta_far, sink, pad, meta_first, sink, pad], axis=1)[None].astype(jnp.int32)

    mq = jnp.arange(n_meta)[:, None]
    tk = jnp.arange(BLOCK)[None, :]
    rel_tok = n_meta + tk - mq
    mband = jnp.where(jnp.abs(rel_tok) <= BLOCK, _t5_bucket(rel_tok, n_buckets), -1)
    mmeta = _t5_bucket(mk - mq, n_buckets)
    met = jnp.concatenate([mband, mmeta, sink[:n_meta], pad[:n_meta]], axis=1).astype(jnp.int32)
    return tok, met[None]


def _bias_kernel(tab_ref, bkt_ref, o_ref, *, n_ids, token_variants):
    h = pl.program_id(0)
    bkt = bkt_ref[...]
    acc = jnp.full(bkt.shape, -jnp.inf, F32)
    for b in range(n_ids):
        acc = jnp.where(bkt == b, tab_ref[b, h], acc)
    acc = acc * LOG2E
    if token_variants:
        band, far, first = acc[0, :, :3 * BLOCK], acc[0, :, 3 * BLOCK:4 * BLOCK], acc[0, :, 4 * BLOCK:]
        masked = jnp.full((BLOCK, BLOCK), -jnp.inf, F32)
        o_ref[0, 0, 0] = jnp.concatenate([band, far], axis=1)
        o_ref[0, 1, 0] = jnp.concatenate([masked, band[:, BLOCK:], first], axis=1)
        o_ref[0, 2, 0] = jnp.concatenate([band[:, :2 * BLOCK], masked, far], axis=1)
    else:
        o_ref[0, :, 0] = acc


def _bias_tiles(table, bkt, token_variants=False):
    n_ids, n_heads = table.shape
    v, r, c = (3, BLOCK, 4 * BLOCK) if token_variants else bkt.shape
    return pl.pallas_call(
        functools.partial(_bias_kernel, n_ids=n_ids, token_variants=token_variants),
        grid=(n_heads,),
        in_specs=[pl.BlockSpec(memory_space=pltpu.SMEM),
                  pl.BlockSpec(bkt.shape, lambda h: (0, 0, 0))],
        out_specs=pl.BlockSpec((1, v, 1, r, c), lambda h: (h // GROUP, 0, h % GROUP, 0, 0)),
        out_shape=jax.ShapeDtypeStruct((n_heads // GROUP, v, GROUP, r, c), F32),
        compiler_params=_params(("parallel",), 2 * _nbytes(bkt.shape, F32)),
        name="bias_tiles",
    )(table, bkt)


def _softmax_pv(q, k, v, bias2):
    s = lax.dot_general(q, k, (((1,), (1,)), ((), ())), preferred_element_type=F32) + bias2
    m = jnp.max(s, axis=-1, keepdims=True)
    p = jnp.exp2(s - m)
    denom = jnp.sum(p, axis=-1, keepdims=True)
    return jnp.dot(p.astype(BF16), v, preferred_element_type=F32) * (1.0 / denom)


def _fill_meta_keys(km_scr, vm_scr, km_ref, vm_ref, kg, n_meta):
    km_scr[...] = jnp.zeros(km_scr.shape, BF16)
    vm_scr[...] = jnp.zeros(vm_scr.shape, BF16)
    km_scr[0:n_meta] = _rms(km_ref[...].astype(F32), kg).astype(BF16)
    vm_scr[0:n_meta] = vm_ref[...].astype(BF16)


def _attn_kernel(qg_ref, kg_ref, bias_ref, q_ref, kp_ref, kc_ref, kn_ref, vp_ref, vc_ref, vn_ref,
                 km_ref, vm_ref, o_ref, q_scr, k_scr, v_scr, km_scr, vm_scr, *, chunks, seq_chunks, n_meta, scale):
    r = pl.program_id(1)
    qg = qg_ref[...] * (scale * LOG2E)
    kg = kg_ref[...]
    body = chunks * BLOCK
    stack = GROUP * BLOCK

    k_scr[0:BLOCK] = _rms(kp_ref[...].astype(F32), kg).astype(BF16)
    k_scr[BLOCK:BLOCK + body] = _rms(kc_ref[...].astype(F32), kg).astype(BF16)
    k_scr[BLOCK + body:2 * BLOCK + body] = _rms(kn_ref[...].astype(F32), kg).astype(BF16)
    v_scr[0:BLOCK] = vp_ref[...].astype(BF16)
    v_scr[BLOCK:BLOCK + body] = vc_ref[...].astype(BF16)
    v_scr[BLOCK + body:2 * BLOCK + body] = vn_ref[...].astype(BF16)
    _fill_meta_keys(km_scr, vm_scr, km_ref, vm_ref, kg, n_meta)
    for g in range(GROUP):
        qn = _rms(q_ref[:, g * HEAD_DIM:(g + 1) * HEAD_DIM].astype(F32), qg).astype(BF16)
        for c in range(chunks):
            q_scr[(c * GROUP + g) * BLOCK:(c * GROUP + g + 1) * BLOCK] = qn[c * BLOCK:(c + 1) * BLOCK]
    k_meta = km_scr[...]
    v_meta = vm_scr[...]

    def chunk(cc, carry):
        gch = r * chunks + cc
        is_first = functools.reduce(jnp.logical_or, [gch == s0 for s0, _ in seq_chunks])
        is_last = functools.reduce(jnp.logical_or, [gch == s0 + n - 1 for s0, n in seq_chunks])
        var = jnp.where(is_first, 1, jnp.where(is_last, 2, 0))
        row0 = pl.multiple_of(cc * BLOCK, BLOCK)
        k_all = jnp.concatenate([k_scr[pl.ds(row0, 3 * BLOCK), :], k_meta], axis=0)
        v_all = jnp.concatenate([v_scr[pl.ds(row0, 3 * BLOCK), :], v_meta], axis=0)
        q = q_scr[pl.ds(pl.multiple_of(cc * stack, stack), stack), :]
        bias = bias_ref[0, var].reshape(stack, 4 * BLOCK)
        o = _softmax_pv(q, k_all, v_all, bias)
        for g in range(GROUP):
            o_ref[pl.ds(row0, BLOCK), g * HEAD_DIM:(g + 1) * HEAD_DIM] = o[g * BLOCK:(g + 1) * BLOCK]
        return carry

    lax.fori_loop(0, chunks, chunk, 0, unroll=True)


def _seq_of_chunk(c, seq_chunks):
    s = 0
    for s0, _ in seq_chunks[1:]:
        s = s + (c >= s0).astype(jnp.int32)
    return s


def _attention(z, zm, bias_tok, q_g, k_g, *, seq_chunks, n_meta, d_model, k_col, v_col):
    m = z.shape[0]
    n_chunks = m // BLOCK
    kv_heads = d_model // (GROUP * HEAD_DIM)
    chunks = math.gcd(16, *[n for _, n in seq_chunks])
    body = chunks * BLOCK
    qw = GROUP * HEAD_DIM
    kb, vb = k_col // HEAD_DIM, v_col // HEAD_DIM
    seq_of = functools.partial(_seq_of_chunk, seq_chunks=seq_chunks)

    def halo_prev(h, r):
        return jnp.maximum(r * chunks - 1, 0)

    def halo_next(h, r):
        return jnp.minimum(r * chunks + chunks, n_chunks - 1)

    in_specs = [
        pl.BlockSpec((1, HEAD_DIM), lambda h, r: (0, 0)),
        pl.BlockSpec((1, HEAD_DIM), lambda h, r: (0, 0)),
        pl.BlockSpec((1, 3, GROUP, BLOCK, 4 * BLOCK), lambda h, r: (h, 0, 0, 0, 0)),
        pl.BlockSpec((body, qw), lambda h, r: (r, h)),
        pl.BlockSpec((BLOCK, HEAD_DIM), lambda h, r: (halo_prev(h, r), kb + h)),
        pl.BlockSpec((body, HEAD_DIM), lambda h, r: (r, kb + h)),
        pl.BlockSpec((BLOCK, HEAD_DIM), lambda h, r: (halo_next(h, r), kb + h)),
        pl.BlockSpec((BLOCK, HEAD_DIM), lambda h, r: (halo_prev(h, r), vb + h)),
        pl.BlockSpec((body, HEAD_DIM), lambda h, r: (r, vb + h)),
        pl.BlockSpec((BLOCK, HEAD_DIM), lambda h, r: (halo_next(h, r), vb + h)),
        pl.BlockSpec((n_meta, HEAD_DIM), lambda h, r: (seq_of(r * chunks), kb + h)),
        pl.BlockSpec((n_meta, HEAD_DIM), lambda h, r: (seq_of(r * chunks), vb + h)),
    ]
    win = (_nbytes((3, GROUP, BLOCK, 4 * BLOCK), F32) + 2 * _nbytes((body, qw), F32)
           + 4 * _nbytes((body + 2 * BLOCK, HEAD_DIM), F32))
    return pl.pallas_call(
        functools.partial(_attn_kernel, chunks=chunks, seq_chunks=seq_chunks, n_meta=n_meta,
                          scale=HEAD_DIM ** -0.5),
        grid=(kv_heads, m // body),
        in_specs=in_specs,
        out_specs=pl.BlockSpec((body, qw), lambda h, r: (r, h)),
        out_shape=jax.ShapeDtypeStruct((m, d_model), F32),
        scratch_shapes=[pltpu.VMEM((body * GROUP, HEAD_DIM), BF16),
                        pltpu.VMEM((body + 2 * BLOCK, HEAD_DIM), BF16),
                        pltpu.VMEM((body + 2 * BLOCK, HEAD_DIM), BF16),
                        pltpu.VMEM((BLOCK, HEAD_DIM), BF16),
                        pltpu.VMEM((BLOCK, HEAD_DIM), BF16)],
        compiler_params=_params(("parallel", "arbitrary"), win),
        name="attention",
    )(q_g.reshape(1, HEAD_DIM), k_g.reshape(1, HEAD_DIM), bias_tok, z, z, z, z, z, z, z, zm, zm)


def _attn_meta_kernel(qg_ref, kg_ref, bias_ref, q_ref, k1_ref, v1_ref, km_ref, vm_ref, o_ref,
                      km_scr, vm_scr, *, n_meta, scale):
    qg = qg_ref[...] * (scale * LOG2E)
    kg = kg_ref[...]
    _fill_meta_keys(km_scr, vm_scr, km_ref, vm_ref, kg, n_meta)
    k_all = jnp.concatenate([_rms(k1_ref[...].astype(F32), kg).astype(BF16), km_scr[...]], axis=0)
    v_all = jnp.concatenate([v1_ref[...].astype(BF16), vm_scr[...]], axis=0)
    q = q_ref[...].astype(F32)
    qn = jnp.concatenate([_rms(q[:, g * HEAD_DIM:(g + 1) * HEAD_DIM], qg).astype(BF16) for g in range(GROUP)], axis=0)
    o = _softmax_pv(qn, k_all, v_all, bias_ref[0, 0].reshape(GROUP * n_meta, 2 * BLOCK))
    for g in range(GROUP):
        o_ref[:, g * HEAD_DIM:(g + 1) * HEAD_DIM] = o[g * n_meta:(g + 1) * n_meta]


def _attention_meta(z, zm, bias_met, q_g, k_g, *, seq_chunks, n_meta, d_model, k_col, v_col):
    n_seq = len(seq_chunks)
    kv_heads = d_model // (GROUP * HEAD_DIM)
    qw = GROUP * HEAD_DIM
    kb, vb = k_col // HEAD_DIM, v_col // HEAD_DIM

    def first_chunk(s):
        c = 0
        for i, (s0, _) in enumerate(seq_chunks):
            c = c + jnp.where(s == i, s0, 0)
        return c

    in_specs = [
        pl.BlockSpec((1, HEAD_DIM), lambda s, h: (0, 0)),
        pl.BlockSpec((1, HEAD_DIM), lambda s, h: (0, 0)),
        pl.BlockSpec((1, 1, GROUP, n_meta, 2 * BLOCK), lambda s, h: (h, 0, 0, 0, 0)),
        pl.BlockSpec((n_meta, qw), lambda s, h: (s, h)),
        pl.BlockSpec((BLOCK, HEAD_DIM), lambda s, h: (first_chunk(s), kb + h)),
        pl.BlockSpec((BLOCK, HEAD_DIM), lambda s, h: (first_chunk(s), vb + h)),
        pl.BlockSpec((n_meta, HEAD_DIM), lambda s, h: (s, kb + h)),
        pl.BlockSpec((n_meta, HEAD_DIM), lambda s, h: (s, vb + h)),
    ]
    return pl.pallas_call(
        functools.partial(_attn_meta_kernel, n_meta=n_meta, scale=HEAD_DIM ** -0.5),
        grid=(n_seq, kv_heads),
        in_specs=in_specs,
        out_specs=pl.BlockSpec((n_meta, qw), lambda s, h: (s, h)),
        out_shape=jax.ShapeDtypeStruct((n_seq * n_meta, d_model), F32),
        scratch_shapes=[pltpu.VMEM((BLOCK, HEAD_DIM), BF16), pltpu.VMEM((BLOCK, HEAD_DIM), BF16)],
        compiler_params=_params(("parallel", "parallel"), 1 << 20),
        name="attention_meta",
    )(q_g.reshape(1, HEAD_DIM), k_g.reshape(1, HEAD_DIM), bias_met, zm, z, z, zm, zm)


def _merge_math(attn, gb, gc, hc, ga, gcv, u_prev, u_next, cw, na, nc):
    u = gc * hc
    rows = u.shape[0]
    ridx = lax.broadcasted_iota(jnp.int32, u.shape, 0)
    up = jnp.where(ridx == 0, u_prev, pltpu.roll(u, 1, axis=0))
    un = jnp.where(ridx == rows - 1, u_next, pltpu.roll(u, rows - 1, axis=0))
    conv = gb * (cw[0:1] * up + cw[1:2] * u + cw[2:3] * un)
    return jax.nn.sigmoid(ga) * _rms(attn, na) + jax.nn.sigmoid(gcv) * _rms(conv, nc)


def _merge_kernel(attn_ref, gb_ref, gc_ref, hc_ref, ga_ref, gcv_ref, gcp_ref, hcp_ref, gcn_ref, hcn_ref,
                  gcm_ref, hcm_ref, cw_ref, na_ref, nc_ref, o_ref, conv_scr, *, seq_chunks):
    c = pl.program_id(0)
    is_first = functools.reduce(jnp.logical_or, [c == s0 for s0, _ in seq_chunks])
    is_last = functools.reduce(jnp.logical_or, [c == s0 + n - 1 for s0, n in seq_chunks])
    last = HALO_ROWS - 1
    rows, d = o_ref.shape
    n_tiles = d // V7X_LANES
    ridx = lax.broadcasted_iota(jnp.int32, (rows, V7X_LANES), 0)

    def cols(ct):
        return pl.ds(pl.multiple_of(ct * V7X_LANES, V7X_LANES), V7X_LANES)

    def f32(ref, sl):
        return ref[:, sl].astype(F32)

    def conv_pass(ct, carry):
        ss_attn, ss_conv = carry
        sl = cols(ct)
        u = f32(gc_ref, sl) * f32(hc_ref, sl)
        u_prev_tok = (f32(gcp_ref, sl) * f32(hcp_ref, sl))[last:last + 1]
        u_prev_meta = (f32(gcm_ref, sl) * f32(hcm_ref, sl))[last:last + 1]
        u_prev = jnp.where(is_first, u_prev_meta, u_prev_tok)
        u_next_tok = (f32(gcn_ref, sl) * f32(hcn_ref, sl))[0:1]
        u_next = jnp.where(is_last, jnp.zeros_like(u_next_tok), u_next_tok)
        up = jnp.where(ridx == 0, u_prev, pltpu.roll(u, 1, axis=0))
        un = jnp.where(ridx == rows - 1, u_next, pltpu.roll(u, rows - 1, axis=0))
        cw = cw_ref[:, sl]
        conv = f32(gb_ref, sl) * (cw[0:1] * up + cw[1:2] * u + cw[2:3] * un)
        conv_scr[:, sl] = conv
        attn = attn_ref[:, sl]
        return ss_attn + attn * attn, ss_conv + conv * conv

    zero = jnp.zeros((rows, V7X_LANES), F32)
    ss_attn, ss_conv = lax.fori_loop(0, n_tiles, conv_pass, (zero, zero), unroll=2)
    r_attn = lax.rsqrt(jnp.sum(ss_attn, axis=-1, keepdims=True) / d + EPS)
    r_conv = lax.rsqrt(jnp.sum(ss_conv, axis=-1, keepdims=True) / d + EPS)

    def gate_pass(ct, carry):
        sl = cols(ct)
        a = attn_ref[:, sl] * r_attn * na_ref[:, sl]
        c = conv_scr[:, sl] * r_conv * nc_ref[:, sl]
        o_ref[:, sl] = (jax.nn.sigmoid(f32(ga_ref, sl)) * a + jax.nn.sigmoid(f32(gcv_ref, sl)) * c).astype(o_ref.dtype)
        return carry

    lax.fori_loop(0, n_tiles, gate_pass, 0, unroll=2)


def _merge(attn, z, zm, conv_w, norm_a, norm_c, *, seq_chunks, n_meta, cols):
    m, d = attn.shape
    n_chunks = m // BLOCK
    per_chunk = BLOCK // HALO_ROWS
    per_meta = n_meta // HALO_ROWS
    gb, gc, hc, ga, gcv = [c // d for c in cols]
    seq_of = functools.partial(_seq_of_chunk, seq_chunks=seq_chunks)

    def main(col):
        return pl.BlockSpec((BLOCK, d), lambda c: (c, col))

    def prev_rows(col):
        return pl.BlockSpec((HALO_ROWS, d), lambda c: (jnp.maximum(c * per_chunk - 1, 0), col))

    def next_rows(col):
        return pl.BlockSpec((HALO_ROWS, d), lambda c: (jnp.minimum(c + 1, n_chunks - 1) * per_chunk, col))

    def meta_rows(col):
        return pl.BlockSpec((HALO_ROWS, d), lambda c: (seq_of(c) * per_meta + per_meta - 1, col))

    def row(nrows):
        return pl.BlockSpec((nrows, d), lambda c: (0, 0))

    win = 6 * _nbytes((BLOCK, d), F32) + _nbytes((BLOCK, d), BF16) + 8 * _nbytes((HALO_ROWS, d), F32)
    return pl.pallas_call(
        functools.partial(_merge_kernel, seq_chunks=seq_chunks),
        grid=(n_chunks,),
        in_specs=[main(0), main(gb), main(gc), main(hc), main(ga), main(gcv),
                  prev_rows(gc), prev_rows(hc), next_rows(gc), next_rows(hc), meta_rows(gc), meta_rows(hc),
                  row(3), row(1), row(1)],
        out_specs=pl.BlockSpec((BLOCK, d), lambda c: (c, 0)),
        out_shape=jax.ShapeDtypeStruct((m, d), BF16),
        scratch_shapes=[pltpu.VMEM((BLOCK, d), F32)],
        compiler_params=_params(("parallel",), win),
        name="merge",
    )(attn, z, z, z, z, z, z, z, z, z, zm, zm, conv_w, norm_a.reshape(1, d), norm_c.reshape(1, d))


def _merge_meta_kernel(attn_ref, gb_ref, gc_ref, hc_ref, ga_ref, gcv_ref, gcn_ref, hcn_ref,
                       cw_ref, na_ref, nc_ref, o_ref):
    f32 = lambda ref: ref[...].astype(F32)
    u_next = (f32(gcn_ref) * f32(hcn_ref))[0:1]
    u_prev = jnp.zeros_like(u_next)
    o_ref[...] = _merge_math(attn_ref[...], f32(gb_ref), f32(gc_ref), f32(hc_ref), f32(ga_ref), f32(gcv_ref),
                             u_prev, u_next, cw_ref[...], na_ref[...], nc_ref[...]).astype(o_ref.dtype)


def _merge_meta(attn_m, z, zm, conv_w, norm_a, norm_c, *, seq_chunks, n_meta, cols):
    mm, d = attn_m.shape
    per_chunk = BLOCK // HALO_ROWS
    gb, gc, hc, ga, gcv = [c // d for c in cols]

    def first_rows(s):
        r = 0
        for i, (s0, _) in enumerate(seq_chunks):
            r = r + jnp.where(s == i, s0 * per_chunk, 0)
        return r

    def main(col):
        return pl.BlockSpec((n_meta, d), lambda s: (s, col))

    def next_rows(col):
        return pl.BlockSpec((HALO_ROWS, d), lambda s: (first_rows(s), col))

    def row(nrows):
        return pl.BlockSpec((nrows, d), lambda s: (0, 0))

    win = 7 * _nbytes((n_meta, d), F32) + 2 * _nbytes((HALO_ROWS, d), F32)
    return pl.pallas_call(
        _merge_meta_kernel,
        grid=(len(seq_chunks),),
        in_specs=[main(0), main(gb), main(gc), main(hc), main(ga), main(gcv), next_rows(gc), next_rows(hc),
                  row(3), row(1), row(1)],
        out_specs=pl.BlockSpec((n_meta, d), lambda s: (s, 0)),
        out_shape=jax.ShapeDtypeStruct((mm, d), BF16),
        compiler_params=_params(("parallel",), win),
        name="merge_meta",
    )(attn_m, zm, zm, zm, zm, zm, z, z, conv_w, norm_a.reshape(1, d), norm_c.reshape(1, d))


def kernel(x_prompt, x_sample, meta_tokens, rel_bias, norm1_g, w_in, q_norm_g, k_norm_g, attn_sink, conv_w,
           branch_norm_a, branch_norm_c, w_out, norm2_g, w_ffn_gate, w_ffn_up, w_ffn_down):
    d = x_prompt.shape[-1]
    n_meta = meta_tokens.shape[0]
    n_buckets = rel_bias.shape[0]
    depth, _, in_dim = w_in.shape
    ffn = w_ffn_gate.shape[-1]
    kv_dim = (in_dim - 6 * d) // 2
    assert d % (GROUP * HEAD_DIM) == 0 and kv_dim == d // GROUP and attn_sink.shape[1] * HEAD_DIM == d
    assert n_meta % HALO_ROWS == 0 and n_meta < BLOCK

    groups = (x_prompt, x_sample)
    seq_chunks = []
    for xg in groups:
        assert xg.shape[1] % BLOCK == 0 and xg.shape[1] >= 2 * BLOCK
        for _ in range(xg.shape[0]):
            start = seq_chunks[-1][0] + seq_chunks[-1][1] if seq_chunks else 0
            seq_chunks.append((start, xg.shape[1] // BLOCK))
    seq_chunks = tuple(seq_chunks)
    n_seq = len(seq_chunks)

    x_parts = [xg.reshape(-1, d) for xg in groups]
    xm = jnp.tile(meta_tokens.astype(F32), (n_seq, 1))

    cols = tuple(d * i for i in range(1, 6))
    k_col, v_col = 6 * d, 6 * d + kv_dim
    ffn_tile = 512
    ffn_p = -(-ffn // ffn_tile) * ffn_tile
    out_tile = _divisor_tile(d, 512, V7X_LANES)
    n_q, n_rest = d // kv_dim, 5 * d // kv_dim

    def regroup(j):
        return jnp.where(j < n_q, j, jnp.where(j < n_q + n_rest, j + 2, j - n_rest))

    bkt_tok, bkt_met = _bucket_tables(n_meta, n_buckets)

    geo = dict(seq_chunks=seq_chunks, n_meta=n_meta)
    xb, ss = _prep(x_parts)
    xmb, ssm = _prep([xm])
    for l in range(depth):
        with_meta = l < depth - 1
        table = jnp.concatenate([rel_bias, attn_sink[l][None]], axis=0)
        z, zm = _wmm(xb, xmb, [w_in], l, in_dim, kv_dim, BF16, src_block=regroup, norm=(norm1_g[l], ss, ssm))
        attn = _attention(z, zm, _bias_tiles(table, bkt_tok, token_variants=True), q_norm_g[l], k_norm_g[l],
                          d_model=d, k_col=k_col, v_col=v_col, **geo)
        merged = _merge(attn, z, zm, conv_w[l], branch_norm_a[l], branch_norm_c[l], cols=cols, **geo)
        merged_m = None
        if with_meta:
            attn_m = _attention_meta(z, zm, _bias_tiles(table, bkt_met), q_norm_g[l], k_norm_g[l],
                                     d_model=d, k_col=k_col, v_col=v_col, **geo)
            merged_m = _merge_meta(attn_m, z, zm, conv_w[l], branch_norm_a[l], branch_norm_c[l], cols=cols, **geo)
        h, hb, hss, *hm = _wmm(merged, merged_m, [w_out], l, d, out_tile, F32, res=x_parts,
                               auxres=xm if with_meta else None, emit_stats=True)
        hm, hmb, hmss = hm if with_meta else (None, None, None)
        t, *tm, wd_b = _wmm(hb, hmb, [w_ffn_gate, w_ffn_up], l, ffn_p, ffn_tile, BF16, act=True,
                            side=(w_ffn_down, ffn_p), norm=(norm2_g[l], hss, hmss))
        if with_meta:
            x, xb, ss = _matmul_res(t, wd_b, h, emit_stats=True, k_real=ffn)
            x_parts = [x]
            xm, xmb, ssm = _matmul_res(tm[0], wd_b, hm, emit_stats=True, k_real=ffn)

    outs = []
    row = 0
    for xg in groups:
        nrows = xg.shape[0] * xg.shape[1]
        outs.append(_matmul_res(t, wd_b, h, row, nrows, k_real=ffn)[0].reshape(xg.shape))
        row += nrows
    return tuple(outs)
```

```python
import functools
import math

import jax
import jax.numpy as jnp
from jax import lax
from jax.experimental import pallas as pl
from jax.experimental.pallas import tpu as pltpu

HEAD_DIM = 128
GROUP = 4
BLOCK = 128
MAX_DISTANCE = 128
EPS = 1e-6
LOG2E = math.log2(math.e)

V7X_VMEM_BYTES = 64 << 20
V7X_LANES = 128
BF16_SUBLANES = 16
F32_SUBLANES = 8
HALO_ROWS = BF16_SUBLANES
VMEM_SLACK_BYTES = 10 << 20

F32 = jnp.float32
BF16 = jnp.bfloat16


def _params(dims, window_bytes):
    limit = min(2 * window_bytes + VMEM_SLACK_BYTES, V7X_VMEM_BYTES - (6 << 20))
    return pltpu.CompilerParams(dimension_semantics=dims, vmem_limit_bytes=int(limit))


def _divisor_tile(n, cap, unit):
    if n <= cap:
        return n
    t = (cap // unit) * unit
    while t >= unit:
        if n % t == 0:
            return t
        t -= unit
    raise ValueError(f"no tile for {n}")


def _nbytes(shape, dtype):
    return math.prod(shape) * jnp.dtype(dtype).itemsize


def _rms(x, g):
    ms = jnp.mean(x * x, axis=-1, keepdims=True)
    return x * lax.rsqrt(ms + EPS) * g


def _row_sumsq(x):
    return jnp.broadcast_to(jnp.sum(x * x, axis=-1, keepdims=True), (x.shape[0], V7X_LANES))


def _rstd(ss_ref, width):
    ss = ss_ref[0]
    for p in range(1, ss_ref.shape[0]):
        ss = ss + ss_ref[p]
    return lax.rsqrt(ss / width + EPS)


def _prep_kernel(*refs, starts):
    x_refs, o_ref, ss_ref = refs[:len(starts)], refs[-2], refs[-1]
    i = pl.program_id(0)
    ends = starts[1:] + (None,)
    for x_ref, lo, hi in zip(x_refs, starts, ends):
        in_seg = i >= lo if hi is None else jnp.logical_and(i >= lo, i < hi)

        @pl.when(in_seg)
        def _():
            x = x_ref[...]
            o_ref[...] = x.astype(o_ref.dtype)
            ss_ref[0] = _row_sumsq(x)


def _prep(xs):
    d = xs[0].shape[1]
    m = sum(x.shape[0] for x in xs)
    bm = _divisor_tile(math.gcd(*[x.shape[0] for x in xs]), 256, BF16_SUBLANES)
    blocks = [x.shape[0] // bm for x in xs]
    starts = tuple(sum(blocks[:s]) for s in range(len(xs)))
    in_specs = [pl.BlockSpec((bm, d), lambda i, start=start, nblk=nblk: (jnp.clip(i - start, 0, nblk - 1), 0))
                for start, nblk in zip(starts, blocks)]
    return pl.pallas_call(
        functools.partial(_prep_kernel, starts=starts),
        grid=(m // bm,),
        in_specs=in_specs,
        out_specs=[pl.BlockSpec((bm, d), lambda i: (i, 0)), pl.BlockSpec((1, bm, V7X_LANES), lambda i: (0, i, 0))],
        out_shape=[jax.ShapeDtypeStruct((m, d), BF16), jax.ShapeDtypeStruct((1, m, V7X_LANES), F32)],
        compiler_params=_params(("parallel",), len(xs) * _nbytes((bm, d), F32) + _nbytes((bm, d), BF16)),
        name="prep",
    )(*xs)


def _wmm_kernel(*refs, n_w, res_starts, has_aux, has_side, has_norm, emit_stats, act, n_tiles, rows, src_cols,
                side_rows):
    it = iter(refs)
    a_ref = next(it)
    aux_ref = next(it) if has_aux else None
    w_refs = [next(it) for _ in range(n_w)]
    gain_ref = next(it) if has_norm else None
    ss_ref = next(it) if has_norm else None
    auxss_ref = next(it) if has_norm and has_aux else None
    res_refs = [next(it) for _ in res_starts]
    auxres_ref = next(it) if res_starts and has_aux else None
    side_ref = next(it) if has_side else None
    o_ref = next(it)
    ob_ref, oss_ref = (next(it), next(it)) if emit_stats else (None, None)
    oaux_ref = next(it) if has_aux else None
    oauxb_ref, oauxss_ref = (next(it), next(it)) if emit_stats and has_aux else (None, None)
    oside_ref = next(it) if has_side else None
    w_bufs = (next(it), next(it))
    k_dim = w_bufs[0].shape[1]
    g = pl.program_id(0)
    i = pl.program_id(1)
    bn = w_bufs[0].shape[-1]

    if has_side:
        w = side_ref[...]
        row = g * w.shape[0] + lax.broadcasted_iota(jnp.int32, w.shape, 0)
        oside_ref[...] = jnp.where(row < side_rows, w, 0.0).astype(oside_ref.dtype)

    def stage(buf):
        row0 = pl.multiple_of(i * rows, rows)
        for t in range(n_w):
            w = w_refs[t][...]
            if has_norm:
                w = w * jnp.tile(gain_ref[...], (1, bn // V7X_LANES))
            if src_cols % bn:
                col = jnp.minimum(g, n_tiles - 1) * bn + lax.broadcasted_iota(jnp.int32, w.shape, 1)
                w = jnp.where(col < src_cols, w, 0.0)
            buf[t, pl.ds(row0, rows), :] = w.astype(buf.dtype)

    def apply(buf, ncols, a, ss, res, out, out_b, out_ss):
        ys = [jnp.dot(a, buf[t, :, 0:ncols], preferred_element_type=F32) for t in range(n_w)]
        if has_norm:
            rstd = jnp.tile(_rstd(ss, k_dim), (1, ncols // V7X_LANES))
            ys = [y * rstd for y in ys]
        y = ys[0] * jax.nn.sigmoid(ys[0]) * ys[1] if act else ys[0]
        y = y if res is None else res + y
        out[:, 0:ncols] = y.astype(out.dtype)
        if ncols < bn:
            out[:, ncols:] = jnp.zeros((y.shape[0], bn - ncols), out.dtype)
        if emit_stats:
            out_b[...] = y.astype(out_b.dtype)
            out_ss[0] = _row_sumsq(y)

    def compute(buf, ncols):
        res = None
        if res_starts:
            res = res_refs[0][...]
            for start, ref in zip(res_starts[1:], res_refs[1:]):
                res = jnp.where(i >= start, ref[...], res)
        apply(buf, ncols, a_ref[...], ss_ref, res, o_ref, ob_ref, oss_ref)
        if has_aux:
            @pl.when(i == 0)
            def _():
                apply(buf, ncols, aux_ref[...], auxss_ref, auxres_ref[...] if res_starts else None,
                      oaux_ref, oauxb_ref, oauxss_ref)

    @pl.when(g == 0)
    def _():
        stage(w_bufs[0])

    last_cols = src_cols % bn
    full_steps = n_tiles - 1 if last_cols else n_tiles
    for parity in range(2):
        @pl.when(jnp.logical_and(jnp.logical_and(g > 0, g <= full_steps), g % 2 == parity))
        def _():
            stage(w_bufs[parity])
            compute(w_bufs[1 - parity], bn)

    if last_cols:
        assert last_cols % V7X_LANES == 0 and not res_starts and not emit_stats

        @pl.when(g == n_tiles)
        def _():
            compute(w_bufs[(n_tiles - 1) % 2], last_cols)


def _wmm(a, aux, ws, layer, n_out, bn, out_dtype, *, src_block=None, act=False, res=None, auxres=None, side=None,
         norm=None, emit_stats=False, bm_cap=1024):
    m, k = a.shape
    src_cols = ws[0].shape[2]
    bm = _divisor_tile(m, bm_cap, BF16_SUBLANES)
    ni, n_tiles = m // bm, n_out // bn
    rows = k // ni
    assert k % ni == 0 and rows % BF16_SUBLANES == 0 and n_out % bn == 0
    assert src_block is None or src_cols % bn == 0
    src = src_block if src_block is not None else (lambda j: j)
    res = list(res) if res is not None else []
    has_aux, has_side, has_norm = aux is not None, side is not None, norm is not None
    ma = aux.shape[0] if has_aux else 0
    res_blocks = [r.shape[0] // bm for r in res]
    assert all(r.shape[0] % bm == 0 for r in res) and sum(res_blocks) == (ni if res else 0)
    res_starts = tuple(sum(res_blocks[:s]) for s in range(len(res)))

    def row_blk(g, i):
        return jnp.where(g == 0, 0, i)

    def col_blk(g):
        return jnp.maximum(g - 1, 0)

    def last_tile(g):
        return jnp.minimum(g, n_tiles - 1)

    in_specs = [pl.BlockSpec((bm, k), lambda g, i: (row_blk(g, i), 0))]
    args = [a]
    if has_aux:
        in_specs.append(pl.BlockSpec((ma, k), lambda g, i: (0, 0)))
        args.append(aux)
    for w in ws:
        in_specs.append(pl.BlockSpec((None, rows, bn), lambda g, i: (layer, i, src(last_tile(g)))))
        args.append(w)
    norm_bytes = 0
    if has_norm:
        gain, ss, ss_aux = norm
        in_specs.append(pl.BlockSpec((rows, V7X_LANES), lambda g, i: (i, 0)))
        args.append(jnp.broadcast_to(gain[:, None], (k, V7X_LANES)))
        in_specs.append(pl.BlockSpec((ss.shape[0], bm, V7X_LANES), lambda g, i: (0, row_blk(g, i), 0)))
        args.append(ss)
        if has_aux:
            in_specs.append(pl.BlockSpec((ss_aux.shape[0], ma, V7X_LANES), lambda g, i: (0, 0, 0)))
            args.append(ss_aux)
        norm_bytes = _nbytes((rows + ss.shape[0] * (bm + ma), V7X_LANES), F32)
    for start, nblk, r in zip(res_starts, res_blocks, res):
        in_specs.append(pl.BlockSpec(
            (bm, bn), lambda g, i, start=start, nblk=nblk: (jnp.clip(row_blk(g, i) - start, 0, nblk - 1), col_blk(g))))
        args.append(r)
    if res and has_aux:
        in_specs.append(pl.BlockSpec((ma, bn), lambda g, i: (0, col_blk(g))))
        args.append(auxres)
    out_specs, out_shape = [], []

    def add_outputs(nrows, blk_rows, row_index):
        out_specs.append(pl.BlockSpec((blk_rows, bn), lambda g, i: (row_index(g, i), col_blk(g))))
        out_shape.append(jax.ShapeDtypeStruct((nrows, n_out), out_dtype))
        if emit_stats:
            out_specs.append(pl.BlockSpec((blk_rows, bn), lambda g, i: (row_index(g, i), col_blk(g))))
            out_shape.append(jax.ShapeDtypeStruct((nrows, n_out), BF16))
            out_specs.append(pl.BlockSpec((1, blk_rows, V7X_LANES), lambda g, i: (col_blk(g), row_index(g, i), 0)))
            out_shape.append(jax.ShapeDtypeStruct((n_tiles, nrows, V7X_LANES), F32))

    add_outputs(m, bm, row_blk)
    if has_aux:
        add_outputs(ma, ma, lambda g, i: 0)
    side_rows = side_bytes = 0
    if has_side:
        side_w, side_out_rows = side
        side_rows, side_cols = side_w.shape[1:]
        sr, sc = side_out_rows // n_tiles, side_cols // ni
        assert side_out_rows % n_tiles == 0 and side_cols % ni == 0 and sr % BF16_SUBLANES == 0 and sc % V7X_LANES == 0
        in_specs.append(pl.BlockSpec((None, sr, sc), lambda g, i: (layer, last_tile(g), i)))
        args.append(side_w)
        out_specs.append(pl.BlockSpec((sr, sc), lambda g, i: (g, i)))
        out_shape.append(jax.ShapeDtypeStruct((side_out_rows + sr, side_cols), BF16))
        side_bytes = _nbytes((sr, sc), F32) + _nbytes((sr, sc), BF16)
    win = (_nbytes((bm + ma, k), BF16) + len(ws) * _nbytes((rows, bn), F32) + side_bytes + norm_bytes
           + _nbytes((bm + ma, bn), out_dtype) + len(res) * _nbytes((bm + ma, bn), F32)
           + emit_stats * (_nbytes((bm + ma, bn), BF16) + _nbytes((bm + ma, V7X_LANES), F32)))
    scratch = _nbytes((2, len(ws), k, bn), BF16)
    return pl.pallas_call(
        functools.partial(_wmm_kernel, n_w=len(ws), res_starts=res_starts, has_aux=has_aux, has_side=has_side,
                          has_norm=has_norm, emit_stats=emit_stats, act=act, n_tiles=n_tiles, rows=rows,
                          src_cols=src_cols, side_rows=side_rows),
        grid=(n_tiles + 1, ni),
        in_specs=in_specs,
        out_specs=out_specs,
        out_shape=out_shape,
        scratch_shapes=[pltpu.VMEM((len(ws), k, bn), BF16), pltpu.VMEM((len(ws), k, bn), BF16)],
        compiler_params=_params(("arbitrary", "arbitrary"), win + scratch // 2),
        name="wmm",
    )(*args)


def _mm_res_kernel(a_ref, b_ref, r_ref, o_ref, *stats, last_k):
    j, kk = pl.program_id(1), pl.program_id(2)
    n_k = pl.num_programs(2)

    @pl.when(kk == 0)
    def _():
        o_ref[...] = r_ref[...]

    if last_k == a_ref.shape[1]:
        o_ref[...] += jnp.dot(a_ref[...], b_ref[...], preferred_element_type=F32)
    else:
        @pl.when(kk < n_k - 1)
        def _():
            o_ref[...] += jnp.dot(a_ref[...], b_ref[...], preferred_element_type=F32)

        @pl.when(kk == n_k - 1)
        def _():
            o_ref[...] += jnp.dot(a_ref[:, 0:last_k], b_ref[0:last_k, :], preferred_element_type=F32)
    if stats:
        ob_ref, ss_ref = stats

        @pl.when(kk == pl.num_programs(2) - 1)
        def _():
            y = o_ref[...]
            ob_ref[...] = y.astype(ob_ref.dtype)

            @pl.when(j == 0)
            def _():
                ss_ref[0] = _row_sumsq(y)

            @pl.when(j > 0)
            def _():
                ss_ref[0] += _row_sumsq(y)


def _matmul_res(a, b, res, row0=0, nrows=None, emit_stats=False, k_real=None):
    m, k = a.shape
    n = b.shape[1]
    nrows = m if nrows is None else nrows
    bm = _divisor_tile(nrows, 1024, BF16_SUBLANES)
    bn = _divisor_tile(n, 1024, V7X_LANES)
    bk = _divisor_tile(k, 4096, V7X_LANES)
    assert row0 % bm == 0
    off = row0 // bm
    win = _nbytes((bm, bk), BF16) + _nbytes((bk, bn), BF16) + 2 * _nbytes((bm, bn), F32)
    last_k = bk if k_real is None else k_real - (k // bk - 1) * bk
    assert 0 < last_k <= bk and last_k % V7X_LANES == 0
    return pl.pallas_call(
        functools.partial(_mm_res_kernel, last_k=last_k),
        grid=(nrows // bm, n // bn, k // bk),
        in_specs=[pl.BlockSpec((bm, bk), lambda i, j, kk: (i + off, kk)),
                  pl.BlockSpec((bk, bn), lambda i, j, kk: (kk, j)),
                  pl.BlockSpec((bm, bn), lambda i, j, kk: (i + off, j))],
        out_specs=[pl.BlockSpec((bm, bn), lambda i, j, kk: (i, j))] + emit_stats * [
            pl.BlockSpec((bm, bn), lambda i, j, kk: (i, j)),
            pl.BlockSpec((1, bm, V7X_LANES), lambda i, j, kk: (0, i, 0))],
        out_shape=[jax.ShapeDtypeStruct((nrows, n), F32)] + emit_stats * [
            jax.ShapeDtypeStruct((nrows, n), BF16), jax.ShapeDtypeStruct((1, nrows, V7X_LANES), F32)],
        compiler_params=_params(("parallel", "arbitrary", "arbitrary"), win + _nbytes((bm, bn), BF16)),
        name="matmul_res",
    )(a, b, res)


def _t5_bucket(rel, n_buckets):
    half = n_buckets // 2
    exact = half // 2
    n = jnp.abs(rel)
    n_f = jnp.maximum(n, 1).astype(F32)
    large = exact + (jnp.log(n_f / exact) / math.log(MAX_DISTANCE / exact) * (half - exact)).astype(jnp.int32)
    large = jnp.minimum(large, half - 1)
    return jnp.where(rel > 0, half, 0) + jnp.where(n < exact, n, large)


def _bucket_tables(n_meta, n_buckets):
    qi = jnp.arange(BLOCK)[:, None]
    sj = jnp.arange(3 * BLOCK)[None, :]
    rel_band = sj - BLOCK - qi
    band = jnp.where(jnp.abs(rel_band) <= BLOCK, _t5_bucket(rel_band, n_buckets), -1)
    sink = jnp.full((BLOCK, 1), n_buckets, jnp.int32)
    pad = jnp.full((BLOCK, BLOCK - n_meta - 1), -1, jnp.int32)
    mk = jnp.arange(n_meta)[None, :]
    meta_first = _t5_bucket(mk - (n_meta + qi), n_buckets)
    meta_far = jnp.full((BLOCK, n_meta), n_buckets // 2 - 1, jnp.int32)
    tok = jnp.concatenate([band, meta_far, sink, pad, meta_first, sink, pad], axis=1)[None].astype(jnp.int32)

    mq = jnp.arange(n_meta)[:, None]
    tk = jnp.arange(BLOCK)[None, :]
    rel_tok = n_meta + tk - mq
    mband = jnp.where(jnp.abs(rel_tok) <= BLOCK, _t5_bucket(rel_tok, n_buckets), -1)
    mmeta = _t5_bucket(mk - mq, n_buckets)
    met = jnp.concatenate([mband, mmeta, sink[:n_meta], pad[:n_meta]], axis=1).astype(jnp.int32)
    return tok, met[None]


def _bias_kernel(tab_ref, bkt_ref, o_ref, *, n_ids, token_variants):
    h = pl.program_id(0)
    bkt = bkt_ref[...]
    acc = jnp.full(bkt.shape, -jnp.inf, F32)
    for b in range(n_ids):
        acc = jnp.where(bkt == b, tab_ref[b, h], acc)
    acc = acc * LOG2E
    if token_variants:
        band, far, first = acc[0, :, :3 * BLOCK], acc[0, :, 3 * BLOCK:4 * BLOCK], acc[0, :, 4 * BLOCK:]
        masked = jnp.full((BLOCK, BLOCK), -jnp.inf, F32)
        o_ref[0, 0, 0] = jnp.concatenate([band, far], axis=1)
        o_ref[0, 1, 0] = jnp.concatenate([masked, band[:, BLOCK:], first], axis=1)
        o_ref[0, 2, 0] = jnp.concatenate([band[:, :2 * BLOCK], masked, far], axis=1)
    else:
        o_ref[0, :, 0] = acc


def _bias_tiles(table, bkt, token_variants=False):
    n_ids, n_heads = table.shape
    v, r, c = (3, BLOCK, 4 * BLOCK) if token_variants else bkt.shape
    return pl.pallas_call(
        functools.partial(_bias_kernel, n_ids=n_ids, token_variants=token_variants),
        grid=(n_heads,),
        in_specs=[pl.BlockSpec(memory_space=pltpu.SMEM),
                  pl.BlockSpec(bkt.shape, lambda h: (0, 0, 0))],
        out_specs=pl.BlockSpec((1, v, 1, r, c), lambda h: (h // GROUP, 0, h % GROUP, 0, 0)),
        out_shape=jax.ShapeDtypeStruct((n_heads // GROUP, v, GROUP, r, c), F32),
        compiler_params=_params(("parallel",), 2 * _nbytes(bkt.shape, F32)),
        name="bias_tiles",
    )(table, bkt)


def _softmax_pv(q, k, v, bias2):
    s = lax.dot_general(q, k, (((1,), (1,)), ((), ())), preferred_element_type=F32) + bias2
    m = jnp.max(s, axis=-1, keepdims=True)
    p = jnp.exp2(s - m)
    denom = jnp.sum(p, axis=-1, keepdims=True)
    return jnp.dot(p.astype(BF16), v, preferred_element_type=F32) * (1.0 / denom)


def _fill_meta_keys(km_scr, vm_scr, km_ref, vm_ref, kg, n_meta):
    km_scr[...] = jnp.zeros(km_scr.shape, BF16)
    vm_scr[...] = jnp.zeros(vm_scr.shape, BF16)
    km_scr[0:n_meta] = _rms(km_ref[...].astype(F32), kg).astype(BF16)
    vm_scr[0:n_meta] = vm_ref[...].astype(BF16)


def _attn_kernel(qg_ref, kg_ref, bias_ref, q_ref, kp_ref, kc_ref, kn_ref, vp_ref, vc_ref, vn_ref,
                 km_ref, vm_ref, o_ref, q_scr, k_scr, v_scr, km_scr, vm_scr, *, chunks, seq_chunks, n_meta, scale):
    r = pl.program_id(1)
    qg = qg_ref[...] * (scale * LOG2E)
    kg = kg_ref[...]
    body = chunks * BLOCK
    stack = GROUP * BLOCK

    k_scr[0:BLOCK] = _rms(kp_ref[...].astype(F32), kg).astype(BF16)
    k_scr[BLOCK:BLOCK + body] = _rms(kc_ref[...].astype(F32), kg).astype(BF16)
    k_scr[BLOCK + body:2 * BLOCK + body] = _rms(kn_ref[...].astype(F32), kg).astype(BF16)
    v_scr[0:BLOCK] = vp_ref[...].astype(BF16)
    v_scr[BLOCK:BLOCK + body] = vc_ref[...].astype(BF16)
    v_scr[BLOCK + body:2 * BLOCK + body] = vn_ref[...].astype(BF16)
    _fill_meta_keys(km_scr, vm_scr, km_ref, vm_ref, kg, n_meta)
    for g in range(GROUP):
        qn = _rms(q_ref[:, g * HEAD_DIM:(g + 1) * HEAD_DIM].astype(F32), qg).astype(BF16)
        for c in range(chunks):
            q_scr[(c * GROUP + g) * BLOCK:(c * GROUP + g + 1) * BLOCK] = qn[c * BLOCK:(c + 1) * BLOCK]
    k_meta = km_scr[...]
    v_meta = vm_scr[...]

    def chunk(cc, carry):
        gch = r * chunks + cc
        is_first = functools.reduce(jnp.logical_or, [gch == s0 for s0, _ in seq_chunks])
        is_last = functools.reduce(jnp.logical_or, [gch == s0 + n - 1 for s0, n in seq_chunks])
        var = jnp.where(is_first, 1, jnp.where(is_last, 2, 0))
        row0 = pl.multiple_of(cc * BLOCK, BLOCK)
        k_all = jnp.concatenate([k_scr[pl.ds(row0, 3 * BLOCK), :], k_meta], axis=0)
        v_all = jnp.concatenate([v_scr[pl.ds(row0, 3 * BLOCK), :], v_meta], axis=0)
        q = q_scr[pl.ds(pl.multiple_of(cc * stack, stack), stack), :]
        bias = bias_ref[0, var].reshape(stack, 4 * BLOCK)
        o = _softmax_pv(q, k_all, v_all, bias)
        for g in range(GROUP):
            o_ref[pl.ds(row0, BLOCK), g * HEAD_DIM:(g + 1) * HEAD_DIM] = o[g * BLOCK:(g + 1) * BLOCK]
        return carry

    lax.fori_loop(0, chunks, chunk, 0, unroll=True)


def _seq_of_chunk(c, seq_chunks):
    s = 0
    for s0, _ in seq_chunks[1:]:
        s = s + (c >= s0).astype(jnp.int32)
    return s


def _attention(z, zm, bias_tok, q_g, k_g, *, seq_chunks, n_meta, d_model, k_col, v_col):
    m = z.shape[0]
    n_chunks = m // BLOCK
    kv_heads = d_model // (GROUP * HEAD_DIM)
    chunks = math.gcd(16, *[n for _, n in seq_chunks])
    body = chunks * BLOCK
    qw = GROUP * HEAD_DIM
    kb, vb = k_col // HEAD_DIM, v_col // HEAD_DIM
    seq_of = functools.partial(_seq_of_chunk, seq_chunks=seq_chunks)

    def halo_prev(h, r):
        return jnp.maximum(r * chunks - 1, 0)

    def halo_next(h, r):
        return jnp.minimum(r * chunks + chunks, n_chunks - 1)

    in_specs = [
        pl.BlockSpec((1, HEAD_DIM), lambda h, r: (0, 0)),
        pl.BlockSpec((1, HEAD_DIM), lambda h, r: (0, 0)),
        pl.BlockSpec((1, 3, GROUP, BLOCK, 4 * BLOCK), lambda h, r: (h, 0, 0, 0, 0)),
        pl.BlockSpec((body, qw), lambda h, r: (r, h)),
        pl.BlockSpec((BLOCK, HEAD_DIM), lambda h, r: (halo_prev(h, r), kb + h)),
        pl.BlockSpec((body, HEAD_DIM), lambda h, r: (r, kb + h)),
        pl.BlockSpec((BLOCK, HEAD_DIM), lambda h, r: (halo_next(h, r), kb + h)),
        pl.BlockSpec((BLOCK, HEAD_DIM), lambda h, r: (halo_prev(h, r), vb + h)),
        pl.BlockSpec((body, HEAD_DIM), lambda h, r: (r, vb + h)),
        pl.BlockSpec((BLOCK, HEAD_DIM), lambda h, r: (halo_next(h, r), vb + h)),
        pl.BlockSpec((n_meta, HEAD_DIM), lambda h, r: (seq_of(r * chunks), kb + h)),
        pl.BlockSpec((n_meta, HEAD_DIM), lambda h, r: (seq_of(r * chunks), vb + h)),
    ]
    win = (_nbytes((3, GROUP, BLOCK, 4 * BLOCK), F32) + 2 * _nbytes((body, qw), F32)
           + 4 * _nbytes((body + 2 * BLOCK, HEAD_DIM), F32))
    return pl.pallas_call(
        functools.partial(_attn_kernel, chunks=chunks, seq_chunks=seq_chunks, n_meta=n_meta,
                          scale=HEAD_DIM ** -0.5),
        grid=(kv_heads, m // body),
        in_specs=in_specs,
        out_specs=pl.BlockSpec((body, qw), lambda h, r: (r, h)),
        out_shape=jax.ShapeDtypeStruct((m, d_model), F32),
        scratch_shapes=[pltpu.VMEM((body * GROUP, HEAD_DIM), BF16),
                        pltpu.VMEM((body + 2 * BLOCK, HEAD_DIM), BF16),
                        pltpu.VMEM((body + 2 * BLOCK, HEAD_DIM), BF16),
                        pltpu.VMEM((BLOCK, HEAD_DIM), BF16),
                        pltpu.VMEM((BLOCK, HEAD_DIM), BF16)],
        compiler_params=_params(("parallel", "arbitrary"), win),
        name="attention",
    )(q_g.reshape(1, HEAD_DIM), k_g.reshape(1, HEAD_DIM), bias_tok, z, z, z, z, z, z, z, zm, zm)


def _attn_meta_kernel(qg_ref, kg_ref, bias_ref, q_ref, k1_ref, v1_ref, km_ref, vm_ref, o_ref,
                      km_scr, vm_scr, *, n_meta, scale):
    qg = qg_ref[...] * (scale * LOG2E)
    kg = kg_ref[...]
    _fill_meta_keys(km_scr, vm_scr, km_ref, vm_ref, kg, n_meta)
    k_all = jnp.concatenate([_rms(k1_ref[...].astype(F32), kg).astype(BF16), km_scr[...]], axis=0)
    v_all = jnp.concatenate([v1_ref[...].astype(BF16), vm_scr[...]], axis=0)
    q = q_ref[...].astype(F32)
    qn = jnp.concatenate([_rms(q[:, g * HEAD_DIM:(g + 1) * HEAD_DIM], qg).astype(BF16) for g in range(GROUP)], axis=0)
    o = _softmax_pv(qn, k_all, v_all, bias_ref[0, 0].reshape(GROUP * n_meta, 2 * BLOCK))
    for g in range(GROUP):
        o_ref[:, g * HEAD_DIM:(g + 1) * HEAD_DIM] = o[g * n_meta:(g + 1) * n_meta]


def _attention_meta(z, zm, bias_met, q_g, k_g, *, seq_chunks, n_meta, d_model, k_col, v_col):
    n_seq = len(seq_chunks)
    kv_heads = d_model // (GROUP * HEAD_DIM)
    qw = GROUP * HEAD_DIM
    kb, vb = k_col // HEAD_DIM, v_col // HEAD_DIM

    def first_chunk(s):
        c = 0
        for i, (s0, _) in enumerate(seq_chunks):
            c = c + jnp.where(s == i, s0, 0)
        return c

    in_specs = [
        pl.BlockSpec((1, HEAD_DIM), lambda s, h: (0, 0)),
        pl.BlockSpec((1, HEAD_DIM), lambda s, h: (0, 0)),
        pl.BlockSpec((1, 1, GROUP, n_meta, 2 * BLOCK), lambda s, h: (h, 0, 0, 0, 0)),
        pl.BlockSpec((n_meta, qw), lambda s, h: (s, h)),
        pl.BlockSpec((BLOCK, HEAD_DIM), lambda s, h: (first_chunk(s), kb + h)),
        pl.BlockSpec((BLOCK, HEAD_DIM), lambda s, h: (first_chunk(s), vb + h)),
        pl.BlockSpec((n_meta, HEAD_DIM), lambda s, h: (s, kb + h)),
        pl.BlockSpec((n_meta, HEAD_DIM), lambda s, h: (s, vb + h)),
    ]
    return pl.pallas_call(
        functools.partial(_attn_meta_kernel, n_meta=n_meta, scale=HEAD_DIM ** -0.5),
        grid=(n_seq, kv_heads),
        in_specs=in_specs,
        out_specs=pl.BlockSpec((n_meta, qw), lambda s, h: (s, h)),
        out_shape=jax.ShapeDtypeStruct((n_seq * n_meta, d_model), F32),
        scratch_shapes=[pltpu.VMEM((BLOCK, HEAD_DIM), BF16), pltpu.VMEM((BLOCK, HEAD_DIM), BF16)],
        compiler_params=_params(("parallel", "parallel"), 1 << 20),
        name="attention_meta",
    )(q_g.reshape(1, HEAD_DIM), k_g.reshape(1, HEAD_DIM), bias_met, zm, z, z, zm, zm)


def _merge_math(attn, gb, gc, hc, ga, gcv, u_prev, u_next, cw, na, nc):
    u = gc * hc
    rows = u.shape[0]
    ridx = lax.broadcasted_iota(jnp.int32, u.shape, 0)
    up = jnp.where(ridx == 0, u_prev, pltpu.roll(u, 1, axis=0))
    un = jnp.where(ridx == rows - 1, u_next, pltpu.roll(u, rows - 1, axis=0))
    conv = gb * (cw[0:1] * up + cw[1:2] * u + cw[2:3] * un)
    return jax.nn.sigmoid(ga) * _rms(attn, na) + jax.nn.sigmoid(gcv) * _rms(conv, nc)


def _merge_kernel(attn_ref, gb_ref, gc_ref, hc_ref, ga_ref, gcv_ref, gcp_ref, hcp_ref, gcn_ref, hcn_ref,
                  gcm_ref, hcm_ref, cw_ref, na_ref, nc_ref, o_ref, conv_scr, *, seq_chunks):
    c = pl.program_id(0)
    is_first = functools.reduce(jnp.logical_or, [c == s0 for s0, _ in seq_chunks])
    is_last = functools.reduce(jnp.logical_or, [c == s0 + n - 1 for s0, n in seq_chunks])
    last = HALO_ROWS - 1
    rows, d = o_ref.shape
    n_tiles = d // V7X_LANES
    ridx = lax.broadcasted_iota(jnp.int32, (rows, V7X_LANES), 0)

    def cols(ct):
        return pl.ds(pl.multiple_of(ct * V7X_LANES, V7X_LANES), V7X_LANES)

    def f32(ref, sl):
        return ref[:, sl].astype(F32)

    def conv_pass(ct, carry):
        ss_attn, ss_conv = carry
        sl = cols(ct)
        u = f32(gc_ref, sl) * f32(hc_ref, sl)
        u_prev_tok = (f32(gcp_ref, sl) * f32(hcp_ref, sl))[last:last + 1]
        u_prev_meta = (f32(gcm_ref, sl) * f32(hcm_ref, sl))[last:last + 1]
        u_prev = jnp.where(is_first, u_prev_meta, u_prev_tok)
        u_next_tok = (f32(gcn_ref, sl) * f32(hcn_ref, sl))[0:1]
        u_next = jnp.where(is_last, jnp.zeros_like(u_next_tok), u_next_tok)
        up = jnp.where(ridx == 0, u_prev, pltpu.roll(u, 1, axis=0))
        un = jnp.where(ridx == rows - 1, u_next, pltpu.roll(u, rows - 1, axis=0))
        cw = cw_ref[:, sl]
        conv = f32(gb_ref, sl) * (cw[0:1] * up + cw[1:2] * u + cw[2:3] * un)
        conv_scr[:, sl] = conv
        attn = attn_ref[:, sl]
        return ss_attn + attn * attn, ss_conv + conv * conv

    zero = jnp.zeros((rows, V7X_LANES), F32)
    ss_attn, ss_conv = lax.fori_loop(0, n_tiles, conv_pass, (zero, zero), unroll=2)
    r_attn = lax.rsqrt(jnp.sum(ss_attn, axis=-1, keepdims=True) / d + EPS)
    r_conv = lax.rsqrt(jnp.sum(ss_conv, axis=-1, keepdims=True) / d + EPS)

    def gate_pass(ct, carry):
        sl = cols(ct)
        a = attn_ref[:, sl] * r_attn * na_ref[:, sl]
        c = conv_scr[:, sl] * r_conv * nc_ref[:, sl]
        o_ref[:, sl] = (jax.nn.sigmoid(f32(ga_ref, sl)) * a + jax.nn.sigmoid(f32(gcv_ref, sl)) * c).astype(o_ref.dtype)
        return carry

    lax.fori_loop(0, n_tiles, gate_pass, 0, unroll=2)


def _merge(attn, z, zm, conv_w, norm_a, norm_c, *, seq_chunks, n_meta, cols):
    m, d = attn.shape
    n_chunks = m // BLOCK
    per_chunk = BLOCK // HALO_ROWS
    per_meta = n_meta // HALO_ROWS
    gb, gc, hc, ga, gcv = [c // d for c in cols]
    seq_of = functools.partial(_seq_of_chunk, seq_chunks=seq_chunks)

    def main(col):
        return pl.BlockSpec((BLOCK, d), lambda c: (c, col))

    def prev_rows(col):
        return pl.BlockSpec((HALO_ROWS, d), lambda c: (jnp.maximum(c * per_chunk - 1, 0), col))

    def next_rows(col):
        return pl.BlockSpec((HALO_ROWS, d), lambda c: (jnp.minimum(c + 1, n_chunks - 1) * per_chunk, col))

    def meta_rows(col):
        return pl.BlockSpec((HALO_ROWS, d), lambda c: (seq_of(c) * per_meta + per_meta - 1, col))

    def row(nrows):
        return pl.BlockSpec((nrows, d), lambda c: (0, 0))

    win = 6 * _nbytes((BLOCK, d), F32) + _nbytes((BLOCK, d), BF16) + 8 * _nbytes((HALO_ROWS, d), F32)
    return pl.pallas_call(
        functools.partial(_merge_kernel, seq_chunks=seq_chunks),
        grid=(n_chunks,),
        in_specs=[main(0), main(gb), main(gc), main(hc), main(ga), main(gcv),
                  prev_rows(gc), prev_rows(hc), next_rows(gc), next_rows(hc), meta_rows(gc), meta_rows(hc),
                  row(3), row(1), row(1)],
        out_specs=pl.BlockSpec((BLOCK, d), lambda c: (c, 0)),
        out_shape=jax.ShapeDtypeStruct((m, d), BF16),
        scratch_shapes=[pltpu.VMEM((BLOCK, d), F32)],
        compiler_params=_params(("parallel",), win),
        name="merge",
    )(attn, z, z, z, z, z, z, z, z, z, zm, zm, conv_w, norm_a.reshape(1, d), norm_c.reshape(1, d))


def _merge_meta_kernel(attn_ref, gb_ref, gc_ref, hc_ref, ga_ref, gcv_ref, gcn_ref, hcn_ref,
                       cw_ref, na_ref, nc_ref, o_ref):
    f32 = lambda ref: ref[...].astype(F32)
    u_next = (f32(gcn_ref) * f32(hcn_ref))[0:1]
    u_prev = jnp.zeros_like(u_next)
    o_ref[...] = _merge_math(attn_ref[...], f32(gb_ref), f32(gc_ref), f32(hc_ref), f32(ga_ref), f32(gcv_ref),
                             u_prev, u_next, cw_ref[...], na_ref[...], nc_ref[...]).astype(o_ref.dtype)


def _merge_meta(attn_m, z, zm, conv_w, norm_a, norm_c, *, seq_chunks, n_meta, cols):
    mm, d = attn_m.shape
    per_chunk = BLOCK // HALO_ROWS
    gb, gc, hc, ga, gcv = [c // d for c in cols]

    def first_rows(s):
        r = 0
        for i, (s0, _) in enumerate(seq_chunks):
            r = r + jnp.where(s == i, s0 * per_chunk, 0)
        return r

    def main(col):
        return pl.BlockSpec((n_meta, d), lambda s: (s, col))

    def next_rows(col):
        return pl.BlockSpec((HALO_ROWS, d), lambda s: (first_rows(s), col))

    def row(nrows):
        return pl.BlockSpec((nrows, d), lambda s: (0, 0))

    win = 7 * _nbytes((n_meta, d), F32) + 2 * _nbytes((HALO_ROWS, d), F32)
    return pl.pallas_call(
        _merge_meta_kernel,
        grid=(len(seq_chunks),),
        in_specs=[main(0), main(gb), main(gc), main(hc), main(ga), main(gcv), next_rows(gc), next_rows(hc),
                  row(3), row(1), row(1)],
        out_specs=pl.BlockSpec((n_meta, d), lambda s: (s, 0)),
        out_shape=jax.ShapeDtypeStruct((mm, d), BF16),
        compiler_params=_params(("parallel",), win),
        name="merge_meta",
    )(attn_m, zm, zm, zm, zm, zm, z, z, conv_w, norm_a.reshape(1, d), norm_c.reshape(1, d))


def kernel(x_prompt, x_sample, meta_tokens, rel_bias, norm1_g, w_in, q_norm_g, k_norm_g, attn_sink, conv_w,
           branch_norm_a, branch_norm_c, w_out, norm2_g, w_ffn_gate, w_ffn_up, w_ffn_down):
    d = x_prompt.shape[-1]
    n_meta = meta_tokens.shape[0]
    n_buckets = rel_bias.shape[0]
    depth, _, in_dim = w_in.shape
    ffn = w_ffn_gate.shape[-1]
    kv_dim = (in_dim - 6 * d) // 2
    assert d % (GROUP * HEAD_DIM) == 0 and kv_dim == d // GROUP and attn_sink.shape[1] * HEAD_DIM == d
    assert n_meta % HALO_ROWS == 0 and n_meta < BLOCK

    groups = (x_prompt, x_sample)
    seq_chunks = []
    for xg in groups:
        assert xg.shape[1] % BLOCK == 0 and xg.shape[1] >= 2 * BLOCK
        for _ in range(xg.shape[0]):
            start = seq_chunks[-1][0] + seq_chunks[-1][1] if seq_chunks else 0
            seq_chunks.append((start, xg.shape[1] // BLOCK))
    seq_chunks = tuple(seq_chunks)
    n_seq = len(seq_chunks)

    x_parts = [xg.reshape(-1, d) for xg in groups]
    xm = jnp.tile(meta_tokens.astype(F32), (n_seq, 1))

    cols = tuple(d * i for i in range(1, 6))
    k_col, v_col = 6 * d, 6 * d + kv_dim
    ffn_tile = 1024
    ffn_p = -(-ffn // ffn_tile) * ffn_tile
    out_tile = _divisor_tile(d, 512, V7X_LANES)
    in_tile = 2 * kv_dim
    n_q, n_rest = d // in_tile, 5 * d // in_tile

    def regroup(j):
        return jnp.where(j < n_q, j, jnp.where(j < n_q + n_rest, j + 1, n_q))

    bkt_tok, bkt_met = _bucket_tables(n_meta, n_buckets)

    geo = dict(seq_chunks=seq_chunks, n_meta=n_meta)
    xb, ss = _prep(x_parts)
    xmb, ssm = _prep([xm])
    for l in range(depth):
        with_meta = l < depth - 1
        table = jnp.concatenate([rel_bias, attn_sink[l][None]], axis=0)
        z, zm = _wmm(xb, xmb, [w_in], l, in_dim, in_tile, BF16, src_block=regroup, norm=(norm1_g[l], ss, ssm),
                     bm_cap=512)
        attn = _attention(z, zm, _bias_tiles(table, bkt_tok, token_variants=True), q_norm_g[l], k_norm_g[l],
                          d_model=d, k_col=k_col, v_col=v_col, **geo)
        merged = _merge(attn, z, zm, conv_w[l], branch_norm_a[l], branch_norm_c[l], cols=cols, **geo)
        merged_m = None
        if with_meta:
            attn_m = _attention_meta(z, zm, _bias_tiles(table, bkt_met), q_norm_g[l], k_norm_g[l],
                                     d_model=d, k_col=k_col, v_col=v_col, **geo)
            merged_m = _merge_meta(attn_m, z, zm, conv_w[l], branch_norm_a[l], branch_norm_c[l], cols=cols, **geo)
        h, hb, hss, *hm = _wmm(merged, merged_m, [w_out], l, d, out_tile, F32, res=x_parts,
                               auxres=xm if with_meta else None, emit_stats=True)
        hm, hmb, hmss = hm if with_meta else (None, None, None)
        t, *tm, wd_b = _wmm(hb, hmb, [w_ffn_gate, w_ffn_up], l, ffn_p, ffn_tile, BF16, act=True,
                            side=(w_ffn_down, ffn_p), norm=(norm2_g[l], hss, hmss), bm_cap=512)
        if with_meta:
            x, xb, ss = _matmul_res(t, wd_b, h, emit_stats=True, k_real=ffn)
            x_parts = [x]
            xm, xmb, ssm = _matmul_res(tm[0], wd_b, hm, emit_stats=True, k_real=ffn)

    outs = []
    row = 0
    for xg in groups:
        nrows = xg.shape[0] * xg.shape[1]
        outs.append(_matmul_res(t, wd_b, h, row, nrows, k_real=ffn)[0].reshape(xg.shape))
        row += nrows
    return tuple(outs)
```

```python
import functools
import math

import jax
import jax.numpy as jnp
from jax import lax
from jax.experimental import pallas as pl
from jax.experimental.pallas import tpu as pltpu

HEAD_DIM = 128
GROUP = 4
BLOCK = 128
MAX_DISTANCE = 128
EPS = 1e-6
LOG2E = math.log2(math.e)

V7X_VMEM_BYTES = 64 << 20
V7X_LANES = 128
BF16_SUBLANES = 16
F32_SUBLANES = 8
HALO_ROWS = BF16_SUBLANES
VMEM_SLACK_BYTES = 10 << 20

F32 = jnp.float32
BF16 = jnp.bfloat16


def _params(dims, window_bytes):
    limit = min(2 * window_bytes + VMEM_SLACK_BYTES, V7X_VMEM_BYTES - (6 << 20))
    return pltpu.CompilerParams(dimension_semantics=dims, vmem_limit_bytes=int(limit))


def _divisor_tile(n, cap, unit):
    if n <= cap:
        return n
    t = (cap // unit) * unit
    while t >= unit:
        if n % t == 0:
            return t
        t -= unit
    raise ValueError(f"no tile for {n}")


def _nbytes(shape, dtype):
    return math.prod(shape) * jnp.dtype(dtype).itemsize


def _rms(x, g):
    ms = jnp.mean(x * x, axis=-1, keepdims=True)
    return x * lax.rsqrt(ms + EPS) * g


def _row_sumsq(x):
    return jnp.broadcast_to(jnp.sum(x * x, axis=-1, keepdims=True), (x.shape[0], V7X_LANES))


def _rstd(ss_ref, width):
    ss = ss_ref[0]
    for p in range(1, ss_ref.shape[0]):
        ss = ss + ss_ref[p]
    return lax.rsqrt(ss / width + EPS)


def _prep_kernel(*refs, starts):
    x_refs, o_ref, ss_ref = refs[:len(starts)], refs[-2], refs[-1]
    i = pl.program_id(0)
    ends = starts[1:] + (None,)
    for x_ref, lo, hi in zip(x_refs, starts, ends):
        in_seg = i >= lo if hi is None else jnp.logical_and(i >= lo, i < hi)

        @pl.when(in_seg)
        def _():
            x = x_ref[...]
            o_ref[...] = x.astype(o_ref.dtype)
            ss_ref[0] = _row_sumsq(x)


def _prep(xs):
    d = xs[0].shape[1]
    m = sum(x.shape[0] for x in xs)
    bm = _divisor_tile(math.gcd(*[x.shape[0] for x in xs]), 256, BF16_SUBLANES)
    blocks = [x.shape[0] // bm for x in xs]
    starts = tuple(sum(blocks[:s]) for s in range(len(xs)))
    in_specs = [pl.BlockSpec((bm, d), lambda i, start=start, nblk=nblk: (jnp.clip(i - start, 0, nblk - 1), 0))
                for start, nblk in zip(starts, blocks)]
    return pl.pallas_call(
        functools.partial(_prep_kernel, starts=starts),
        grid=(m // bm,),
        in_specs=in_specs,
        out_specs=[pl.BlockSpec((bm, d), lambda i: (i, 0)), pl.BlockSpec((1, bm, V7X_LANES), lambda i: (0, i, 0))],
        out_shape=[jax.ShapeDtypeStruct((m, d), BF16), jax.ShapeDtypeStruct((1, m, V7X_LANES), F32)],
        compiler_params=_params(("parallel",), len(xs) * _nbytes((bm, d), F32) + _nbytes((bm, d), BF16)),
        name="prep",
    )(*xs)


def _wmm_kernel(*refs, n_w, res_starts, has_aux, has_side, has_norm, emit_stats, act, n_tiles, rows, src_cols,
                side_rows):
    it = iter(refs)
    a_ref = next(it)
    aux_ref = next(it) if has_aux else None
    w_refs = [next(it) for _ in range(n_w)]
    gain_ref = next(it) if has_norm else None
    ss_ref = next(it) if has_norm else None
    auxss_ref = next(it) if has_norm and has_aux else None
    res_refs = [next(it) for _ in res_starts]
    auxres_ref = next(it) if res_starts and has_aux else None
    side_ref = next(it) if has_side else None
    o_ref = next(it)
    ob_ref, oss_ref = (next(it), next(it)) if emit_stats else (None, None)
    oaux_ref = next(it) if has_aux else None
    oauxb_ref, oauxss_ref = (next(it), next(it)) if emit_stats and has_aux else (None, None)
    oside_ref = next(it) if has_side else None
    w_bufs = (next(it), next(it))
    k_dim = w_bufs[0].shape[1]
    g = pl.program_id(0)
    i = pl.program_id(1)
    bn = w_bufs[0].shape[-1]

    if has_side:
        w = side_ref[...]
        row = g * w.shape[0] + lax.broadcasted_iota(jnp.int32, w.shape, 0)
        oside_ref[...] = jnp.where(row < side_rows, w, 0.0).astype(oside_ref.dtype)

    def stage(buf):
        row0 = pl.multiple_of(i * rows, rows)
        for t in range(n_w):
            w = w_refs[t][...]
            if has_norm:
                w = w * jnp.tile(gain_ref[...], (1, bn // V7X_LANES))
            if src_cols % bn:
                col = jnp.minimum(g, n_tiles - 1) * bn + lax.broadcasted_iota(jnp.int32, w.shape, 1)
                w = jnp.where(col < src_cols, w, 0.0)
            buf[t, pl.ds(row0, rows), :] = w.astype(buf.dtype)

    def apply(buf, ncols, a, ss, res, out, out_b, out_ss):
        ys = [jnp.dot(a, buf[t, :, 0:ncols], preferred_element_type=F32) for t in range(n_w)]
        if has_norm:
            rstd = jnp.tile(_rstd(ss, k_dim), (1, ncols // V7X_LANES))
            ys = [y * rstd for y in ys]
        y = ys[0] * jax.nn.sigmoid(ys[0]) * ys[1] if act else ys[0]
        y = y if res is None else res + y
        out[:, 0:ncols] = y.astype(out.dtype)
        if ncols < bn:
            out[:, ncols:] = jnp.zeros((y.shape[0], bn - ncols), out.dtype)
        if emit_stats:
            out_b[...] = y.astype(out_b.dtype)
            out_ss[0] = _row_sumsq(y)

    def compute(buf, ncols):
        res = None
        if res_starts:
            res = res_refs[0][...]
            for start, ref in zip(res_starts[1:], res_refs[1:]):
                res = jnp.where(i >= start, ref[...], res)
        apply(buf, ncols, a_ref[...], ss_ref, res, o_ref, ob_ref, oss_ref)
        if has_aux:
            @pl.when(i == 0)
            def _():
                apply(buf, ncols, aux_ref[...], auxss_ref, auxres_ref[...] if res_starts else None,
                      oaux_ref, oauxb_ref, oauxss_ref)

    @pl.when(g == 0)
    def _():
        stage(w_bufs[0])

    last_cols = src_cols % bn
    full_steps = n_tiles - 1 if last_cols else n_tiles
    for parity in range(2):
        @pl.when(jnp.logical_and(jnp.logical_and(g > 0, g <= full_steps), g % 2 == parity))
        def _():
            stage(w_bufs[parity])
            compute(w_bufs[1 - parity], bn)

    if last_cols:
        assert last_cols % V7X_LANES == 0 and not res_starts and not emit_stats

        @pl.when(g == n_tiles)
        def _():
            compute(w_bufs[(n_tiles - 1) % 2], last_cols)


def _wmm(a, aux, ws, layer, n_out, bn, out_dtype, *, src_block=None, act=False, res=None, auxres=None, side=None,
         norm=None, emit_stats=False, bm_cap=1024):
    m, k = a.shape
    src_cols = ws[0].shape[2]
    bm = _divisor_tile(m, bm_cap, BF16_SUBLANES)
    ni, n_tiles = m // bm, n_out // bn
    rows = k // ni
    assert k % ni == 0 and rows % BF16_SUBLANES == 0 and n_out % bn == 0
    assert src_block is None or src_cols % bn == 0
    src = src_block if src_block is not None else (lambda j: j)
    res = list(res) if res is not None else []
    has_aux, has_side, has_norm = aux is not None, side is not None, norm is not None
    ma = aux.shape[0] if has_aux else 0
    res_blocks = [r.shape[0] // bm for r in res]
    assert all(r.shape[0] % bm == 0 for r in res) and sum(res_blocks) == (ni if res else 0)
    res_starts = tuple(sum(res_blocks[:s]) for s in range(len(res)))

    def row_blk(g, i):
        return jnp.where(g == 0, 0, i)

    def col_blk(g):
        return jnp.maximum(g - 1, 0)

    def last_tile(g):
        return jnp.minimum(g, n_tiles - 1)

    in_specs = [pl.BlockSpec((bm, k), lambda g, i: (row_blk(g, i), 0))]
    args = [a]
    if has_aux:
        in_specs.append(pl.BlockSpec((ma, k), lambda g, i: (0, 0)))
        args.append(aux)
    for w in ws:
        in_specs.append(pl.BlockSpec((None, rows, bn), lambda g, i: (layer, i, src(last_tile(g)))))
        args.append(w)
    norm_bytes = 0
    if has_norm:
        gain, ss, ss_aux = norm
        in_specs.append(pl.BlockSpec((rows, V7X_LANES), lambda g, i: (i, 0)))
        args.append(jnp.broadcast_to(gain[:, None], (k, V7X_LANES)))
        in_specs.append(pl.BlockSpec((ss.shape[0], bm, V7X_LANES), lambda g, i: (0, row_blk(g, i), 0)))
        args.append(ss)
        if has_aux:
            in_specs.append(pl.BlockSpec((ss_aux.shape[0], ma, V7X_LANES), lambda g, i: (0, 0, 0)))
            args.append(ss_aux)
        norm_bytes = _nbytes((rows + ss.shape[0] * (bm + ma), V7X_LANES), F32)
    for start, nblk, r in zip(res_starts, res_blocks, res):
        in_specs.append(pl.BlockSpec(
            (bm, bn), lambda g, i, start=start, nblk=nblk: (jnp.clip(row_blk(g, i) - start, 0, nblk - 1), col_blk(g))))
        args.append(r)
    if res and has_aux:
        in_specs.append(pl.BlockSpec((ma, bn), lambda g, i: (0, col_blk(g))))
        args.append(auxres)
    out_specs, out_shape = [], []

    def add_outputs(nrows, blk_rows, row_index):
        out_specs.append(pl.BlockSpec((blk_rows, bn), lambda g, i: (row_index(g, i), col_blk(g))))
        out_shape.append(jax.ShapeDtypeStruct((nrows, n_out), out_dtype))
        if emit_stats:
            out_specs.append(pl.BlockSpec((blk_rows, bn), lambda g, i: (row_index(g, i), col_blk(g))))
            out_shape.append(jax.ShapeDtypeStruct((nrows, n_out), BF16))
            out_specs.append(pl.BlockSpec((1, blk_rows, V7X_LANES), lambda g, i: (col_blk(g), row_index(g, i), 0)))
            out_shape.append(jax.ShapeDtypeStruct((n_tiles, nrows, V7X_LANES), F32))

    add_outputs(m, bm, row_blk)
    if has_aux:
        add_outputs(ma, ma, lambda g, i: 0)
    side_rows = side_bytes = 0
    if has_side:
        side_w, side_out_rows = side
        side_rows, side_cols = side_w.shape[1:]
        sr, sc = side_out_rows // n_tiles, side_cols // ni
        assert side_out_rows % n_tiles == 0 and side_cols % ni == 0 and sr % BF16_SUBLANES == 0 and sc % V7X_LANES == 0
        in_specs.append(pl.BlockSpec((None, sr, sc), lambda g, i: (layer, last_tile(g), i)))
        args.append(side_w)
        out_specs.append(pl.BlockSpec((sr, sc), lambda g, i: (g, i)))
        out_shape.append(jax.ShapeDtypeStruct((side_out_rows + sr, side_cols), BF16))
        side_bytes = _nbytes((sr, sc), F32) + _nbytes((sr, sc), BF16)
    win = (_nbytes((bm + ma, k), BF16) + len(ws) * _nbytes((rows, bn), F32) + side_bytes + norm_bytes
           + _nbytes((bm + ma, bn), out_dtype) + len(res) * _nbytes((bm + ma, bn), F32)
           + emit_stats * (_nbytes((bm + ma, bn), BF16) + _nbytes((bm + ma, V7X_LANES), F32)))
    scratch = _nbytes((2, len(ws), k, bn), BF16)
    return pl.pallas_call(
        functools.partial(_wmm_kernel, n_w=len(ws), res_starts=res_starts, has_aux=has_aux, has_side=has_side,
                          has_norm=has_norm, emit_stats=emit_stats, act=act, n_tiles=n_tiles, rows=rows,
                          src_cols=src_cols, side_rows=side_rows),
        grid=(n_tiles + 1, ni),
        in_specs=in_specs,
        out_specs=out_specs,
        out_shape=out_shape,
        scratch_shapes=[pltpu.VMEM((len(ws), k, bn), BF16), pltpu.VMEM((len(ws), k, bn), BF16)],
        compiler_params=_params(("arbitrary", "arbitrary"), win + scratch // 2),
        name="wmm",
    )(*args)


def _mm_res_kernel(a_ref, b_ref, r_ref, o_ref, *stats, last_k):
    j, kk = pl.program_id(1), pl.program_id(2)
    n_k = pl.num_programs(2)

    @pl.when(kk == 0)
    def _():
        o_ref[...] = r_ref[...]

    if last_k == a_ref.shape[1]:
        o_ref[...] += jnp.dot(a_ref[...], b_ref[...], preferred_element_type=F32)
    else:
        @pl.when(kk < n_k - 1)
        def _():
            o_ref[...] += jnp.dot(a_ref[...], b_ref[...], preferred_element_type=F32)

        @pl.when(kk == n_k - 1)
        def _():
            o_ref[...] += jnp.dot(a_ref[:, 0:last_k], b_ref[0:last_k, :], preferred_element_type=F32)
    if stats:
        ob_ref, ss_ref = stats

        @pl.when(kk == pl.num_programs(2) - 1)
        def _():
            y = o_ref[...]
            ob_ref[...] = y.astype(ob_ref.dtype)

            @pl.when(j == 0)
            def _():
                ss_ref[0] = _row_sumsq(y)

            @pl.when(j > 0)
            def _():
                ss_ref[0] += _row_sumsq(y)


def _matmul_res(a, b, res, row0=0, nrows=None, emit_stats=False, k_real=None):
    m, k = a.shape
    n = b.shape[1]
    nrows = m if nrows is None else nrows
    bm = _divisor_tile(nrows, 1024, BF16_SUBLANES)
    bn = _divisor_tile(n, 1024, V7X_LANES)
    bk = _divisor_tile(k, 4096, V7X_LANES)
    assert row0 % bm == 0
    off = row0 // bm
    win = _nbytes((bm, bk), BF16) + _nbytes((bk, bn), BF16) + 2 * _nbytes((bm, bn), F32)
    last_k = bk if k_real is None else k_real - (k // bk - 1) * bk
    assert 0 < last_k <= bk and last_k % V7X_LANES == 0
    return pl.pallas_call(
        functools.partial(_mm_res_kernel, last_k=last_k),
        grid=(nrows // bm, n // bn, k // bk),
        in_specs=[pl.BlockSpec((bm, bk), lambda i, j, kk: (i + off, kk)),
                  pl.BlockSpec((bk, bn), lambda i, j, kk: (kk, j)),
                  pl.BlockSpec((bm, bn), lambda i, j, kk: (i + off, j))],
        out_specs=[pl.BlockSpec((bm, bn), lambda i, j, kk: (i, j))] + emit_stats * [
            pl.BlockSpec((bm, bn), lambda i, j, kk: (i, j)),
            pl.BlockSpec((1, bm, V7X_LANES), lambda i, j, kk: (0, i, 0))],
        out_shape=[jax.ShapeDtypeStruct((nrows, n), F32)] + emit_stats * [
            jax.ShapeDtypeStruct((nrows, n), BF16), jax.ShapeDtypeStruct((1, nrows, V7X_LANES), F32)],
        compiler_params=_params(("parallel", "arbitrary", "arbitrary"), win + _nbytes((bm, bn), BF16)),
        name="matmul_res",
    )(a, b, res)


def _t5_bucket(rel, n_buckets):
    half = n_buckets // 2
    exact = half // 2
    n = jnp.abs(rel)
    n_f = jnp.maximum(n, 1).astype(F32)
    large = exact + (jnp.log(n_f / exact) / math.log(MAX_DISTANCE / exact) * (half - exact)).astype(jnp.int32)
    large = jnp.minimum(large, half - 1)
    return jnp.where(rel > 0, half, 0) + jnp.where(n < exact, n, large)


def _bucket_tables(n_meta, n_buckets):
    qi = jnp.arange(BLOCK)[:, None]
    sj = jnp.arange(3 * BLOCK)[None, :]
    rel_band = sj - BLOCK - qi
    band = jnp.where(jnp.abs(rel_band) <= BLOCK, _t5_bucket(rel_band, n_buckets), -1)
    sink = jnp.full((BLOCK, 1), n_buckets, jnp.int32)
    pad = jnp.full((BLOCK, BLOCK - n_meta - 1), -1, jnp.int32)
    mk = jnp.arange(n_meta)[None, :]
    meta_first = _t5_bucket(mk - (n_meta + qi), n_buckets)
    meta_far = jnp.full((BLOCK, n_meta), n_buckets // 2 - 1, jnp.int32)
    tok = jnp.concatenate([band, meta_far, sink, pad, meta_first, sink, pad], axis=1)[None].astype(jnp.int32)

    mq = jnp.arange(n_meta)[:, None]
    tk = jnp.arange(BLOCK)[None, :]
    rel_tok = n_meta + tk - mq
    mband = jnp.where(jnp.abs(rel_tok) <= BLOCK, _t5_bucket(rel_tok, n_buckets), -1)
    mmeta = _t5_bucket(mk - mq, n_buckets)
    met = jnp.concatenate([mband, mmeta, sink[:n_meta], pad[:n_meta]], axis=1).astype(jnp.int32)
    return tok, met[None]


def _bias_kernel(tab_ref, bkt_ref, o_ref, *, n_ids, token_variants):
    h = pl.program_id(0)
    bkt = bkt_ref[...]
    acc = jnp.full(bkt.shape, -jnp.inf, F32)
    for b in range(n_ids):
        acc = jnp.where(bkt == b, tab_ref[b, h], acc)
    acc = acc * LOG2E
    if token_variants:
        band, far, first = acc[0, :, :3 * BLOCK], acc[0, :, 3 * BLOCK:4 * BLOCK], acc[0, :, 4 * BLOCK:]
        masked = jnp.full((BLOCK, BLOCK), -jnp.inf, F32)
        o_ref[0, 0, 0] = jnp.concatenate([band, far], axis=1)
        o_ref[0, 1, 0] = jnp.concatenate([masked, band[:, BLOCK:], first], axis=1)
        o_ref[0, 2, 0] = jnp.concatenate([band[:, :2 * BLOCK], masked, far], axis=1)
    else:
        o_ref[0, :, 0] = acc


def _bias_tiles(table, bkt, token_variants=False):
    n_ids, n_heads = table.shape
    v, r, c = (3, BLOCK, 4 * BLOCK) if token_variants else bkt.shape
    return pl.pallas_call(
        functools.partial(_bias_kernel, n_ids=n_ids, token_variants=token_variants),
        grid=(n_heads,),
        in_specs=[pl.BlockSpec(memory_space=pltpu.SMEM),
                  pl.BlockSpec(bkt.shape, lambda h: (0, 0, 0))],
        out_specs=pl.BlockSpec((1, v, 1, r, c), lambda h: (h // GROUP, 0, h % GROUP, 0, 0)),
        out_shape=jax.ShapeDtypeStruct((n_heads // GROUP, v, GROUP, r, c), F32),
        compiler_params=_params(("parallel",), 2 * _nbytes(bkt.shape, F32)),
        name="bias_tiles",
    )(table, bkt)


def _softmax_pv(q, k, v, bias2):
    s = lax.dot_general(q, k, (((1,), (1,)), ((), ())), preferred_element_type=F32) + bias2
    m = jnp.max(s, axis=-1, keepdims=True)
    p = jnp.exp2(s - m)
    denom = jnp.sum(p, axis=-1, keepdims=True)
    return jnp.dot(p.astype(BF16), v, preferred_element_type=F32) * (1.0 / denom)


def _fill_meta_keys(km_scr, vm_scr, km_ref, vm_ref, kg, n_meta):
    km_scr[...] = jnp.zeros(km_scr.shape, BF16)
    vm_scr[...] = jnp.zeros(vm_scr.shape, BF16)
    km_scr[0:n_meta] = _rms(km_ref[...].astype(F32), kg).astype(BF16)
    vm_scr[0:n_meta] = vm_ref[...].astype(BF16)


def _attn_kernel(qg_ref, kg_ref, bias_ref, q_ref, kp_ref, kc_ref, kn_ref, vp_ref, vc_ref, vn_ref,
                 km_ref, vm_ref, o_ref, q_scr, k_scr, v_scr, km_scr, vm_scr, *, chunks, seq_chunks, n_meta, scale):
    r = pl.program_id(1)
    qg = qg_ref[...] * (scale * LOG2E)
    kg = kg_ref[...]
    body = chunks * BLOCK
    stack = GROUP * BLOCK

    k_scr[0:BLOCK] = _rms(kp_ref[...].astype(F32), kg).astype(BF16)
    k_scr[BLOCK:BLOCK + body] = _rms(kc_ref[...].astype(F32), kg).astype(BF16)
    k_scr[BLOCK + body:2 * BLOCK + body] = _rms(kn_ref[...].astype(F32), kg).astype(BF16)
    v_scr[0:BLOCK] = vp_ref[...].astype(BF16)
    v_scr[BLOCK:BLOCK + body] = vc_ref[...].astype(BF16)
    v_scr[BLOCK + body:2 * BLOCK + body] = vn_ref[...].astype(BF16)
    _fill_meta_keys(km_scr, vm_scr, km_ref, vm_ref, kg, n_meta)
    for g in range(GROUP):
        qn = _rms(q_ref[:, g * HEAD_DIM:(g + 1) * HEAD_DIM].astype(F32), qg).astype(BF16)
        for c in range(chunks):
            q_scr[(c * GROUP + g) * BLOCK:(c * GROUP + g + 1) * BLOCK] = qn[c * BLOCK:(c + 1) * BLOCK]
    k_meta = km_scr[...]
    v_meta = vm_scr[...]

    def chunk(cc, carry):
        gch = r * chunks + cc
        is_first = functools.reduce(jnp.logical_or, [gch == s0 for s0, _ in seq_chunks])
        is_last = functools.reduce(jnp.logical_or, [gch == s0 + n - 1 for s0, n in seq_chunks])
        var = jnp.where(is_first, 1, jnp.where(is_last, 2, 0))
        row0 = pl.multiple_of(cc * BLOCK, BLOCK)
        k_all = jnp.concatenate([k_scr[pl.ds(row0, 3 * BLOCK), :], k_meta], axis=0)
        v_all = jnp.concatenate([v_scr[pl.ds(row0, 3 * BLOCK), :], v_meta], axis=0)
        q = q_scr[pl.ds(pl.multiple_of(cc * stack, stack), stack), :]
        bias = bias_ref[0, var].reshape(stack, 4 * BLOCK)
        o = _softmax_pv(q, k_all, v_all, bias)
        for g in range(GROUP):
            o_ref[pl.ds(row0, BLOCK), g * HEAD_DIM:(g + 1) * HEAD_DIM] = o[g * BLOCK:(g + 1) * BLOCK]
        return carry

    lax.fori_loop(0, chunks, chunk, 0, unroll=True)


def _seq_of_chunk(c, seq_chunks):
    s = 0
    for s0, _ in seq_chunks[1:]:
        s = s + (c >= s0).astype(jnp.int32)
    return s


def _attention(z, zm, bias_tok, q_g, k_g, *, seq_chunks, n_meta, d_model, k_col, v_col):
    m = z.shape[0]
    n_chunks = m // BLOCK
    kv_heads = d_model // (GROUP * HEAD_DIM)
    chunks = math.gcd(16, *[n for _, n in seq_chunks])
    body = chunks * BLOCK
    qw = GROUP * HEAD_DIM
    kb, vb = k_col // HEAD_DIM, v_col // HEAD_DIM
    seq_of = functools.partial(_seq_of_chunk, seq_chunks=seq_chunks)

    def halo_prev(h, r):
        return jnp.maximum(r * chunks - 1, 0)

    def halo_next(h, r):
        return jnp.minimum(r * chunks + chunks, n_chunks - 1)

    in_specs = [
        pl.BlockSpec((1, HEAD_DIM), lambda h, r: (0, 0)),
        pl.BlockSpec((1, HEAD_DIM), lambda h, r: (0, 0)),
        pl.BlockSpec((1, 3, GROUP, BLOCK, 4 * BLOCK), lambda h, r: (h, 0, 0, 0, 0)),
        pl.BlockSpec((body, qw), lambda h, r: (r, h)),
        pl.BlockSpec((BLOCK, HEAD_DIM), lambda h, r: (halo_prev(h, r), kb + h)),
        pl.BlockSpec((body, HEAD_DIM), lambda h, r: (r, kb + h)),
        pl.BlockSpec((BLOCK, HEAD_DIM), lambda h, r: (halo_next(h, r), kb + h)),
        pl.BlockSpec((BLOCK, HEAD_DIM), lambda h, r: (halo_prev(h, r), vb + h)),
        pl.BlockSpec((body, HEAD_DIM), lambda h, r: (r, vb + h)),
        pl.BlockSpec((BLOCK, HEAD_DIM), lambda h, r: (halo_next(h, r), vb + h)),
        pl.BlockSpec((n_meta, HEAD_DIM), lambda h, r: (seq_of(r * chunks), kb + h)),
        pl.BlockSpec((n_meta, HEAD_DIM), lambda h, r: (seq_of(r * chunks), vb + h)),
    ]
    win = (_nbytes((3, GROUP, BLOCK, 4 * BLOCK), F32) + 2 * _nbytes((body, qw), F32)
           + 4 * _nbytes((body + 2 * BLOCK, HEAD_DIM), F32))
    return pl.pallas_call(
        functools.partial(_attn_kernel, chunks=chunks, seq_chunks=seq_chunks, n_meta=n_meta,
                          scale=HEAD_DIM ** -0.5),
        grid=(kv_heads, m // body),
        in_specs=in_specs,
        out_specs=pl.BlockSpec((body, qw), lambda h, r: (r, h)),
        out_shape=jax.ShapeDtypeStruct((m, d_model), F32),
        scratch_shapes=[pltpu.VMEM((body * GROUP, HEAD_DIM), BF16),
                        pltpu.VMEM((body + 2 * BLOCK, HEAD_DIM), BF16),
                        pltpu.VMEM((body + 2 * BLOCK, HEAD_DIM), BF16),
                        pltpu.VMEM((BLOCK, HEAD_DIM), BF16),
                        pltpu.VMEM((BLOCK, HEAD_DIM), BF16)],
        compiler_params=_params(("parallel", "arbitrary"), win),
        name="attention",
    )(q_g.reshape(1, HEAD_DIM), k_g.reshape(1, HEAD_DIM), bias_tok, z, z, z, z, z, z, z, zm, zm)


def _attn_meta_kernel(qg_ref, kg_ref, bias_ref, q_ref, k1_ref, v1_ref, km_ref, vm_ref, o_ref,
                      km_scr, vm_scr, *, n_meta, scale):
    qg = qg_ref[...] * (scale * LOG2E)
    kg = kg_ref[...]
    _fill_meta_keys(km_scr, vm_scr, km_ref, vm_ref, kg, n_meta)
    k_all = jnp.concatenate([_rms(k1_ref[...].astype(F32), kg).astype(BF16), km_scr[...]], axis=0)
    v_all = jnp.concatenate([v1_ref[...].astype(BF16), vm_scr[...]], axis=0)
    q = q_ref[...].astype(F32)
    qn = jnp.concatenate([_rms(q[:, g * HEAD_DIM:(g + 1) * HEAD_DIM], qg).astype(BF16) for g in range(GROUP)], axis=0)
    o = _softmax_pv(qn, k_all, v_all, bias_ref[0, 0].reshape(GROUP * n_meta, 2 * BLOCK))
    for g in range(GROUP):
        o_ref[:, g * HEAD_DIM:(g + 1) * HEAD_DIM] = o[g * n_meta:(g + 1) * n_meta]


def _attention_meta(z, zm, bias_met, q_g, k_g, *, seq_chunks, n_meta, d_model, k_col, v_col):
    n_seq = len(seq_chunks)
    kv_heads = d_model // (GROUP * HEAD_DIM)
    qw = GROUP * HEAD_DIM
    kb, vb = k_col // HEAD_DIM, v_col // HEAD_DIM

    def first_chunk(s):
        c = 0
        for i, (s0, _) in enumerate(seq_chunks):
            c = c + jnp.where(s == i, s0, 0)
        return c

    in_specs = [
        pl.BlockSpec((1, HEAD_DIM), lambda s, h: (0, 0)),
        pl.BlockSpec((1, HEAD_DIM), lambda s, h: (0, 0)),
        pl.BlockSpec((1, 1, GROUP, n_meta, 2 * BLOCK), lambda s, h: (h, 0, 0, 0, 0)),
        pl.BlockSpec((n_meta, qw), lambda s, h: (s, h)),
        pl.BlockSpec((BLOCK, HEAD_DIM), lambda s, h: (first_chunk(s), kb + h)),
        pl.BlockSpec((BLOCK, HEAD_DIM), lambda s, h: (first_chunk(s), vb + h)),
        pl.BlockSpec((n_meta, HEAD_DIM), lambda s, h: (s, kb + h)),
        pl.BlockSpec((n_meta, HEAD_DIM), lambda s, h: (s, vb + h)),
    ]
    return pl.pallas_call(
        functools.partial(_attn_meta_kernel, n_meta=n_meta, scale=HEAD_DIM ** -0.5),
        grid=(n_seq, kv_heads),
        in_specs=in_specs,
        out_specs=pl.BlockSpec((n_meta, qw), lambda s, h: (s, h)),
        out_shape=jax.ShapeDtypeStruct((n_seq * n_meta, d_model), F32),
        scratch_shapes=[pltpu.VMEM((BLOCK, HEAD_DIM), BF16), pltpu.VMEM((BLOCK, HEAD_DIM), BF16)],
        compiler_params=_params(("parallel", "parallel"), 1 << 20),
        name="attention_meta",
    )(q_g.reshape(1, HEAD_DIM), k_g.reshape(1, HEAD_DIM), bias_met, zm, z, z, zm, zm)


def _merge_math(attn, gb, gc, hc, ga, gcv, u_prev, u_next, cw, na, nc):
    u = gc * hc
    rows = u.shape[0]
    ridx = lax.broadcasted_iota(jnp.int32, u.shape, 0)
    up = jnp.where(ridx == 0, u_prev, pltpu.roll(u, 1, axis=0))
    un = jnp.where(ridx == rows - 1, u_next, pltpu.roll(u, rows - 1, axis=0))
    conv = gb * (cw[0:1] * up + cw[1:2] * u + cw[2:3] * un)
    return jax.nn.sigmoid(ga) * _rms(attn, na) + jax.nn.sigmoid(gcv) * _rms(conv, nc)


def _merge_kernel(attn_ref, gb_ref, gc_ref, hc_ref, ga_ref, gcv_ref, gcp_ref, hcp_ref, gcn_ref, hcn_ref,
                  gcm_ref, hcm_ref, cw_ref, na_ref, nc_ref, o_ref, conv_scr, *, seq_chunks):
    c = pl.program_id(0)
    is_first = functools.reduce(jnp.logical_or, [c == s0 for s0, _ in seq_chunks])
    is_last = functools.reduce(jnp.logical_or, [c == s0 + n - 1 for s0, n in seq_chunks])
    last = HALO_ROWS - 1
    rows, d = o_ref.shape
    n_tiles = d // V7X_LANES
    ridx = lax.broadcasted_iota(jnp.int32, (rows, V7X_LANES), 0)

    def cols(ct):
        return pl.ds(pl.multiple_of(ct * V7X_LANES, V7X_LANES), V7X_LANES)

    def f32(ref, sl):
        return ref[:, sl].astype(F32)

    def conv_pass(ct, carry):
        ss_attn, ss_conv = carry
        sl = cols(ct)
        u = f32(gc_ref, sl) * f32(hc_ref, sl)
        u_prev_tok = (f32(gcp_ref, sl) * f32(hcp_ref, sl))[last:last + 1]
        u_prev_meta = (f32(gcm_ref, sl) * f32(hcm_ref, sl))[last:last + 1]
        u_prev = jnp.where(is_first, u_prev_meta, u_prev_tok)
        u_next_tok = (f32(gcn_ref, sl) * f32(hcn_ref, sl))[0:1]
        u_next = jnp.where(is_last, jnp.zeros_like(u_next_tok), u_next_tok)
        up = jnp.where(ridx == 0, u_prev, pltpu.roll(u, 1, axis=0))
        un = jnp.where(ridx == rows - 1, u_next, pltpu.roll(u, rows - 1, axis=0))
        cw = cw_ref[:, sl]
        conv = f32(gb_ref, sl) * (cw[0:1] * up + cw[1:2] * u + cw[2:3] * un)
        conv_scr[:, sl] = conv
        attn = attn_ref[:, sl]
        return ss_attn + attn * attn, ss_conv + conv * conv

    zero = jnp.zeros((rows, V7X_LANES), F32)
    ss_attn, ss_conv = lax.fori_loop(0, n_tiles, conv_pass, (zero, zero), unroll=2)
    r_attn = lax.rsqrt(jnp.sum(ss_attn, axis=-1, keepdims=True) / d + EPS)
    r_conv = lax.rsqrt(jnp.sum(ss_conv, axis=-1, keepdims=True) / d + EPS)

    def gate_pass(ct, carry):
        sl = cols(ct)
        a = attn_ref[:, sl] * r_attn * na_ref[:, sl]
        c = conv_scr[:, sl] * r_conv * nc_ref[:, sl]
        o_ref[:, sl] = (jax.nn.sigmoid(f32(ga_ref, sl)) * a + jax.nn.sigmoid(f32(gcv_ref, sl)) * c).astype(o_ref.dtype)
        return carry

    lax.fori_loop(0, n_tiles, gate_pass, 0, unroll=2)


def _merge(attn, z, zm, conv_w, norm_a, norm_c, *, seq_chunks, n_meta, cols):
    m, d = attn.shape
    n_chunks = m // BLOCK
    per_chunk = BLOCK // HALO_ROWS
    per_meta = n_meta // HALO_ROWS
    gb, gc, hc, ga, gcv = [c // d for c in cols]
    seq_of = functools.partial(_seq_of_chunk, seq_chunks=seq_chunks)

    def main(col):
        return pl.BlockSpec((BLOCK, d), lambda c: (c, col))

    def prev_rows(col):
        return pl.BlockSpec((HALO_ROWS, d), lambda c: (jnp.maximum(c * per_chunk - 1, 0), col))

    def next_rows(col):
        return pl.BlockSpec((HALO_ROWS, d), lambda c: (jnp.minimum(c + 1, n_chunks - 1) * per_chunk, col))

    def meta_rows(col):
        return pl.BlockSpec((HALO_ROWS, d), lambda c: (seq_of(c) * per_meta + per_meta - 1, col))

    def row(nrows):
        return pl.BlockSpec((nrows, d), lambda c: (0, 0))

    win = 6 * _nbytes((BLOCK, d), F32) + _nbytes((BLOCK, d), BF16) + 8 * _nbytes((HALO_ROWS, d), F32)
    return pl.pallas_call(
        functools.partial(_merge_kernel, seq_chunks=seq_chunks),
        grid=(n_chunks,),
        in_specs=[main(0), main(gb), main(gc), main(hc), main(ga), main(gcv),
                  prev_rows(gc), prev_rows(hc), next_rows(gc), next_rows(hc), meta_rows(gc), meta_rows(hc),
                  row(3), row(1), row(1)],
        out_specs=pl.BlockSpec((BLOCK, d), lambda c: (c, 0)),
        out_shape=jax.ShapeDtypeStruct((m, d), BF16),
        scratch_shapes=[pltpu.VMEM((BLOCK, d), F32)],
        compiler_params=_params(("parallel",), win),
        name="merge",
    )(attn, z, z, z, z, z, z, z, z, z, zm, zm, conv_w, norm_a.reshape(1, d), norm_c.reshape(1, d))


def _merge_meta_kernel(attn_ref, gb_ref, gc_ref, hc_ref, ga_ref, gcv_ref, gcn_ref, hcn_ref,
                       cw_ref, na_ref, nc_ref, o_ref):
    f32 = lambda ref: ref[...].astype(F32)
    u_next = (f32(gcn_ref) * f32(hcn_ref))[0:1]
    u_prev = jnp.zeros_like(u_next)
    o_ref[...] = _merge_math(attn_ref[...], f32(gb_ref), f32(gc_ref), f32(hc_ref), f32(ga_ref), f32(gcv_ref),
                             u_prev, u_next, cw_ref[...], na_ref[...], nc_ref[...]).astype(o_ref.dtype)


def _merge_meta(attn_m, z, zm, conv_w, norm_a, norm_c, *, seq_chunks, n_meta, cols):
    mm, d = attn_m.shape
    per_chunk = BLOCK // HALO_ROWS
    gb, gc, hc, ga, gcv = [c // d for c in cols]

    def first_rows(s):
        r = 0
        for i, (s0, _) in enumerate(seq_chunks):
            r = r + jnp.where(s == i, s0 * per_chunk, 0)
        return r

    def main(col):
        return pl.BlockSpec((n_meta, d), lambda s: (s, col))

    def next_rows(col):
        return pl.BlockSpec((HALO_ROWS, d), lambda s: (first_rows(s), col))

    def row(nrows):
        return pl.BlockSpec((nrows, d), lambda s: (0, 0))

    win = 7 * _nbytes((n_meta, d), F32) + 2 * _nbytes((HALO_ROWS, d), F32)
    return pl.pallas_call(
        _merge_meta_kernel,
        grid=(len(seq_chunks),),
        in_specs=[main(0), main(gb), main(gc), main(hc), main(ga), main(gcv), next_rows(gc), next_rows(hc),
                  row(3), row(1), row(1)],
        out_specs=pl.BlockSpec((n_meta, d), lambda s: (s, 0)),
        out_shape=jax.ShapeDtypeStruct((mm, d), BF16),
        compiler_params=_params(("parallel",), win),
        name="merge_meta",
    )(attn_m, zm, zm, zm, zm, zm, z, z, conv_w, norm_a.reshape(1, d), norm_c.reshape(1, d))


def kernel(x_prompt, x_sample, meta_tokens, rel_bias, norm1_g, w_in, q_norm_g, k_norm_g, attn_sink, conv_w,
           branch_norm_a, branch_norm_c, w_out, norm2_g, w_ffn_gate, w_ffn_up, w_ffn_down):
    d = x_prompt.shape[-1]
    n_meta = meta_tokens.shape[0]
    n_buckets = rel_bias.shape[0]
    depth, _, in_dim = w_in.shape
    ffn = w_ffn_gate.shape[-1]
    kv_dim = (in_dim - 6 * d) // 2
    assert d % (GROUP * HEAD_DIM) == 0 and kv_dim == d // GROUP and attn_sink.shape[1] * HEAD_DIM == d
    assert n_meta % HALO_ROWS == 0 and n_meta < BLOCK

    groups = (x_prompt, x_sample)
    seq_chunks = []
    for xg in groups:
        assert xg.shape[1] % BLOCK == 0 and xg.shape[1] >= 2 * BLOCK
        for _ in range(xg.shape[0]):
            start = seq_chunks[-1][0] + seq_chunks[-1][1] if seq_chunks else 0
            seq_chunks.append((start, xg.shape[1] // BLOCK))
    seq_chunks = tuple(seq_chunks)
    n_seq = len(seq_chunks)

    x_parts = [xg.reshape(-1, d) for xg in groups]
    xm = jnp.tile(meta_tokens.astype(F32), (n_seq, 1))

    cols = tuple(d * i for i in range(1, 6))
    k_col, v_col = 6 * d, 6 * d + kv_dim
    ffn_tile = 512
    ffn_p = -(-ffn // ffn_tile) * ffn_tile
    out_tile = _divisor_tile(d, 512, V7X_LANES)
    in_tile = kv_dim // 2
    n_q, n_kv, n_rest = d // in_tile, 2 * kv_dim // in_tile, 5 * d // in_tile

    def regroup(j):
        return jnp.where(j < n_q, j, jnp.where(j < n_q + n_rest, j + n_kv, j - n_rest))

    bkt_tok, bkt_met = _bucket_tables(n_meta, n_buckets)

    geo = dict(seq_chunks=seq_chunks, n_meta=n_meta)
    xb, ss = _prep(x_parts)
    xmb, ssm = _prep([xm])
    for l in range(depth):
        with_meta = l < depth - 1
        table = jnp.concatenate([rel_bias, attn_sink[l][None]], axis=0)
        z, zm = _wmm(xb, xmb, [w_in], l, in_dim, in_tile, BF16, src_block=regroup, norm=(norm1_g[l], ss, ssm),
                     bm_cap=2048)
        attn = _attention(z, zm, _bias_tiles(table, bkt_tok, token_variants=True), q_norm_g[l], k_norm_g[l],
                          d_model=d, k_col=k_col, v_col=v_col, **geo)
        merged = _merge(attn, z, zm, conv_w[l], branch_norm_a[l], branch_norm_c[l], cols=cols, **geo)
        merged_m = None
        if with_meta:
            attn_m = _attention_meta(z, zm, _bias_tiles(table, bkt_met), q_norm_g[l], k_norm_g[l],
                                     d_model=d, k_col=k_col, v_col=v_col, **geo)
            merged_m = _merge_meta(attn_m, z, zm, conv_w[l], branch_norm_a[l], branch_norm_c[l], cols=cols, **geo)
        h, hb, hss, *hm = _wmm(merged, merged_m, [w_out], l, d, out_tile, F32, res=x_parts,
                               auxres=xm if with_meta else None, emit_stats=True)
        hm, hmb, hmss = hm if with_meta else (None, None, None)
        t, *tm, wd_b = _wmm(hb, hmb, [w_ffn_gate, w_ffn_up], l, ffn_p, ffn_tile, BF16, act=True,
                            side=(w_ffn_down, ffn_p), norm=(norm2_g[l], hss, hmss))
        if with_meta:
            x, xb, ss = _matmul_res(t, wd_b, h, emit_stats=True, k_real=ffn)
            x_parts = [x]
            xm, xmb, ssm = _matmul_res(tm[0], wd_b, hm, emit_stats=True, k_real=ffn)

    outs = []
    row = 0
    for xg in groups:
        nrows = xg.shape[0] * xg.shape[1]
        outs.append(_matmul_res(t, wd_b, h, row, nrows, k_real=ffn)[0].reshape(xg.shape))
        row += nrows
    return tuple(outs)
```

```python
import functools
import math

import jax
import jax.numpy as jnp
from jax import lax
from jax.experimental import pallas as pl
from jax.experimental.pallas import tpu as pltpu

HEAD_DIM = 128
GROUP = 4
BLOCK = 128
MAX_DISTANCE = 128
EPS = 1e-6
LOG2E = math.log2(math.e)

V7X_VMEM_BYTES = 64 << 20
V7X_LANES = 128
BF16_SUBLANES = 16
F32_SUBLANES = 8
HALO_ROWS = BF16_SUBLANES
VMEM_SLACK_BYTES = 10 << 20
A_RING_SLOTS = 3

F32 = jnp.float32
BF16 = jnp.bfloat16


def _params(dims, window_bytes):
    limit = min(2 * window_bytes + VMEM_SLACK_BYTES, V7X_VMEM_BYTES - (6 << 20))
    return pltpu.CompilerParams(dimension_semantics=dims, vmem_limit_bytes=int(limit))


def _divisor_tile(n, cap, unit):
    if n <= cap:
        return n
    t = (cap // unit) * unit
    while t >= unit:
        if n % t == 0:
            return t
        t -= unit
    raise ValueError(f"no tile for {n}")


def _nbytes(shape, dtype):
    return math.prod(shape) * jnp.dtype(dtype).itemsize


def _rms(x, g):
    ms = jnp.mean(x * x, axis=-1, keepdims=True)
    return x * lax.rsqrt(ms + EPS) * g


def _row_sumsq(x):
    return jnp.broadcast_to(jnp.sum(x * x, axis=-1, keepdims=True), (x.shape[0], V7X_LANES))


def _rstd(ss_ref, width):
    ss = ss_ref[0]
    for p in range(1, ss_ref.shape[0]):
        ss = ss + ss_ref[p]
    return lax.rsqrt(ss / width + EPS)


def _prep_kernel(*refs, starts):
    x_refs, o_ref, ss_ref = refs[:len(starts)], refs[-2], refs[-1]
    i = pl.program_id(0)
    ends = starts[1:] + (None,)
    for x_ref, lo, hi in zip(x_refs, starts, ends):
        in_seg = i >= lo if hi is None else jnp.logical_and(i >= lo, i < hi)

        @pl.when(in_seg)
        def _():
            x = x_ref[...]
            o_ref[...] = x.astype(o_ref.dtype)
            ss_ref[0] = _row_sumsq(x)


def _prep(xs):
    d = xs[0].shape[1]
    m = sum(x.shape[0] for x in xs)
    bm = _divisor_tile(math.gcd(*[x.shape[0] for x in xs]), 256, BF16_SUBLANES)
    blocks = [x.shape[0] // bm for x in xs]
    starts = tuple(sum(blocks[:s]) for s in range(len(xs)))
    in_specs = [pl.BlockSpec((bm, d), lambda i, start=start, nblk=nblk: (jnp.clip(i - start, 0, nblk - 1), 0))
                for start, nblk in zip(starts, blocks)]
    return pl.pallas_call(
        functools.partial(_prep_kernel, starts=starts),
        grid=(m // bm,),
        in_specs=in_specs,
        out_specs=[pl.BlockSpec((bm, d), lambda i: (i, 0)), pl.BlockSpec((1, bm, V7X_LANES), lambda i: (0, i, 0))],
        out_shape=[jax.ShapeDtypeStruct((m, d), BF16), jax.ShapeDtypeStruct((1, m, V7X_LANES), F32)],
        compiler_params=_params(("parallel",), len(xs) * _nbytes((bm, d), F32) + _nbytes((bm, d), BF16)),
        name="prep",
    )(*xs)


def _wmm_kernel(*refs, n_w, res_starts, has_aux, has_side, has_norm, emit_stats, act, n_tiles, rows, src_cols,
                side_rows, ring):
    it = iter(refs)
    a_ref = next(it)
    aux_ref = next(it) if has_aux else None
    w_refs = [next(it) for _ in range(n_w)]
    gain_ref = next(it) if has_norm else None
    ss_ref = next(it) if has_norm else None
    auxss_ref = next(it) if has_norm and has_aux else None
    res_refs = [next(it) for _ in res_starts]
    auxres_ref = next(it) if res_starts and has_aux else None
    side_ref = next(it) if has_side else None
    o_ref = next(it)
    ob_ref, oss_ref = (next(it), next(it)) if emit_stats else (None, None)
    oaux_ref = next(it) if has_aux else None
    oauxb_ref, oauxss_ref = (next(it), next(it)) if emit_stats and has_aux else (None, None)
    oside_ref = next(it) if has_side else None
    w_bufs = (next(it), next(it))
    k_dim = w_bufs[0].shape[1]
    g = pl.program_id(0)
    i = pl.program_id(1)
    bn = w_bufs[0].shape[-1]

    if ring:
        ring_ref, sem_ref = next(it), next(it)
        n_inner = pl.num_programs(1)
        step = g * n_inner + i

        def a_copy(target):
            slot = lax.rem(target, A_RING_SLOTS)
            row0 = pl.multiple_of(lax.rem(target, n_inner) * ring_ref.shape[1], ring_ref.shape[1])
            return pltpu.make_async_copy(a_ref.at[pl.ds(row0, ring_ref.shape[1])], ring_ref.at[slot], sem_ref.at[slot])

        ahead = step + A_RING_SLOTS - 1

        @pl.when(jnp.logical_and(ahead >= n_inner, ahead < (n_tiles + 1) * n_inner))
        def _():
            a_copy(ahead).start()

    if has_side:
        w = side_ref[...]
        row = g * w.shape[0] + lax.broadcasted_iota(jnp.int32, w.shape, 0)
        oside_ref[...] = jnp.where(row < side_rows, w, 0.0).astype(oside_ref.dtype)

    def stage(buf):
        row0 = pl.multiple_of(i * rows, rows)
        for t in range(n_w):
            w = w_refs[t][...]
            if has_norm:
                w = w * jnp.tile(gain_ref[...], (1, bn // V7X_LANES))
            if src_cols % bn:
                col = jnp.minimum(g, n_tiles - 1) * bn + lax.broadcasted_iota(jnp.int32, w.shape, 1)
                w = jnp.where(col < src_cols, w, 0.0)
            buf[t, pl.ds(row0, rows), :] = w.astype(buf.dtype)

    def apply(buf, ncols, a, ss, res, out, out_b, out_ss):
        ys = [jnp.dot(a, buf[t, :, 0:ncols], preferred_element_type=F32) for t in range(n_w)]
        if has_norm:
            rstd = jnp.tile(_rstd(ss, k_dim), (1, ncols // V7X_LANES))
            ys = [y * rstd for y in ys]
        y = ys[0] * jax.nn.sigmoid(ys[0]) * ys[1] if act else ys[0]
        y = y if res is None else res + y
        out[:, 0:ncols] = y.astype(out.dtype)
        if ncols < bn:
            out[:, ncols:] = jnp.zeros((y.shape[0], bn - ncols), out.dtype)
        if emit_stats:
            out_b[...] = y.astype(out_b.dtype)
            out_ss[0] = _row_sumsq(y)

    def compute(buf, ncols):
        res = None
        if res_starts:
            res = res_refs[0][...]
            for start, ref in zip(res_starts[1:], res_refs[1:]):
                res = jnp.where(i >= start, ref[...], res)
        if ring:
            a_copy(step).wait()
            a = ring_ref[lax.rem(step, A_RING_SLOTS)]
        else:
            a = a_ref[...]
        apply(buf, ncols, a, ss_ref, res, o_ref, ob_ref, oss_ref)
        if has_aux:
            @pl.when(i == 0)
            def _():
                apply(buf, ncols, aux_ref[...], auxss_ref, auxres_ref[...] if res_starts else None,
                      oaux_ref, oauxb_ref, oauxss_ref)

    @pl.when(g == 0)
    def _():
        stage(w_bufs[0])

    last_cols = src_cols % bn
    full_steps = n_tiles - 1 if last_cols else n_tiles
    for parity in range(2):
        @pl.when(jnp.logical_and(jnp.logical_and(g > 0, g <= full_steps), g % 2 == parity))
        def _():
            stage(w_bufs[parity])
            compute(w_bufs[1 - parity], bn)

    if last_cols:
        assert last_cols % V7X_LANES == 0 and not res_starts and not emit_stats

        @pl.when(g == n_tiles)
        def _():
            compute(w_bufs[(n_tiles - 1) % 2], last_cols)


def _wmm(a, aux, ws, layer, n_out, bn, out_dtype, *, src_block=None, act=False, res=None, auxres=None, side=None,
         norm=None, emit_stats=False, ring=False):
    m, k = a.shape
    src_cols = ws[0].shape[2]
    bm = _divisor_tile(m, 1024, BF16_SUBLANES)
    ni, n_tiles = m // bm, n_out // bn
    rows = k // ni
    assert k % ni == 0 and rows % BF16_SUBLANES == 0 and n_out % bn == 0
    assert src_block is None or src_cols % bn == 0
    src = src_block if src_block is not None else (lambda j: j)
    res = list(res) if res is not None else []
    has_aux, has_side, has_norm = aux is not None, side is not None, norm is not None
    ma = aux.shape[0] if has_aux else 0
    res_blocks = [r.shape[0] // bm for r in res]
    assert all(r.shape[0] % bm == 0 for r in res) and sum(res_blocks) == (ni if res else 0)
    res_starts = tuple(sum(res_blocks[:s]) for s in range(len(res)))

    def row_blk(g, i):
        return jnp.where(g == 0, 0, i)

    def col_blk(g):
        return jnp.maximum(g - 1, 0)

    def last_tile(g):
        return jnp.minimum(g, n_tiles - 1)

    assert not ring or ni >= A_RING_SLOTS - 1
    in_specs = [pl.BlockSpec(memory_space=pl.ANY) if ring else pl.BlockSpec((bm, k), lambda g, i: (row_blk(g, i), 0))]
    args = [a]
    if has_aux:
        in_specs.append(pl.BlockSpec((ma, k), lambda g, i: (0, 0)))
        args.append(aux)
    for w in ws:
        in_specs.append(pl.BlockSpec((None, rows, bn), lambda g, i: (layer, i, src(last_tile(g)))))
        args.append(w)
    norm_bytes = 0
    if has_norm:
        gain, ss, ss_aux = norm
        in_specs.append(pl.BlockSpec((rows, V7X_LANES), lambda g, i: (i, 0)))
        args.append(jnp.broadcast_to(gain[:, None], (k, V7X_LANES)))
        in_specs.append(pl.BlockSpec((ss.shape[0], bm, V7X_LANES), lambda g, i: (0, row_blk(g, i), 0)))
        args.append(ss)
        if has_aux:
            in_specs.append(pl.BlockSpec((ss_aux.shape[0], ma, V7X_LANES), lambda g, i: (0, 0, 0)))
            args.append(ss_aux)
        norm_bytes = _nbytes((rows + ss.shape[0] * (bm + ma), V7X_LANES), F32)
    for start, nblk, r in zip(res_starts, res_blocks, res):
        in_specs.append(pl.BlockSpec(
            (bm, bn), lambda g, i, start=start, nblk=nblk: (jnp.clip(row_blk(g, i) - start, 0, nblk - 1), col_blk(g))))
        args.append(r)
    if res and has_aux:
        in_specs.append(pl.BlockSpec((ma, bn), lambda g, i: (0, col_blk(g))))
        args.append(auxres)
    out_specs, out_shape = [], []

    def add_outputs(nrows, blk_rows, row_index):
        out_specs.append(pl.BlockSpec((blk_rows, bn), lambda g, i: (row_index(g, i), col_blk(g))))
        out_shape.append(jax.ShapeDtypeStruct((nrows, n_out), out_dtype))
        if emit_stats:
            out_specs.append(pl.BlockSpec((blk_rows, bn), lambda g, i: (row_index(g, i), col_blk(g))))
            out_shape.append(jax.ShapeDtypeStruct((nrows, n_out), BF16))
            out_specs.append(pl.BlockSpec((1, blk_rows, V7X_LANES), lambda g, i: (col_blk(g), row_index(g, i), 0)))
            out_shape.append(jax.ShapeDtypeStruct((n_tiles, nrows, V7X_LANES), F32))

    add_outputs(m, bm, row_blk)
    if has_aux:
        add_outputs(ma, ma, lambda g, i: 0)
    side_rows = side_bytes = 0
    if has_side:
        side_w, side_out_rows = side
        side_rows, side_cols = side_w.shape[1:]
        sr, sc = side_out_rows // n_tiles, side_cols // ni
        assert side_out_rows % n_tiles == 0 and side_cols % ni == 0 and sr % BF16_SUBLANES == 0 and sc % V7X_LANES == 0
        in_specs.append(pl.BlockSpec((None, sr, sc), lambda g, i: (layer, last_tile(g), i)))
        args.append(side_w)
        out_specs.append(pl.BlockSpec((sr, sc), lambda g, i: (g, i)))
        out_shape.append(jax.ShapeDtypeStruct((side_out_rows + sr, side_cols), BF16))
        side_bytes = _nbytes((sr, sc), F32) + _nbytes((sr, sc), BF16)
    win = (_nbytes((bm * (not ring) + ma, k), BF16) + len(ws) * _nbytes((rows, bn), F32) + side_bytes + norm_bytes
           + _nbytes((bm + ma, bn), out_dtype) + len(res) * _nbytes((bm + ma, bn), F32)
           + emit_stats * (_nbytes((bm + ma, bn), BF16) + _nbytes((bm + ma, V7X_LANES), F32)))
    scratch = _nbytes((2, len(ws), k, bn), BF16) + ring * _nbytes((A_RING_SLOTS, bm, k), BF16)
    return pl.pallas_call(
        functools.partial(_wmm_kernel, n_w=len(ws), res_starts=res_starts, has_aux=has_aux, has_side=has_side,
                          has_norm=has_norm, emit_stats=emit_stats, act=act, n_tiles=n_tiles, rows=rows,
                          src_cols=src_cols, side_rows=side_rows, ring=ring),
        grid=(n_tiles + 1, ni),
        in_specs=in_specs,
        out_specs=out_specs,
        out_shape=out_shape,
        scratch_shapes=[pltpu.VMEM((len(ws), k, bn), BF16), pltpu.VMEM((len(ws), k, bn), BF16)] + ring * [
            pltpu.VMEM((A_RING_SLOTS, bm, k), BF16), pltpu.SemaphoreType.DMA((A_RING_SLOTS,))],
        compiler_params=_params(("arbitrary", "arbitrary"), win + scratch // 2),
        name="wmm",
    )(*args)


def _mm_res_kernel(a_ref, b_ref, r_ref, o_ref, *stats, last_k):
    j, kk = pl.program_id(1), pl.program_id(2)
    n_k = pl.num_programs(2)

    @pl.when(kk == 0)
    def _():
        o_ref[...] = r_ref[...]

    if last_k == a_ref.shape[1]:
        o_ref[...] += jnp.dot(a_ref[...], b_ref[...], preferred_element_type=F32)
    else:
        @pl.when(kk < n_k - 1)
        def _():
            o_ref[...] += jnp.dot(a_ref[...], b_ref[...], preferred_element_type=F32)

        @pl.when(kk == n_k - 1)
        def _():
            o_ref[...] += jnp.dot(a_ref[:, 0:last_k], b_ref[0:last_k, :], preferred_element_type=F32)
    if stats:
        ob_ref, ss_ref = stats

        @pl.when(kk == pl.num_programs(2) - 1)
        def _():
            y = o_ref[...]
            ob_ref[...] = y.astype(ob_ref.dtype)

            @pl.when(j == 0)
            def _():
                ss_ref[0] = _row_sumsq(y)

            @pl.when(j > 0)
            def _():
                ss_ref[0] += _row_sumsq(y)


def _matmul_res(a, b, res, row0=0, nrows=None, emit_stats=False, k_real=None):
    m, k = a.shape
    n = b.shape[1]
    nrows = m if nrows is None else nrows
    bm = _divisor_tile(nrows, 1024, BF16_SUBLANES)
    bn = _divisor_tile(n, 1024, V7X_LANES)
    bk = _divisor_tile(k, 4096, V7X_LANES)
    assert row0 % bm == 0
    off = row0 // bm
    win = _nbytes((bm, bk), BF16) + _nbytes((bk, bn), BF16) + 2 * _nbytes((bm, bn), F32)
    last_k = bk if k_real is None else k_real - (k // bk - 1) * bk
    assert 0 < last_k <= bk and last_k % V7X_LANES == 0
    return pl.pallas_call(
        functools.partial(_mm_res_kernel, last_k=last_k),
        grid=(nrows // bm, n // bn, k // bk),
        in_specs=[pl.BlockSpec((bm, bk), lambda i, j, kk: (i + off, kk)),
                  pl.BlockSpec((bk, bn), lambda i, j, kk: (kk, j)),
                  pl.BlockSpec((bm, bn), lambda i, j, kk: (i + off, j))],
        out_specs=[pl.BlockSpec((bm, bn), lambda i, j, kk: (i, j))] + emit_stats * [
            pl.BlockSpec((bm, bn), lambda i, j, kk: (i, j)),
            pl.BlockSpec((1, bm, V7X_LANES), lambda i, j, kk: (0, i, 0))],
        out_shape=[jax.ShapeDtypeStruct((nrows, n), F32)] + emit_stats * [
            jax.ShapeDtypeStruct((nrows, n), BF16), jax.ShapeDtypeStruct((1, nrows, V7X_LANES), F32)],
        compiler_params=_params(("parallel", "arbitrary", "arbitrary"), win + _nbytes((bm, bn), BF16)),
        name="matmul_res",
    )(a, b, res)


def _t5_bucket(rel, n_buckets):
    half = n_buckets // 2
    exact = half // 2
    n = jnp.abs(rel)
    n_f = jnp.maximum(n, 1).astype(F32)
    large = exact + (jnp.log(n_f / exact) / math.log(MAX_DISTANCE / exact) * (half - exact)).astype(jnp.int32)
    large = jnp.minimum(large, half - 1)
    return jnp.where(rel > 0, half, 0) + jnp.where(n < exact, n, large)


def _bucket_tables(n_meta, n_buckets):
    qi = jnp.arange(BLOCK)[:, None]
    sj = jnp.arange(3 * BLOCK)[None, :]
    rel_band = sj - BLOCK - qi
    band = jnp.where(jnp.abs(rel_band) <= BLOCK, _t5_bucket(rel_band, n_buckets), -1)
    sink = jnp.full((BLOCK, 1), n_buckets, jnp.int32)
    pad = jnp.full((BLOCK, BLOCK - n_meta - 1), -1, jnp.int32)
    mk = jnp.arange(n_meta)[None, :]
    meta_first = _t5_bucket(mk - (n_meta + qi), n_buckets)
    meta_far = jnp.full((BLOCK, n_meta), n_buckets // 2 - 1, jnp.int32)
    tok = jnp.concatenate([band, meta_far, sink, pad, meta_first, sink, pad], axis=1)[None].astype(jnp.int32)

    mq = jnp.arange(n_meta)[:, None]
    tk = jnp.arange(BLOCK)[None, :]
    rel_tok = n_meta + tk - mq
    mband = jnp.where(jnp.abs(rel_tok) <= BLOCK, _t5_bucket(rel_tok, n_buckets), -1)
    mmeta = _t5_bucket(mk - mq, n_buckets)
    met = jnp.concatenate([mband, mmeta, sink[:n_meta], pad[:n_meta]], axis=1).astype(jnp.int32)
    return tok, met[None]


def _bias_kernel(tab_ref, bkt_ref, o_ref, *, n_ids, token_variants):
    h = pl.program_id(0)
    bkt = bkt_ref[...]
    acc = jnp.full(bkt.shape, -jnp.inf, F32)
    for b in range(n_ids):
        acc = jnp.where(bkt == b, tab_ref[b, h], acc)
    acc = acc * LOG2E
    if token_variants:
        band, far, first = acc[0, :, :3 * BLOCK], acc[0, :, 3 * BLOCK:4 * BLOCK], acc[0, :, 4 * BLOCK:]
        masked = jnp.full((BLOCK, BLOCK), -jnp.inf, F32)
        o_ref[0, 0, 0] = jnp.concatenate([band, far], axis=1)
        o_ref[0, 1, 0] = jnp.concatenate([masked, band[:, BLOCK:], first], axis=1)
        o_ref[0, 2, 0] = jnp.concatenate([band[:, :2 * BLOCK], masked, far], axis=1)
    else:
        o_ref[0, :, 0] = acc


def _bias_tiles(table, bkt, token_variants=False):
    n_ids, n_heads = table.shape
    v, r, c = (3, BLOCK, 4 * BLOCK) if token_variants else bkt.shape
    return pl.pallas_call(
        functools.partial(_bias_kernel, n_ids=n_ids, token_variants=token_variants),
        grid=(n_heads,),
        in_specs=[pl.BlockSpec(memory_space=pltpu.SMEM),
                  pl.BlockSpec(bkt.shape, lambda h: (0, 0, 0))],
        out_specs=pl.BlockSpec((1, v, 1, r, c), lambda h: (h // GROUP, 0, h % GROUP, 0, 0)),
        out_shape=jax.ShapeDtypeStruct((n_heads // GROUP, v, GROUP, r, c), F32),
        compiler_params=_params(("parallel",), 2 * _nbytes(bkt.shape, F32)),
        name="bias_tiles",
    )(table, bkt)


def _softmax_pv(q, k, v, bias2):
    s = lax.dot_general(q, k, (((1,), (1,)), ((), ())), preferred_element_type=F32) + bias2
    m = jnp.max(s, axis=-1, keepdims=True)
    p = jnp.exp2(s - m)
    denom = jnp.sum(p, axis=-1, keepdims=True)
    return jnp.dot(p.astype(BF16), v, preferred_element_type=F32) * (1.0 / denom)


def _fill_meta_keys(km_scr, vm_scr, km_ref, vm_ref, kg, n_meta):
    km_scr[...] = jnp.zeros(km_scr.shape, BF16)
    vm_scr[...] = jnp.zeros(vm_scr.shape, BF16)
    km_scr[0:n_meta] = _rms(km_ref[...].astype(F32), kg).astype(BF16)
    vm_scr[0:n_meta] = vm_ref[...].astype(BF16)


def _attn_kernel(qg_ref, kg_ref, bias_ref, q_ref, kp_ref, kc_ref, kn_ref, vp_ref, vc_ref, vn_ref,
                 km_ref, vm_ref, o_ref, q_scr, k_scr, v_scr, km_scr, vm_scr, *, chunks, seq_chunks, n_meta, scale):
    r = pl.program_id(1)
    qg = qg_ref[...] * (scale * LOG2E)
    kg = kg_ref[...]
    body = chunks * BLOCK
    stack = GROUP * BLOCK

    k_scr[0:BLOCK] = _rms(kp_ref[...].astype(F32), kg).astype(BF16)
    k_scr[BLOCK:BLOCK + body] = _rms(kc_ref[...].astype(F32), kg).astype(BF16)
    k_scr[BLOCK + body:2 * BLOCK + body] = _rms(kn_ref[...].astype(F32), kg).astype(BF16)
    v_scr[0:BLOCK] = vp_ref[...].astype(BF16)
    v_scr[BLOCK:BLOCK + body] = vc_ref[...].astype(BF16)
    v_scr[BLOCK + body:2 * BLOCK + body] = vn_ref[...].astype(BF16)
    _fill_meta_keys(km_scr, vm_scr, km_ref, vm_ref, kg, n_meta)
    for g in range(GROUP):
        qn = _rms(q_ref[:, g * HEAD_DIM:(g + 1) * HEAD_DIM].astype(F32), qg).astype(BF16)
        for c in range(chunks):
            q_scr[(c * GROUP + g) * BLOCK:(c * GROUP + g + 1) * BLOCK] = qn[c * BLOCK:(c + 1) * BLOCK]
    k_meta = km_scr[...]
    v_meta = vm_scr[...]

    def chunk(cc, carry):
        gch = r * chunks + cc
        is_first = functools.reduce(jnp.logical_or, [gch == s0 for s0, _ in seq_chunks])
        is_last = functools.reduce(jnp.logical_or, [gch == s0 + n - 1 for s0, n in seq_chunks])
        var = jnp.where(is_first, 1, jnp.where(is_last, 2, 0))
        row0 = pl.multiple_of(cc * BLOCK, BLOCK)
        k_all = jnp.concatenate([k_scr[pl.ds(row0, 3 * BLOCK), :], k_meta], axis=0)
        v_all = jnp.concatenate([v_scr[pl.ds(row0, 3 * BLOCK), :], v_meta], axis=0)
        q = q_scr[pl.ds(pl.multiple_of(cc * stack, stack), stack), :]
        bias = bias_ref[0, var].reshape(stack, 4 * BLOCK)
        o = _softmax_pv(q, k_all, v_all, bias)
        for g in range(GROUP):
            o_ref[pl.ds(row0, BLOCK), g * HEAD_DIM:(g + 1) * HEAD_DIM] = o[g * BLOCK:(g + 1) * BLOCK]
        return carry

    lax.fori_loop(0, chunks, chunk, 0, unroll=True)


def _seq_of_chunk(c, seq_chunks):
    s = 0
    for s0, _ in seq_chunks[1:]:
        s = s + (c >= s0).astype(jnp.int32)
    return s


def _attention(z, zm, bias_tok, q_g, k_g, *, seq_chunks, n_meta, d_model, k_col, v_col):
    m = z.shape[0]
    n_chunks = m // BLOCK
    kv_heads = d_model // (GROUP * HEAD_DIM)
    chunks = math.gcd(16, *[n for _, n in seq_chunks])
    body = chunks * BLOCK
    qw = GROUP * HEAD_DIM
    kb, vb = k_col // HEAD_DIM, v_col // HEAD_DIM
    seq_of = functools.partial(_seq_of_chunk, seq_chunks=seq_chunks)

    def halo_prev(h, r):
        return jnp.maximum(r * chunks - 1, 0)

    def halo_next(h, r):
        return jnp.minimum(r * chunks + chunks, n_chunks - 1)

    in_specs = [
        pl.BlockSpec((1, HEAD_DIM), lambda h, r: (0, 0)),
        pl.BlockSpec((1, HEAD_DIM), lambda h, r: (0, 0)),
        pl.BlockSpec((1, 3, GROUP, BLOCK, 4 * BLOCK), lambda h, r: (h, 0, 0, 0, 0)),
        pl.BlockSpec((body, qw), lambda h, r: (r, h)),
        pl.BlockSpec((BLOCK, HEAD_DIM), lambda h, r: (halo_prev(h, r), kb + h)),
        pl.BlockSpec((body, HEAD_DIM), lambda h, r: (r, kb + h)),
        pl.BlockSpec((BLOCK, HEAD_DIM), lambda h, r: (halo_next(h, r), kb + h)),
        pl.BlockSpec((BLOCK, HEAD_DIM), lambda h, r: (halo_prev(h, r), vb + h)),
        pl.BlockSpec((body, HEAD_DIM), lambda h, r: (r, vb + h)),
        pl.BlockSpec((BLOCK, HEAD_DIM), lambda h, r: (halo_next(h, r), vb + h)),
        pl.BlockSpec((n_meta, HEAD_DIM), lambda h, r: (seq_of(r * chunks), kb + h)),
        pl.BlockSpec((n_meta, HEAD_DIM), lambda h, r: (seq_of(r * chunks), vb + h)),
    ]
    win = (_nbytes((3, GROUP, BLOCK, 4 * BLOCK), F32) + 2 * _nbytes((body, qw), F32)
           + 4 * _nbytes((body + 2 * BLOCK, HEAD_DIM), F32))
    return pl.pallas_call(
        functools.partial(_attn_kernel, chunks=chunks, seq_chunks=seq_chunks, n_meta=n_meta,
                          scale=HEAD_DIM ** -0.5),
        grid=(kv_heads, m // body),
        in_specs=in_specs,
        out_specs=pl.BlockSpec((body, qw), lambda h, r: (r, h)),
        out_shape=jax.ShapeDtypeStruct((m, d_model), F32),
        scratch_shapes=[pltpu.VMEM((body * GROUP, HEAD_DIM), BF16),
                        pltpu.VMEM((body + 2 * BLOCK, HEAD_DIM), BF16),
                        pltpu.VMEM((body + 2 * BLOCK, HEAD_DIM), BF16),
                        pltpu.VMEM((BLOCK, HEAD_DIM), BF16),
                        pltpu.VMEM((BLOCK, HEAD_DIM), BF16)],
        compiler_params=_params(("parallel", "arbitrary"), win),
        name="attention",
    )(q_g.reshape(1, HEAD_DIM), k_g.reshape(1, HEAD_DIM), bias_tok, z, z, z, z, z, z, z, zm, zm)


def _attn_meta_kernel(qg_ref, kg_ref, bias_ref, q_ref, k1_ref, v1_ref, km_ref, vm_ref, o_ref,
                      km_scr, vm_scr, *, n_meta, scale):
    qg = qg_ref[...] * (scale * LOG2E)
    kg = kg_ref[...]
    _fill_meta_keys(km_scr, vm_scr, km_ref, vm_ref, kg, n_meta)
    k_all = jnp.concatenate([_rms(k1_ref[...].astype(F32), kg).astype(BF16), km_scr[...]], axis=0)
    v_all = jnp.concatenate([v1_ref[...].astype(BF16), vm_scr[...]], axis=0)
    q = q_ref[...].astype(F32)
    qn = jnp.concatenate([_rms(q[:, g * HEAD_DIM:(g + 1) * HEAD_DIM], qg).astype(BF16) for g in range(GROUP)], axis=0)
    o = _softmax_pv(qn, k_all, v_all, bias_ref[0, 0].reshape(GROUP * n_meta, 2 * BLOCK))
    for g in range(GROUP):
        o_ref[:, g * HEAD_DIM:(g + 1) * HEAD_DIM] = o[g * n_meta:(g + 1) * n_meta]


def _attention_meta(z, zm, bias_met, q_g, k_g, *, seq_chunks, n_meta, d_model, k_col, v_col):
    n_seq = len(seq_chunks)
    kv_heads = d_model // (GROUP * HEAD_DIM)
    qw = GROUP * HEAD_DIM
    kb, vb = k_col // HEAD_DIM, v_col // HEAD_DIM

    def first_chunk(s):
        c = 0
        for i, (s0, _) in enumerate(seq_chunks):
            c = c + jnp.where(s == i, s0, 0)
        return c

    in_specs = [
        pl.BlockSpec((1, HEAD_DIM), lambda s, h: (0, 0)),
        pl.BlockSpec((1, HEAD_DIM), lambda s, h: (0, 0)),
        pl.BlockSpec((1, 1, GROUP, n_meta, 2 * BLOCK), lambda s, h: (h, 0, 0, 0, 0)),
        pl.BlockSpec((n_meta, qw), lambda s, h: (s, h)),
        pl.BlockSpec((BLOCK, HEAD_DIM), lambda s, h: (first_chunk(s), kb + h)),
        pl.BlockSpec((BLOCK, HEAD_DIM), lambda s, h: (first_chunk(s), vb + h)),
        pl.BlockSpec((n_meta, HEAD_DIM), lambda s, h: (s, kb + h)),
        pl.BlockSpec((n_meta, HEAD_DIM), lambda s, h: (s, vb + h)),
    ]
    return pl.pallas_call(
        functools.partial(_attn_meta_kernel, n_meta=n_meta, scale=HEAD_DIM ** -0.5),
        grid=(n_seq, kv_heads),
        in_specs=in_specs,
        out_specs=pl.BlockSpec((n_meta, qw), lambda s, h: (s, h)),
        out_shape=jax.ShapeDtypeStruct((n_seq * n_meta, d_model), F32),
        scratch_shapes=[pltpu.VMEM((BLOCK, HEAD_DIM), BF16), pltpu.VMEM((BLOCK, HEAD_DIM), BF16)],
        compiler_params=_params(("parallel", "parallel"), 1 << 20),
        name="attention_meta",
    )(q_g.reshape(1, HEAD_DIM), k_g.reshape(1, HEAD_DIM), bias_met, zm, z, z, zm, zm)


def _merge_math(attn, gb, gc, hc, ga, gcv, u_prev, u_next, cw, na, nc):
    u = gc * hc
    rows = u.shape[0]
    ridx = lax.broadcasted_iota(jnp.int32, u.shape, 0)
    up = jnp.where(ridx == 0, u_prev, pltpu.roll(u, 1, axis=0))
    un = jnp.where(ridx == rows - 1, u_next, pltpu.roll(u, rows - 1, axis=0))
    conv = gb * (cw[0:1] * up + cw[1:2] * u + cw[2:3] * un)
    return jax.nn.sigmoid(ga) * _rms(attn, na) + jax.nn.sigmoid(gcv) * _rms(conv, nc)


def _merge_kernel(attn_ref, gb_ref, gc_ref, hc_ref, ga_ref, gcv_ref, gcp_ref, hcp_ref, gcn_ref, hcn_ref,
                  gcm_ref, hcm_ref, cw_ref, na_ref, nc_ref, o_ref, conv_scr, *, seq_chunks):
    c = pl.program_id(0)
    is_first = functools.reduce(jnp.logical_or, [c == s0 for s0, _ in seq_chunks])
    is_last = functools.reduce(jnp.logical_or, [c == s0 + n - 1 for s0, n in seq_chunks])
    last = HALO_ROWS - 1
    rows, d = o_ref.shape
    n_tiles = d // V7X_LANES
    ridx = lax.broadcasted_iota(jnp.int32, (rows, V7X_LANES), 0)

    def cols(ct):
        return pl.ds(pl.multiple_of(ct * V7X_LANES, V7X_LANES), V7X_LANES)

    def f32(ref, sl):
        return ref[:, sl].astype(F32)

    def conv_pass(ct, carry):
        ss_attn, ss_conv = carry
        sl = cols(ct)
        u = f32(gc_ref, sl) * f32(hc_ref, sl)
        u_prev_tok = (f32(gcp_ref, sl) * f32(hcp_ref, sl))[last:last + 1]
        u_prev_meta = (f32(gcm_ref, sl) * f32(hcm_ref, sl))[last:last + 1]
        u_prev = jnp.where(is_first, u_prev_meta, u_prev_tok)
        u_next_tok = (f32(gcn_ref, sl) * f32(hcn_ref, sl))[0:1]
        u_next = jnp.where(is_last, jnp.zeros_like(u_next_tok), u_next_tok)
        up = jnp.where(ridx == 0, u_prev, pltpu.roll(u, 1, axis=0))
        un = jnp.where(ridx == rows - 1, u_next, pltpu.roll(u, rows - 1, axis=0))
        cw = cw_ref[:, sl]
        conv = f32(gb_ref, sl) * (cw[0:1] * up + cw[1:2] * u + cw[2:3] * un)
        conv_scr[:, sl] = conv
        attn = attn_ref[:, sl]
        return ss_attn + attn * attn, ss_conv + conv * conv

    zero = jnp.zeros((rows, V7X_LANES), F32)
    ss_attn, ss_conv = lax.fori_loop(0, n_tiles, conv_pass, (zero, zero), unroll=2)
    r_attn = lax.rsqrt(jnp.sum(ss_attn, axis=-1, keepdims=True) / d + EPS)
    r_conv = lax.rsqrt(jnp.sum(ss_conv, axis=-1, keepdims=True) / d + EPS)

    def gate_pass(ct, carry):
        sl = cols(ct)
        a = attn_ref[:, sl] * r_attn * na_ref[:, sl]
        c = conv_scr[:, sl] * r_conv * nc_ref[:, sl]
        o_ref[:, sl] = (jax.nn.sigmoid(f32(ga_ref, sl)) * a + jax.nn.sigmoid(f32(gcv_ref, sl)) * c).astype(o_ref.dtype)
        return carry

    lax.fori_loop(0, n_tiles, gate_pass, 0, unroll=2)


def _merge(attn, z, zm, conv_w, norm_a, norm_c, *, seq_chunks, n_meta, cols):
    m, d = attn.shape
    n_chunks = m // BLOCK
    per_chunk = BLOCK // HALO_ROWS
    per_meta = n_meta // HALO_ROWS
    gb, gc, hc, ga, gcv = [c // d for c in cols]
    seq_of = functools.partial(_seq_of_chunk, seq_chunks=seq_chunks)

    def main(col):
        return pl.BlockSpec((BLOCK, d), lambda c: (c, col))

    def prev_rows(col):
        return pl.BlockSpec((HALO_ROWS, d), lambda c: (jnp.maximum(c * per_chunk - 1, 0), col))

    def next_rows(col):
        return pl.BlockSpec((HALO_ROWS, d), lambda c: (jnp.minimum(c + 1, n_chunks - 1) * per_chunk, col))

    def meta_rows(col):
        return pl.BlockSpec((HALO_ROWS, d), lambda c: (seq_of(c) * per_meta + per_meta - 1, col))

    def row(nrows):
        return pl.BlockSpec((nrows, d), lambda c: (0, 0))

    win = 6 * _nbytes((BLOCK, d), F32) + _nbytes((BLOCK, d), BF16) + 8 * _nbytes((HALO_ROWS, d), F32)
    return pl.pallas_call(
        functools.partial(_merge_kernel, seq_chunks=seq_chunks),
        grid=(n_chunks,),
        in_specs=[main(0), main(gb), main(gc), main(hc), main(ga), main(gcv),
                  prev_rows(gc), prev_rows(hc), next_rows(gc), next_rows(hc), meta_rows(gc), meta_rows(hc),
                  row(3), row(1), row(1)],
        out_specs=pl.BlockSpec((BLOCK, d), lambda c: (c, 0)),
        out_shape=jax.ShapeDtypeStruct((m, d), BF16),
        scratch_shapes=[pltpu.VMEM((BLOCK, d), F32)],
        compiler_params=_params(("parallel",), win),
        name="merge",
    )(attn, z, z, z, z, z, z, z, z, z, zm, zm, conv_w, norm_a.reshape(1, d), norm_c.reshape(1, d))


def _merge_meta_kernel(attn_ref, gb_ref, gc_ref, hc_ref, ga_ref, gcv_ref, gcn_ref, hcn_ref,
                       cw_ref, na_ref, nc_ref, o_ref):
    f32 = lambda ref: ref[...].astype(F32)
    u_next = (f32(gcn_ref) * f32(hcn_ref))[0:1]
    u_prev = jnp.zeros_like(u_next)
    o_ref[...] = _merge_math(attn_ref[...], f32(gb_ref), f32(gc_ref), f32(hc_ref), f32(ga_ref), f32(gcv_ref),
                             u_prev, u_next, cw_ref[...], na_ref[...], nc_ref[...]).astype(o_ref.dtype)


def _merge_meta(attn_m, z, zm, conv_w, norm_a, norm_c, *, seq_chunks, n_meta, cols):
    mm, d = attn_m.shape
    per_chunk = BLOCK // HALO_ROWS
    gb, gc, hc, ga, gcv = [c // d for c in cols]

    def first_rows(s):
        r = 0
        for i, (s0, _) in enumerate(seq_chunks):
            r = r + jnp.where(s == i, s0 * per_chunk, 0)
        return r

    def main(col):
        return pl.BlockSpec((n_meta, d), lambda s: (s, col))

    def next_rows(col):
        return pl.BlockSpec((HALO_ROWS, d), lambda s: (first_rows(s), col))

    def row(nrows):
        return pl.BlockSpec((nrows, d), lambda s: (0, 0))

    win = 7 * _nbytes((n_meta, d), F32) + 2 * _nbytes((HALO_ROWS, d), F32)
    return pl.pallas_call(
        _merge_meta_kernel,
        grid=(len(seq_chunks),),
        in_specs=[main(0), main(gb), main(gc), main(hc), main(ga), main(gcv), next_rows(gc), next_rows(hc),
                  row(3), row(1), row(1)],
        out_specs=pl.BlockSpec((n_meta, d), lambda s: (s, 0)),
        out_shape=jax.ShapeDtypeStruct((mm, d), BF16),
        compiler_params=_params(("parallel",), win),
        name="merge_meta",
    )(attn_m, zm, zm, zm, zm, zm, z, z, conv_w, norm_a.reshape(1, d), norm_c.reshape(1, d))


def kernel(x_prompt, x_sample, meta_tokens, rel_bias, norm1_g, w_in, q_norm_g, k_norm_g, attn_sink, conv_w,
           branch_norm_a, branch_norm_c, w_out, norm2_g, w_ffn_gate, w_ffn_up, w_ffn_down):
    d = x_prompt.shape[-1]
    n_meta = meta_tokens.shape[0]
    n_buckets = rel_bias.shape[0]
    depth, _, in_dim = w_in.shape
    ffn = w_ffn_gate.shape[-1]
    kv_dim = (in_dim - 6 * d) // 2
    assert d % (GROUP * HEAD_DIM) == 0 and kv_dim == d // GROUP and attn_sink.shape[1] * HEAD_DIM == d
    assert n_meta % HALO_ROWS == 0 and n_meta < BLOCK

    groups = (x_prompt, x_sample)
    seq_chunks = []
    for xg in groups:
        assert xg.shape[1] % BLOCK == 0 and xg.shape[1] >= 2 * BLOCK
        for _ in range(xg.shape[0]):
            start = seq_chunks[-1][0] + seq_chunks[-1][1] if seq_chunks else 0
            seq_chunks.append((start, xg.shape[1] // BLOCK))
    seq_chunks = tuple(seq_chunks)
    n_seq = len(seq_chunks)

    x_parts = [xg.reshape(-1, d) for xg in groups]
    xm = jnp.tile(meta_tokens.astype(F32), (n_seq, 1))

    cols = tuple(d * i for i in range(1, 6))
    k_col, v_col = 6 * d, 6 * d + kv_dim
    ffn_tile = 512
    ffn_p = -(-ffn // ffn_tile) * ffn_tile
    out_tile = _divisor_tile(d, 512, V7X_LANES)
    n_q, n_rest = d // kv_dim, 5 * d // kv_dim

    def regroup(j):
        return jnp.where(j < n_q, j, jnp.where(j < n_q + n_rest, j + 2, j - n_rest))

    bkt_tok, bkt_met = _bucket_tables(n_meta, n_buckets)

    geo = dict(seq_chunks=seq_chunks, n_meta=n_meta)
    xb, ss = _prep(x_parts)
    xmb, ssm = _prep([xm])
    for l in range(depth):
        with_meta = l < depth - 1
        table = jnp.concatenate([rel_bias, attn_sink[l][None]], axis=0)
        z, zm = _wmm(xb, xmb, [w_in], l, in_dim, kv_dim, BF16, src_block=regroup, norm=(norm1_g[l], ss, ssm), ring=True)
        attn = _attention(z, zm, _bias_tiles(table, bkt_tok, token_variants=True), q_norm_g[l], k_norm_g[l],
                          d_model=d, k_col=k_col, v_col=v_col, **geo)
        merged = _merge(attn, z, zm, conv_w[l], branch_norm_a[l], branch_norm_c[l], cols=cols, **geo)
        merged_m = None
        if with_meta:
            attn_m = _attention_meta(z, zm, _bias_tiles(table, bkt_met), q_norm_g[l], k_norm_g[l],
                                     d_model=d, k_col=k_col, v_col=v_col, **geo)
            merged_m = _merge_meta(attn_m, z, zm, conv_w[l], branch_norm_a[l], branch_norm_c[l], cols=cols, **geo)
        h, hb, hss, *hm = _wmm(merged, merged_m, [w_out], l, d, out_tile, F32, res=x_parts,
                               auxres=xm if with_meta else None, emit_stats=True, ring=True)
        hm, hmb, hmss = hm if with_meta else (None, None, None)
        t, *tm, wd_b = _wmm(hb, hmb, [w_ffn_gate, w_ffn_up], l, ffn_p, ffn_tile, BF16, act=True,
                            side=(w_ffn_down, ffn_p), norm=(norm2_g[l], hss, hmss))
        if with_meta:
            x, xb, ss = _matmul_res(t, wd_b, h, emit_stats=True, k_real=ffn)
            x_parts = [x]
            xm, xmb, ssm = _matmul_res(tm[0], wd_b, hm, emit_stats=True, k_real=ffn)

    outs = []
    row = 0
    for xg in groups:
        nrows = xg.shape[0] * xg.shape[1]
        outs.append(_matmul_res(t, wd_b, h, row, nrows, k_real=ffn)[0].reshape(xg.shape))
        row += nrows
    return tuple(outs)
```
